```python
import jax, jax.numpy as jnp
from jax import lax
import numpy as np

D_MODEL = 2048
BATCH = 8
SEQ = 4096
DEPTH = 2

N_META = 16
HEAD_DIM = 128
N_HEADS_SB = D_MODEL // (2 * HEAD_DIM)
N_HEADS_FOX = D_MODEL // (2 * HEAD_DIM)
W_SB = N_HEADS_SB * HEAD_DIM
W_FOX = N_HEADS_FOX * HEAD_DIM
W_MIX = W_SB + W_FOX
N_IN = 3 * W_SB + 3 * W_FOX + N_HEADS_FOX
D_FF = 11 * D_MODEL // 4
CONV_WIDTH = 3
Q_BLOCK = 128
EPS = 1e-6

kernel_name = "hymba_stickbreak_fox_convffn"


def rms_norm(x, g):
    xf = x.astype(jnp.float32)
    y = xf * lax.rsqrt(jnp.mean(xf * xf, axis=-1, keepdims=True) + EPS)
    return (y * g.astype(jnp.float32)).astype(x.dtype)


def block_bounds():
    bounds = [(0, N_META)]
    for i in range(SEQ // Q_BLOCK):
        bounds.append((N_META + i * Q_BLOCK, N_META + (i + 1) * Q_BLOCK))
    return bounds


def stick_breaking_attention(q, k, v):
    scale = HEAD_DIM ** -0.5
    outs = []
    for qs, qe in block_bounds():
        z = jnp.einsum('bqhd,bkhd->bhqk', q[:, qs:qe], k[:, :qe]).astype(jnp.float32) * scale
        t_pos = jnp.arange(qs, qe)[:, None]
        s_pos = jnp.arange(qe)[None, :]
        before = s_pos < t_pos
        log_keep = jnp.where(before, -jax.nn.softplus(z), 0.0)
        log_keep_between = lax.cumsum(log_keep, axis=3, reverse=True) - log_keep
        a = jnp.where(before, jnp.exp(jax.nn.log_sigmoid(z) + log_keep_between), 0.0)
        outs.append(jnp.einsum('bhqk,bkhd->bqhd', a.astype(v.dtype), v[:, :qe]))
    return jnp.concatenate(outs, axis=1)


def forgetting_attention(q, k, v, log_f):
    scale = HEAD_DIM ** -0.5
    c = jnp.transpose(jnp.cumsum(log_f, axis=1), (0, 2, 1))
    outs = []
    for qs, qe in block_bounds():
        logits = jnp.einsum('bqhd,bkhd->bhqk', q[:, qs:qe], k[:, :qe]).astype(jnp.float32) * scale
        logits = logits + (c[:, :, qs:qe, None] - c[:, :, None, :qe])
        t_pos = jnp.arange(qs, qe)[:, None]
        s_pos = jnp.arange(qe)[None, :]
        logits = jnp.where(s_pos <= t_pos, logits, -jnp.inf)
        p = jax.nn.softmax(logits, axis=-1)
        outs.append(jnp.einsum('bhqk,bkhd->bqhd', p.astype(v.dtype), v[:, :qe]))
    return jnp.concatenate(outs, axis=1)


def causal_depthwise_conv(a, w, bias):
    c = a.shape[-1]
    out = lax.conv_general_dilated(
        a, w.astype(a.dtype)[:, None, :], window_strides=(1,),
        padding=[(CONV_WIDTH - 1, 0)], dimension_numbers=('NWC', 'WIO', 'NWC'),
        feature_group_count=c)
    return out + bias.astype(a.dtype)


def hybrid_layer(h, g_mix_pre, w_in, b_f, g_sb, g_fox, w_out, g_mix_post,
                 g_ffn_pre, w_up, conv_w, conv_b, w_down, g_ffn_post):
    b, l, _ = h.shape
    u = rms_norm(h, g_mix_pre)
    proj = u @ w_in
    splits = [W_SB, 2 * W_SB, 3 * W_SB, 3 * W_SB + W_FOX, 3 * W_SB + 2 * W_FOX, 3 * W_SB + 3 * W_FOX]
    q_sb, k_sb, v_sb, q_fx, k_fx, v_fx, f_logit = jnp.split(proj, splits, axis=-1)
    heads_sb = lambda t: t.reshape(b, l, N_HEADS_SB, HEAD_DIM)
    heads_fx = lambda t: t.reshape(b, l, N_HEADS_FOX, HEAD_DIM)
    o_sb = stick_breaking_attention(heads_sb(q_sb), heads_sb(k_sb), heads_sb(v_sb))
    log_f = jax.nn.log_sigmoid((f_logit + b_f).astype(jnp.float32))
    o_fx = forgetting_attention(heads_fx(q_fx), heads_fx(k_fx), heads_fx(v_fx), log_f)
    o_sb = rms_norm(o_sb, g_sb).reshape(b, l, W_SB)
    o_fx = rms_norm(o_fx, g_fox).reshape(b, l, W_FOX)
    mix = jnp.concatenate([o_sb, o_fx], axis=-1) @ w_out
    h = h + rms_norm(mix, g_mix_post)
    u = rms_norm(h, g_ffn_pre)
    a = causal_depthwise_conv(u @ w_up, conv_w, conv_b)
    gate, up = jnp.split(a, [D_FF], axis=-1)
    ff = (jax.nn.silu(gate) * up) @ w_down
    return h + rms_norm(ff, g_ffn_post)


def _fwd_setup_inputs(seed: int = 0) -> dict:
    key = jax.random.key(seed)
    ks = jax.random.split(key, 16)
    f32 = jnp.float32
    nrm = lambda k, shape, s: jax.random.normal(k, shape, f32) * s
    gain = lambda k, shape: 1.0 + 0.02 * jax.random.normal(k, shape, f32)
    return {
        "x": nrm(ks[0], (BATCH, SEQ, D_MODEL), 1.0),
        "meta": nrm(ks[1], (N_META, D_MODEL), 1.0),
        "g_mix_pre": gain(ks[2], (DEPTH, D_MODEL)),
        "w_in": nrm(ks[3], (DEPTH, D_MODEL, N_IN), D_MODEL ** -0.5),
        "b_f": 3.0 + 0.5 * jax.random.normal(ks[4], (DEPTH, N_HEADS_FOX), f32),
        "g_sb": gain(ks[5], (DEPTH, N_HEADS_SB, HEAD_DIM)),
        "g_fox": gain(ks[6], (DEPTH, N_HEADS_FOX, HEAD_DIM)),
        "w_out": nrm(ks[7], (DEPTH, W_MIX, D_MODEL), W_MIX ** -0.5),
        "g_mix_post": gain(ks[8], (DEPTH, D_MODEL)),
        "g_ffn_pre": gain(ks[9], (DEPTH, D_MODEL)),
        "w_up": nrm(ks[10], (DEPTH, D_MODEL, 2 * D_FF), D_MODEL ** -0.5),
        "conv_w": nrm(ks[11], (DEPTH, CONV_WIDTH, 2 * D_FF), CONV_WIDTH ** -0.5),
        "conv_b": nrm(ks[12], (DEPTH, 2 * D_FF), 0.01),
        "w_down": nrm(ks[13], (DEPTH, D_FF, D_MODEL), D_FF ** -0.5),
        "g_ffn_post": gain(ks[14], (DEPTH, D_MODEL)),
    }


def _fwd_reference(x, meta, g_mix_pre, w_in, b_f, g_sb, g_fox, w_out, g_mix_post,
              g_ffn_pre, w_up, conv_w, conv_b, w_down, g_ffn_post):
    b = x.shape[0]
    meta_b = jnp.broadcast_to(meta[None].astype(x.dtype), (b, N_META, D_MODEL))
    h = jnp.concatenate([meta_b, x], axis=1)
    for i in range(DEPTH):
        h = hybrid_layer(h, g_mix_pre[i], w_in[i], b_f[i], g_sb[i], g_fox[i], w_out[i],
                         g_mix_post[i], g_ffn_pre[i], w_up[i], conv_w[i], conv_b[i],
                         w_down[i], g_ffn_post[i])
    return h[:, N_META:]


import jax as _jax
import jax.numpy as _jnp

TWIN_FORMAT = 'train_step'
FWD_PARAMS = ['x', 'meta', 'g_mix_pre', 'w_in', 'b_f', 'g_sb', 'g_fox', 'w_out', 'g_mix_post', 'g_ffn_pre', 'w_up', 'conv_w', 'conv_b', 'w_down', 'g_ffn_post']
TWIN_WEIGHTS = ['meta', 'g_mix_pre', 'w_in', 'b_f', 'g_sb', 'g_fox', 'w_out', 'g_mix_post', 'g_ffn_pre', 'w_up', 'conv_w', 'conv_b', 'w_down', 'g_ffn_post']
TWIN_DIFF_INPUT = 'x'
TWIN_INPUTS = ['x', 'meta', 'g_mix_pre', 'w_in', 'b_f', 'g_sb', 'g_fox', 'w_out', 'g_mix_post', 'g_ffn_pre', 'w_up', 'conv_w', 'conv_b', 'w_down', 'g_ffn_post', 'loss_target', 'm_meta', 'm_g_mix_pre', 'm_w_in', 'm_b_f', 'm_g_sb', 'm_g_fox', 'm_w_out', 'm_g_mix_post', 'm_g_ffn_pre', 'm_w_up', 'm_conv_w', 'm_conv_b', 'm_w_down', 'm_g_ffn_post', 'v_meta', 'v_g_mix_pre', 'v_w_in', 'v_b_f', 'v_g_sb', 'v_g_fox', 'v_w_out', 'v_g_mix_post', 'v_g_ffn_pre', 'v_w_up', 'v_conv_w', 'v_conv_b', 'v_w_down', 'v_g_ffn_post']
TWIN_OUTPUTS = ['loss', 'grad_x', 'grad_meta', 'grad_g_mix_pre', 'grad_w_in', 'grad_b_f', 'grad_g_sb', 'grad_g_fox', 'grad_w_out', 'grad_g_mix_post', 'grad_g_ffn_pre', 'grad_w_up', 'grad_conv_w', 'grad_conv_b', 'grad_w_down', 'grad_g_ffn_post', 'delta_meta', 'delta_g_mix_pre', 'delta_w_in', 'delta_b_f', 'delta_g_sb', 'delta_g_fox', 'delta_w_out', 'delta_g_mix_post', 'delta_g_ffn_pre', 'delta_w_up', 'delta_conv_w', 'delta_conv_b', 'delta_w_down', 'delta_g_ffn_post', 'new_m_meta', 'new_m_g_mix_pre', 'new_m_w_in', 'new_m_b_f', 'new_m_g_sb', 'new_m_g_fox', 'new_m_w_out', 'new_m_g_mix_post', 'new_m_g_ffn_pre', 'new_m_w_up', 'new_m_conv_w', 'new_m_conv_b', 'new_m_w_down', 'new_m_g_ffn_post', 'new_v_meta', 'new_v_g_mix_pre', 'new_v_w_in', 'new_v_b_f', 'new_v_g_sb', 'new_v_g_fox', 'new_v_w_out', 'new_v_g_mix_post', 'new_v_g_ffn_pre', 'new_v_w_up', 'new_v_conv_w', 'new_v_conv_b', 'new_v_w_down', 'new_v_g_ffn_post']
TWIN_LEAF_KINDS = {'loss': 'loss', 'grad_x': 'grad_x', 'grad_meta': 'grad_w', 'grad_g_mix_pre': 'grad_w', 'grad_w_in': 'grad_w', 'grad_b_f': 'grad_w', 'grad_g_sb': 'grad_w', 'grad_g_fox': 'grad_w', 'grad_w_out': 'grad_w', 'grad_g_mix_post': 'grad_w', 'grad_g_ffn_pre': 'grad_w', 'grad_w_up': 'grad_w', 'grad_conv_w': 'grad_w', 'grad_conv_b': 'grad_w', 'grad_w_down': 'grad_w', 'grad_g_ffn_post': 'grad_w', 'delta_meta': 'delta_w', 'delta_g_mix_pre': 'delta_w', 'delta_w_in': 'delta_w', 'delta_b_f': 'delta_w', 'delta_g_sb': 'delta_w', 'delta_g_fox': 'delta_w', 'delta_w_out': 'delta_w', 'delta_g_mix_post': 'delta_w', 'delta_g_ffn_pre': 'delta_w', 'delta_w_up': 'delta_w', 'delta_conv_w': 'delta_w', 'delta_conv_b': 'delta_w', 'delta_w_down': 'delta_w', 'delta_g_ffn_post': 'delta_w', 'new_m_meta': 'new_m', 'new_m_g_mix_pre': 'new_m', 'new_m_w_in': 'new_m', 'new_m_b_f': 'new_m', 'new_m_g_sb': 'new_m', 'new_m_g_fox': 'new_m', 'new_m_w_out': 'new_m', 'new_m_g_mix_post': 'new_m', 'new_m_g_ffn_pre': 'new_m', 'new_m_w_up': 'new_m', 'new_m_conv_w': 'new_m', 'new_m_conv_b': 'new_m', 'new_m_w_down': 'new_m', 'new_m_g_ffn_post': 'new_m', 'new_v_meta': 'new_v', 'new_v_g_mix_pre': 'new_v', 'new_v_w_in': 'new_v', 'new_v_b_f': 'new_v', 'new_v_g_sb': 'new_v', 'new_v_g_fox': 'new_v', 'new_v_w_out': 'new_v', 'new_v_g_mix_post': 'new_v', 'new_v_g_ffn_pre': 'new_v', 'new_v_w_up': 'new_v', 'new_v_conv_w': 'new_v', 'new_v_conv_b': 'new_v', 'new_v_w_down': 'new_v', 'new_v_g_ffn_post': 'new_v'}


def _forward(args):
    return _fwd_reference(*[args[k] for k in FWD_PARAMS])


def _output_shape():
    def fwd():
        inp = _fwd_setup_inputs(0)
        return _fwd_reference(*[inp[k] for k in FWD_PARAMS])
    out = _jax.eval_shape(fwd)
    return out.shape, out.dtype

N_MICROBATCH = 1
ADAM_LR = 0.001
ADAM_B1 = 0.9
ADAM_B2 = 0.999
ADAM_EPS = 1e-08
ADAM_WD = 0.01
ADAM_STEP = 10
PER_EXAMPLE_BATCH_AXIS = {'x': 0, 'loss_target': 0}
SHARED_INPUTS = []
_WEIGHT_DTYPES = {'meta': _jnp.float32, 'g_mix_pre': _jnp.float32, 'w_in': _jnp.float32, 'b_f': _jnp.float32, 'g_sb': _jnp.float32, 'g_fox': _jnp.float32, 'w_out': _jnp.float32, 'g_mix_post': _jnp.float32, 'g_ffn_pre': _jnp.float32, 'w_up': _jnp.float32, 'conv_w': _jnp.float32, 'conv_b': _jnp.float32, 'w_down': _jnp.float32, 'g_ffn_post': _jnp.float32}
MOMENT_SCALE = {'meta': 3.825084e-02, 'g_mix_pre': 5.651231e-01, 'w_in': 3.192816e-01, 'b_f': 2.548508e+00, 'g_sb': 4.341885e-01, 'g_fox': 5.209180e-01, 'w_out': 4.664568e-01, 'g_mix_post': 1.597657e+01, 'g_ffn_pre': 4.079744e-01, 'w_up': 1.714331e-01, 'conv_w': 1.804739e-01, 'conv_b': 2.820038e-01, 'w_down': 3.051295e-01, 'g_ffn_post': 1.598690e+01}


def _to_microbatches(a, axis):
    t = _jnp.moveaxis(a, axis, 0)
    t = t.reshape((N_MICROBATCH, t.shape[0] // N_MICROBATCH) + t.shape[1:])
    return _jnp.moveaxis(t, 1, axis + 1)


def setup_inputs(seed: int = 0) -> dict:
    inp = _fwd_setup_inputs(seed)
    key = _jax.random.fold_in(_jax.random.key(seed), 7919)
    shape, _ = _output_shape()
    out = dict(inp)
    out["loss_target"] = _jax.random.normal(_jax.random.fold_in(key, 0), shape, _jnp.float32)
    for i, name in enumerate(TWIN_WEIGHTS):
        w = inp[name].astype(_jnp.float32)
        if MOMENT_SCALE is None:
            s = _jnp.sqrt(_jnp.mean(_jnp.square(w)) + 1e-30)
        else:
            s = MOMENT_SCALE[name]
        km, kv = _jax.random.split(_jax.random.fold_in(key, i + 1))
        out[name] = w
        out["m_" + name] = s * _jax.random.normal(km, w.shape, _jnp.float32)
        out["v_" + name] = (s * s) * _jax.random.uniform(kv, w.shape, _jnp.float32, 0.5, 1.5)
    if N_MICROBATCH > 1:
        for name, axis in PER_EXAMPLE_BATCH_AXIS.items():
            out[name] = _to_microbatches(out[name], axis)
    return {'x': out['x'], 'meta': out['meta'], 'g_mix_pre': out['g_mix_pre'], 'w_in': out['w_in'], 'b_f': out['b_f'], 'g_sb': out['g_sb'], 'g_fox': out['g_fox'], 'w_out': out['w_out'], 'g_mix_post': out['g_mix_post'], 'g_ffn_pre': out['g_ffn_pre'], 'w_up': out['w_up'], 'conv_w': out['conv_w'], 'conv_b': out['conv_b'], 'w_down': out['w_down'], 'g_ffn_post': out['g_ffn_post'], 'loss_target': out['loss_target'], 'm_meta': out['m_meta'], 'm_g_mix_pre': out['m_g_mix_pre'], 'm_w_in': out['m_w_in'], 'm_b_f': out['m_b_f'], 'm_g_sb': out['m_g_sb'], 'm_g_fox': out['m_g_fox'], 'm_w_out': out['m_w_out'], 'm_g_mix_post': out['m_g_mix_post'], 'm_g_ffn_pre': out['m_g_ffn_pre'], 'm_w_up': out['m_w_up'], 'm_conv_w': out['m_conv_w'], 'm_conv_b': out['m_conv_b'], 'm_w_down': out['m_w_down'], 'm_g_ffn_post': out['m_g_ffn_post'], 'v_meta': out['v_meta'], 'v_g_mix_pre': out['v_g_mix_pre'], 'v_w_in': out['v_w_in'], 'v_b_f': out['v_b_f'], 'v_g_sb': out['v_g_sb'], 'v_g_fox': out['v_g_fox'], 'v_w_out': out['v_w_out'], 'v_g_mix_post': out['v_g_mix_post'], 'v_g_ffn_pre': out['v_g_ffn_pre'], 'v_w_up': out['v_w_up'], 'v_conv_w': out['v_conv_w'], 'v_conv_b': out['v_conv_b'], 'v_w_down': out['v_w_down'], 'v_g_ffn_post': out['v_g_ffn_post']}


def _loss(weights, diff, rest, loss_target):
    with _jax.named_scope("forward"):
        args = {**rest, TWIN_DIFF_INPUT: diff, **{k: w.astype(_WEIGHT_DTYPES[k]) for k, w in weights.items()}}
        y = _forward(args)
    with _jax.named_scope("loss_head"):
        err = _jnp.square(y.astype(_jnp.float32) - loss_target)
        return 0.5 * _jnp.sum(_jnp.mean(err, axis=-1)) if err.ndim else 0.5 * err


def _adamw(w, g, m, v):
    m = ADAM_B1 * m + (1.0 - ADAM_B1) * g
    v = ADAM_B2 * v + (1.0 - ADAM_B2) * _jnp.square(g)
    m_hat = m / (1.0 - ADAM_B1 ** ADAM_STEP)
    v_hat = v / (1.0 - ADAM_B2 ** ADAM_STEP)
    delta = -ADAM_LR * (m_hat / (_jnp.sqrt(v_hat) + ADAM_EPS) + ADAM_WD * w)
    return delta, m, v


def reference(x, meta, g_mix_pre, w_in, b_f, g_sb, g_fox, w_out, g_mix_post, g_ffn_pre, w_up, conv_w, conv_b, w_down, g_ffn_post, loss_target, m_meta, m_g_mix_pre, m_w_in, m_b_f, m_g_sb, m_g_fox, m_w_out, m_g_mix_post, m_g_ffn_pre, m_w_up, m_conv_w, m_conv_b, m_w_down, m_g_ffn_post, v_meta, v_g_mix_pre, v_w_in, v_b_f, v_g_sb, v_g_fox, v_w_out, v_g_mix_post, v_g_ffn_pre, v_w_up, v_conv_w, v_conv_b, v_w_down, v_g_ffn_post):
    given = dict(x=x, meta=meta, g_mix_pre=g_mix_pre, w_in=w_in, b_f=b_f, g_sb=g_sb, g_fox=g_fox, w_out=w_out, g_mix_post=g_mix_post, g_ffn_pre=g_ffn_pre, w_up=w_up, conv_w=conv_w, conv_b=conv_b, w_down=w_down, g_ffn_post=g_ffn_post, loss_target=loss_target, m_meta=m_meta, m_g_mix_pre=m_g_mix_pre, m_w_in=m_w_in, m_b_f=m_b_f, m_g_sb=m_g_sb, m_g_fox=m_g_fox, m_w_out=m_w_out, m_g_mix_post=m_g_mix_post, m_g_ffn_pre=m_g_ffn_pre, m_w_up=m_w_up, m_conv_w=m_conv_w, m_conv_b=m_conv_b, m_w_down=m_w_down, m_g_ffn_post=m_g_ffn_post, v_meta=v_meta, v_g_mix_pre=v_g_mix_pre, v_w_in=v_w_in, v_b_f=v_b_f, v_g_sb=v_g_sb, v_g_fox=v_g_fox, v_w_out=v_w_out, v_g_mix_post=v_g_mix_post, v_g_ffn_pre=v_g_ffn_pre, v_w_up=v_w_up, v_conv_w=v_conv_w, v_conv_b=v_conv_b, v_w_down=v_w_down, v_g_ffn_post=v_g_ffn_post)
    weights = {n: given[n] for n in TWIN_WEIGHTS}
    shared = {n: given[n] for n in SHARED_INPUTS}
    per_example = {n: given[n] for n in ['x']}
    grad_fn = _jax.value_and_grad(_loss, argnums=(0, 1))

    def one_microbatch(ex, loss_target):
        ex = dict(ex)
        diff = ex.pop(TWIN_DIFF_INPUT)
        return grad_fn(weights, diff, {**shared, **ex}, loss_target)

    if N_MICROBATCH == 1:
        loss, (grad_w, grad_x) = one_microbatch(per_example, given["loss_target"])
    else:
        def body(carry, xs):
            loss_sum, grad_sum = carry
            l_k, (gw_k, gx_k) = one_microbatch(xs[0], xs[1])
            with _jax.named_scope("update"):
                return (loss_sum + l_k, _jax.tree.map(_jnp.add, grad_sum, gw_k)), gx_k

        init = (_jnp.zeros((), _jnp.float32), _jax.tree.map(_jnp.zeros_like, weights))
        (loss, grad_w), grad_x = _jax.lax.scan(body, init, (per_example, given["loss_target"]))
    with _jax.named_scope("update"):
        delta_w, new_m, new_v = {}, {}, {}
        for n in TWIN_WEIGHTS:
            delta_w[n], new_m[n], new_v[n] = _adamw(weights[n], grad_w[n], given["m_" + n], given["v_" + n])
    return (loss, grad_x, *[grad_w[n] for n in TWIN_WEIGHTS], *[delta_w[n] for n in TWIN_WEIGHTS],
            *[new_m[n] for n in TWIN_WEIGHTS], *[new_v[n] for n in TWIN_WEIGHTS])
```

```python
import functools
import math
from typing import NamedTuple

import jax
import jax.numpy as jnp
from jax import lax
from jax.experimental import pallas as pl
from jax.experimental.pallas import tpu as pltpu

F32 = jnp.float32
MXU = jnp.bfloat16
HD = 128
LANES = 128
EPS = 1e-6
NEG = -1e30
ADAM_LR, ADAM_B1, ADAM_B2, ADAM_EPS, ADAM_WD, ADAM_STEP = 0.001, 0.9, 0.999, 1e-08, 0.01, 10
VMEM_LIMIT = 56 * 1024 * 1024
MESH = pl.DeviceIdType.MESH
NCHIP = 4

WEIGHTS = ['meta', 'g_mix_pre', 'w_in', 'b_f', 'g_sb', 'g_fox', 'w_out', 'g_mix_post', 'g_ffn_pre',
           'w_up', 'conv_w', 'conv_b', 'w_down', 'g_ffn_post']
BIG = ['w_in', 'w_out', 'w_up', 'w_down']
SMALL = [n for n in WEIGHTS if n not in BIG]

NT = (((1,), (1,)), ((), ()))
TN = (((0,), (0,)), ((), ()))
NN = (((1,), (0,)), ((), ()))


class Cfg(NamedTuple):
    D: int
    SEQ: int
    NMETA: int
    NH: int
    F: int
    LP: int
    tq: int
    tr: int
    FB: int
    DEPTH: int = 2

    @property
    def L(self): return self.SEQ + self.NMETA
    @property
    def WG(self): return self.NH * HD
    @property
    def WMIX(self): return 2 * self.WG
    @property
    def NQKV(self): return 6 * self.WG
    @property
    def N_IN(self): return self.NQKV + self.NH
    @property
    def NEXT(self): return self.NQKV + LANES
    @property
    def F2(self): return 2 * self.F


PROD = Cfg(D=2048, SEQ=4096, NMETA=16, NH=8, F=5632, LP=4224, tq=384, tr=192, FB=512)


def _tile(n, cap, mult=LANES):
    best = None
    for t in range(mult, min(n, cap) + 1, mult):
        if n % t == 0:
            best = t
    return best if best is not None else n


def _cp(sem):
    return pltpu.CompilerParams(dimension_semantics=sem, vmem_limit_bytes=VMEM_LIMIT)


def _split_dot(x, tri, pieces):
    acc, r = None, x
    for p in range(pieces):
        xp = r.astype(MXU)
        d = jnp.dot(xp, tri, preferred_element_type=F32)
        acc = d if acc is None else acc + d
        if p + 1 < pieces:
            r = r - xp.astype(F32)
    return acc


def _split_dot_left(tri, x, pieces):
    acc, r = None, x
    for p in range(pieces):
        xp = r.astype(MXU)
        d = jnp.dot(tri, xp, preferred_element_type=F32)
        acc = d if acc is None else acc + d
        if p + 1 < pieces:
            r = r - xp.astype(F32)
    return acc


def _softplus(z):
    return jnp.maximum(z, 0.0) + jnp.log1p(jnp.exp(-jnp.abs(z)))


def matmul(a, b, mode, out_dtype, name, tm_cap=1408, tn_cap=1024, tk_cap=8192):
    if mode == 'tn':
        K, M = a.shape
    else:
        M, K = a.shape
    N = b.shape[0] if mode == 'nt' else b.shape[1]
    tm, tn, tk = _tile(M, tm_cap, 8), _tile(N, tn_cap), _tile(K, tk_cap)
    nk = K // tk
    dn = {'nn': NN, 'nt': NT, 'tn': TN}[mode]

    def body(a_ref, b_ref, o_ref, *scratch):
        d = lax.dot_general(a_ref[...], b_ref[...], dn, preferred_element_type=F32)
        if nk == 1:
            o_ref[...] = d.astype(out_dtype)
        else:
            acc_ref, = scratch
            k = pl.program_id(2)

            @pl.when(k == 0)
            def _():
                acc_ref[...] = d

            @pl.when(k > 0)
            def _():
                acc_ref[...] += d

            @pl.when(k == nk - 1)
            def _():
                o_ref[...] = acc_ref[...].astype(out_dtype)

    a_spec = (pl.BlockSpec((tk, tm), lambda i, j, k: (k, i)) if mode == 'tn'
              else pl.BlockSpec((tm, tk), lambda i, j, k: (i, k)))
    b_spec = (pl.BlockSpec((tn, tk), lambda i, j, k: (j, k)) if mode == 'nt'
              else pl.BlockSpec((tk, tn), lambda i, j, k: (k, j)))
    return pl.pallas_call(
        body, name=name, out_shape=jax.ShapeDtypeStruct((M, N), out_dtype),
        grid=(M // tm, N // tn, nk), in_specs=[a_spec, b_spec],
        out_specs=pl.BlockSpec((tm, tn), lambda i, j, k: (i, j)),
        scratch_shapes=[] if nk == 1 else [pltpu.VMEM((tm, tn), F32)],
        compiler_params=_cp(("parallel", "parallel", "arbitrary")),
    )(a, b)


def _rstd(x):
    return lax.rsqrt(jnp.mean(x * x, axis=-1, keepdims=True) + EPS)


def pre_norm(cfg, h, g, name):
    LP, D, tr = cfg.LP, cfg.D, cfg.tr

    def body(h_ref, g_ref, u_ref):
        x = h_ref[...]
        u_ref[...] = ((x * _rstd(x)) * g_ref[...]).astype(MXU)

    row = pl.BlockSpec((tr, D), lambda i: (i, 0))
    vec = pl.BlockSpec((1, D), lambda i: (0, 0))
    return pl.pallas_call(body, name=name, out_shape=jax.ShapeDtypeStruct((LP, D), MXU), grid=(LP // tr,),
                          in_specs=[row, vec], out_specs=row, compiler_params=_cp(("parallel",)))(h, g)


def resid_norm(cfg, y, h, g_post, g_next, name):
    LP, D, tr = cfg.LP, cfg.D, cfg.tr

    def body(y_ref, h_ref, gp_ref, gn_ref, hn_ref, u_ref):
        yv = y_ref[...]
        hn = h_ref[...] + (yv * _rstd(yv)) * gp_ref[...]
        hn_ref[...] = hn
        u_ref[...] = ((hn * _rstd(hn)) * gn_ref[...]).astype(MXU)

    row = pl.BlockSpec((tr, D), lambda i: (i, 0))
    vec = pl.BlockSpec((1, D), lambda i: (0, 0))
    return pl.pallas_call(
        body, name=name,
        out_shape=(jax.ShapeDtypeStruct((LP, D), F32), jax.ShapeDtypeStruct((LP, D), MXU)),
        grid=(LP // tr,), in_specs=[row, row, vec, vec], out_specs=(row, row),
        compiler_params=_cp(("parallel",)))(y, h, g_post, g_next)


def loss_head(cfg, h, target, name):
    LP, D, tr = cfg.LP, cfg.D, cfg.tr
    lo, hi = cfg.NMETA, cfg.L

    def body(h_ref, t_ref, dh_ref, loss_ref):
        i = pl.program_id(0)
        rows = lax.broadcasted_iota(jnp.int32, (tr, D), 0) + i * tr
        diff = jnp.where((rows >= lo) & (rows < hi), h_ref[...] - t_ref[...], 0.0)
        dh_ref[...] = diff * (1.0 / D)
        part = 0.5 * jnp.sum(jnp.sum(diff * diff, axis=1, keepdims=True), axis=0, keepdims=True) * (1.0 / D)

        @pl.when(i == 0)
        def _():
            loss_ref[...] = jnp.zeros_like(loss_ref)

        loss_ref[...] += jnp.broadcast_to(part, loss_ref.shape)

    row = pl.BlockSpec((tr, D), lambda i: (i, 0))
    return pl.pallas_call(
        body, name=name,
        out_shape=(jax.ShapeDtypeStruct((LP, D), F32), jax.ShapeDtypeStruct((8, LANES), F32)),
        grid=(LP // tr,), in_specs=[row, row],
        out_specs=(row, pl.BlockSpec((8, LANES), lambda i: (0, 0))),
        compiler_params=_cp(("arbitrary",)))(h, target)


def norm_bwd(cfg, x, g, dy, dres, out_dtype, name):
    LP, D, tr = cfg.LP, cfg.D, cfg.tr
    has_res = dres is not None

    def body(*refs):
        if has_res:
            x_ref, g_ref, dy_ref, dres_ref, dx_ref, dg_ref = refs
        else:
            x_ref, g_ref, dy_ref, dx_ref, dg_ref = refs
        xv, dyv = x_ref[...], dy_ref[...]
        r = _rstd(xv)
        xhat = xv * r
        gdy = dyv * g_ref[...]
        dx = r * (gdy - xhat * jnp.mean(gdy * xhat, axis=-1, keepdims=True))
        if has_res:
            dx = dx + dres_ref[...]
        dx_ref[...] = dx.astype(out_dtype)

        @pl.when(pl.program_id(0) == 0)
        def _():
            dg_ref[...] = jnp.zeros_like(dg_ref)

        dg_ref[...] += jnp.sum(dyv * xhat, axis=0, keepdims=True)

    row = pl.BlockSpec((tr, D), lambda i: (i, 0))
    vec = pl.BlockSpec((1, D), lambda i: (0, 0))
    ins = [x, g, dy] + ([dres] if has_res else [])
    return pl.pallas_call(
        body, name=name,
        out_shape=(jax.ShapeDtypeStruct((LP, D), out_dtype), jax.ShapeDtypeStruct((1, D), F32)),
        grid=(LP // tr,), in_specs=[row, vec, row] + ([row] if has_res else []), out_specs=(row, vec),
        compiler_params=_cp(("arbitrary",)))(*ins)


HALO = 8


def _conv3(ext, w_ref, b_ref):
    n = ext.shape[0]
    return (w_ref[0:1, :] * pltpu.roll(ext, 2, 0) + w_ref[1:2, :] * pltpu.roll(ext, 1, 0)
            + w_ref[2:3, :] * ext + b_ref[...])


def conv_act_fwd(cfg, a, cw, cb, name):
    LP, F, FB, tm = cfg.LP, cfg.F, cfg.FB, cfg.tr
    nb = tm // HALO

    def body(a_ref, prev_ref, w_ref, b_ref, act_ref):
        i = pl.program_id(1)
        prev = jnp.where(i > 0, prev_ref[...], 0.0)
        ext = jnp.concatenate([prev, a_ref[...]], axis=0)
        c = _conv3(ext, w_ref, b_ref)[HALO:, :]
        cg, cu = c[:, :FB], c[:, FB:]
        act_ref[...] = (cg * jax.nn.sigmoid(cg) * cu).astype(MXU)

    return pl.pallas_call(
        body, name=name, out_shape=jax.ShapeDtypeStruct((LP, F), MXU), grid=(F // FB, LP // tm),
        in_specs=[pl.BlockSpec((tm, 2 * FB), lambda j, i: (i, j)),
                  pl.BlockSpec((HALO, 2 * FB), lambda j, i: (jnp.maximum(i * nb - 1, 0), j)),
                  pl.BlockSpec((3, 2 * FB), lambda j, i: (0, j)),
                  pl.BlockSpec((1, 2 * FB), lambda j, i: (0, j))],
        out_specs=pl.BlockSpec((tm, FB), lambda j, i: (i, j)),
        compiler_params=_cp(("parallel", "parallel")))(a, a, cw, cb)


def conv_act_bwd(cfg, a, d_act, cw, cb, name):
    LP, F, FB, tm = cfg.LP, cfg.F, cfg.FB, cfg.tr
    nb, last = tm // HALO, LP // tm - 1
    nrow = LP // HALO

    def body(a_ref, aprev_ref, anext_ref, d_ref, dnext_ref, w_ref, b_ref, da_ref, dcv_ref):
        i = pl.program_id(1)
        prev = jnp.where(i > 0, aprev_ref[...], 0.0)
        ext = jnp.concatenate([prev, a_ref[...], anext_ref[...]], axis=0)
        c = _conv3(ext, w_ref, b_ref)
        cg, cu = c[:, :FB], c[:, FB:]
        dnext = jnp.where(i < last, dnext_ref[...], 0.0)
        dact = jnp.concatenate([jnp.zeros((HALO, FB), F32), d_ref[...], dnext], axis=0)
        sg = jax.nn.sigmoid(cg)
        silu = cg * sg
        d_cg = dact * cu * (sg * (1.0 + cg * (1.0 - sg)))
        d_cu = dact * silu
        dc = jnp.concatenate([d_cg, d_cu], axis=1)
        n = tm + 2 * HALO
        da = (w_ref[2:3, :] * dc + w_ref[1:2, :] * pltpu.roll(dc, n - 1, 0)
              + w_ref[0:1, :] * pltpu.roll(dc, n - 2, 0))
        da_ref[...] = da[HALO:HALO + tm, :].astype(MXU)
        dcb = dc[HALO:HALO + tm, :]

        @pl.when(i == 0)
        def _():
            dcv_ref[...] = jnp.zeros_like(dcv_ref)

        dcv_ref[0:1, :] += jnp.sum(dcb * pltpu.roll(ext, 2, 0)[HALO:HALO + tm, :], axis=0, keepdims=True)
        dcv_ref[1:2, :] += jnp.sum(dcb * pltpu.roll(ext, 1, 0)[HALO:HALO + tm, :], axis=0, keepdims=True)
        dcv_ref[2:3, :] += jnp.sum(dcb * ext[HALO:HALO + tm, :], axis=0, keepdims=True)
        dcv_ref[3:4, :] += jnp.sum(dcb, axis=0, keepdims=True)

    return pl.pallas_call(
        body, name=name,
        out_shape=(jax.ShapeDtypeStruct((LP, 2 * F), MXU), jax.ShapeDtypeStruct((8, 2 * F), F32)),
        grid=(F // FB, LP // tm),
        in_specs=[pl.BlockSpec((tm, 2 * FB), lambda j, i: (i, j)),
                  pl.BlockSpec((HALO, 2 * FB), lambda j, i: (jnp.maximum(i * nb - 1, 0), j)),
                  pl.BlockSpec((HALO, 2 * FB), lambda j, i: (jnp.minimum((i + 1) * nb, nrow - 1), j)),
                  pl.BlockSpec((tm, FB), lambda j, i: (i, j)),
                  pl.BlockSpec((HALO, FB), lambda j, i: (jnp.minimum((i + 1) * nb, nrow - 1), j)),
                  pl.BlockSpec((3, 2 * FB), lambda j, i: (0, j)),
                  pl.BlockSpec((1, 2 * FB), lambda j, i: (0, j))],
        out_specs=(pl.BlockSpec((tm, 2 * FB), lambda j, i: (i, j)),
                   pl.BlockSpec((8, 2 * FB), lambda j, i: (0, j))),
        compiler_params=_cp(("parallel", "arbitrary")))(a, a, a, d_act, d_act, cw, cb)


def fox_gate_fwd(cfg, f_logit, b_pad, name):
    LP, tb = cfg.LP, cfg.tq

    def body(f_ref, b_ref, c_ref, carry_ref):
        @pl.when(pl.program_id(0) == 0)
        def _():
            carry_ref[...] = jnp.zeros_like(carry_ref)

        xv = f_ref[...] + b_ref[...]
        lf = -_softplus(-xv)
        r = lax.broadcasted_iota(jnp.int32, (tb, tb), 0)
        s = lax.broadcasted_iota(jnp.int32, (tb, tb), 1)
        c = _split_dot_left((s <= r).astype(MXU), lf, 3) + carry_ref[0:1, :]
        c_ref[...] = c
        carry_ref[0:1, :] = c[tb - 1:tb, :]

    blk = pl.BlockSpec((tb, LANES), lambda i: (i, 0))
    return pl.pallas_call(
        body, name=name, out_shape=jax.ShapeDtypeStruct((LP, LANES), F32), grid=(LP // tb,),
        in_specs=[blk, pl.BlockSpec((1, LANES), lambda i: (0, 0))], out_specs=blk,
        scratch_shapes=[pltpu.VMEM((8, LANES), F32)], compiler_params=_cp(("arbitrary",)))(f_logit, b_pad)


def fox_gate_bwd(cfg, dc, f_logit, b_pad, name):
    LP, tb = cfg.LP, cfg.tq
    nblk = LP // tb

    def body(dc_ref, f_ref, b_ref, df_ref, db_ref, carry_ref):
        @pl.when(pl.program_id(0) == 0)
        def _():
            carry_ref[...] = jnp.zeros_like(carry_ref)
            db_ref[...] = jnp.zeros_like(db_ref)

        r = lax.broadcasted_iota(jnp.int32, (tb, tb), 0)
        s = lax.broadcasted_iota(jnp.int32, (tb, tb), 1)
        dlf = _split_dot_left((s >= r).astype(MXU), dc_ref[...], 3) + carry_ref[0:1, :]
        carry_ref[0:1, :] = dlf[0:1, :]
        df = dlf * jax.nn.sigmoid(-(f_ref[...] + b_ref[...]))
        df_ref[...] = df.astype(MXU)
        db_ref[...] += jnp.sum(df, axis=0, keepdims=True)

    blk = pl.BlockSpec((tb, LANES), lambda i: (nblk - 1 - i, 0))
    vec = pl.BlockSpec((1, LANES), lambda i: (0, 0))
    return pl.pallas_call(
        body, name=name,
        out_shape=(jax.ShapeDtypeStruct((LP, LANES), MXU), jax.ShapeDtypeStruct((1, LANES), F32)),
        grid=(nblk,), in_specs=[blk, blk, vec], out_specs=(blk, vec),
        scratch_shapes=[pltpu.VMEM((8, LANES), F32)], compiler_params=_cp(("arbitrary",)))(dc, f_logit, b_pad)


def _head_norm_fwd(o, g):
    return (o * lax.rsqrt(jnp.mean(o * o, axis=-1, keepdims=True) + EPS)) * g


def _head_norm_bwd(o, g, d_on):
    r = lax.rsqrt(jnp.mean(o * o, axis=-1, keepdims=True) + EPS)
    ohat = o * r
    gdy = d_on * g
    d_o = r * (gdy - ohat * jnp.mean(gdy * ohat, axis=-1, keepdims=True))
    return d_o, jnp.sum(d_on * ohat, axis=0, keepdims=True)


def _tile_pos(i, tq):
    rows = lax.broadcasted_iota(jnp.int32, (tq, HD), 0) + i * tq
    cols = lax.broadcasted_iota(jnp.int32, (tq, HD), 1)
    return rows, cols


def _tri(pred):
    a = lax.broadcasted_iota(jnp.int32, (HD, HD), 0)
    b = lax.broadcasted_iota(jnp.int32, (HD, HD), 1)
    return pred(a, b).astype(MXU)


def _attn_specs(cfg, group):
    base = 3 * cfg.NH * group
    return base, base + cfg.NH, base + 2 * cfg.NH


def sb_fwd(cfg, qkv, g3, name):
    LP, NH, tq = cfg.LP, cfg.NH, cfg.tq
    nq, R = LP // tq, tq // HD
    scale = HD ** -0.5
    cq, ck, cv = _attn_specs(cfg, 0)

    def body(q_ref, k_ref, v_ref, g_ref, opre_ref, on_ref, rtot_ref):
        i = pl.program_id(1)
        q = q_ref[...]
        rows, cols = _tile_pos(i, tq)
        m_after = _tri(lambda a, b: a > b)
        nkb = (i + 1) * R

        def step(it, carry):
            acc, rc = carry
            j = nkb - 1 - it
            off = pl.multiple_of(j * HD, HD)
            kb, vb = k_ref[pl.ds(off, HD), :], v_ref[pl.ds(off, HD), :]
            z = lax.dot_general(q, kb, NT, preferred_element_type=F32) * scale
            before = (cols + j * HD) < rows
            sp = _softplus(z)
            lk = jnp.where(before, -sp, 0.0)
            c = _split_dot(lk, m_after, 3) + rc
            a = jnp.where(before, jnp.exp((z - sp) + c), 0.0)
            acc = acc + jnp.dot(a.astype(MXU), vb, preferred_element_type=F32)
            return acc, rc + jnp.sum(lk, axis=1, keepdims=True)

        acc, rc = lax.fori_loop(0, nkb, step, (jnp.zeros((tq, HD), F32), jnp.zeros((tq, 1), F32)))
        opre_ref[...] = acc
        on_ref[...] = _head_norm_fwd(acc, g_ref[0]).astype(MXU)
        rtot_ref[0] = rc

    return pl.pallas_call(
        body, name=name,
        out_shape=(jax.ShapeDtypeStruct((LP, cfg.WG), F32), jax.ShapeDtypeStruct((LP, cfg.WG), MXU),
                   jax.ShapeDtypeStruct((NH, LP, 1), F32)),
        grid=(NH, nq),
        in_specs=[pl.BlockSpec((tq, HD), lambda h, i: (i, cq + h)),
                  pl.BlockSpec((LP, HD), lambda h, i: (0, ck + h)),
                  pl.BlockSpec((LP, HD), lambda h, i: (0, cv + h)),
                  pl.BlockSpec((1, 1, HD), lambda h, i: (h, 0, 0))],
        out_specs=(pl.BlockSpec((tq, HD), lambda h, i: (i, h)),
                   pl.BlockSpec((tq, HD), lambda h, i: (i, h)),
                   pl.BlockSpec((1, tq, 1), lambda h, i: (h, i, 0))),
        compiler_params=_cp(("parallel", "parallel")))(qkv, qkv, qkv, g3)


def sb_bwd(cfg, qkv, g3, o_pre, d_on, rtot, name):
    LP, NH, tq = cfg.LP, cfg.NH, cfg.tq
    nq, R = LP // tq, tq // HD
    scale = HD ** -0.5
    cq, ck, cv = _attn_specs(cfg, 0)

    def body(q_ref, k_ref, v_ref, g_ref, o_ref, don_ref, rtot_ref, dq_ref, dk_ref, dv_ref, dg_ref,
             dk_acc, dv_acc):
        i = pl.program_id(1)

        @pl.when(i == 0)
        def _():
            dk_acc[...] = jnp.zeros_like(dk_acc)
            dv_acc[...] = jnp.zeros_like(dv_acc)
            dg_ref[...] = jnp.zeros_like(dg_ref)

        q = q_ref[...]
        d_o, dg = _head_norm_bwd(o_ref[...], g_ref[0], don_ref[...])
        dg_ref[0] += dg
        do_b = d_o.astype(MXU)
        rt = rtot_ref[0]
        rows, cols = _tile_pos(i, tq)
        m_le = _tri(lambda a, b: a <= b)
        m_lt = _tri(lambda a, b: a < b)
        nkb = (i + 1) * R

        def step(j, carry):
            dq, lc, pc = carry
            off = pl.multiple_of(j * HD, HD)
            kb, vb = k_ref[pl.ds(off, HD), :], v_ref[pl.ds(off, HD), :]
            z = lax.dot_general(q, kb, NT, preferred_element_type=F32) * scale
            before = (cols + j * HD) < rows
            sp = _softplus(z)
            lk = jnp.where(before, -sp, 0.0)
            c = rt - lc - _split_dot(lk, m_le, 3)
            ls = z - sp
            a = jnp.where(before, jnp.exp(ls + c), 0.0)
            da = lax.dot_general(do_b, vb, NT, preferred_element_type=F32)
            gg = a * da
            p = pc + _split_dot(gg, m_lt, 2)
            sig = jnp.exp(ls)
            dz = jnp.where(before, gg * (1.0 - sig) - sig * p, 0.0) * scale
            dz_b, a_b = dz.astype(MXU), a.astype(MXU)
            dq = dq + jnp.dot(dz_b, kb, preferred_element_type=F32)
            dk_acc[pl.ds(off, HD), :] += lax.dot_general(dz_b, q, TN, preferred_element_type=F32)
            dv_acc[pl.ds(off, HD), :] += lax.dot_general(a_b, do_b, TN, preferred_element_type=F32)
            return dq, lc + jnp.sum(lk, axis=1, keepdims=True), pc + jnp.sum(gg, axis=1, keepdims=True)

        zc = jnp.zeros((tq, 1), F32)
        dq, _, _ = lax.fori_loop(0, nkb, step, (jnp.zeros((tq, HD), F32), zc, zc))
        dq_ref[...] = dq.astype(MXU)

        @pl.when(i == nq - 1)
        def _():
            dk_ref[...] = dk_acc[...].astype(MXU)
            dv_ref[...] = dv_acc[...].astype(MXU)

    blk = pl.BlockSpec((tq, HD), lambda h, i: (i, h))
    full = pl.BlockSpec((LP, HD), lambda h, i: (0, h))
    gspec = pl.BlockSpec((1, 1, HD), lambda h, i: (h, 0, 0))
    return pl.pallas_call(
        body, name=name,
        out_shape=(jax.ShapeDtypeStruct((LP, cfg.WG), MXU),) * 3 + (jax.ShapeDtypeStruct((NH, 1, HD), F32),),
        grid=(NH, nq),
        in_specs=[pl.BlockSpec((tq, HD), lambda h, i: (i, cq + h)),
                  pl.BlockSpec((LP, HD), lambda h, i: (0, ck + h)),
                  pl.BlockSpec((LP, HD), lambda h, i: (0, cv + h)),
                  gspec, blk, blk, pl.BlockSpec((1, tq, 1), lambda h, i: (h, i, 0))],
        out_specs=(blk, full, full, gspec),
        scratch_shapes=[pltpu.VMEM((LP, HD), F32), pltpu.VMEM((LP, HD), F32)],
        compiler_params=_cp(("parallel", "arbitrary")))(qkv, qkv, qkv, g3, o_pre, d_on, rtot)


def fox_fwd(cfg, qkv, g3, c_col, c_row, name):
    LP, NH, tq = cfg.LP, cfg.NH, cfg.tq
    nq, R = LP // tq, tq // HD
    scale = HD ** -0.5
    cq, ck, cv = _attn_specs(cfg, 1)

    def body(q_ref, k_ref, v_ref, g_ref, ccol_ref, crow_ref, opre_ref, on_ref, lse_ref):
        i = pl.program_id(1)
        q = q_ref[...]
        cqv = ccol_ref[0]
        rows, cols = _tile_pos(i, tq)
        nkb = (i + 1) * R

        def step(j, carry):
            acc, m, l = carry
            off = pl.multiple_of(j * HD, HD)
            kb, vb = k_ref[pl.ds(off, HD), :], v_ref[pl.ds(off, HD), :]
            s = lax.dot_general(q, kb, NT, preferred_element_type=F32) * scale + (cqv - crow_ref[0, pl.ds(j, 1), :])
            s = jnp.where((cols + j * HD) <= rows, s, NEG)
            m_new = jnp.maximum(m, jnp.max(s, axis=1, keepdims=True))
            alpha = jnp.exp(m - m_new)
            p = jnp.exp(s - m_new)
            l = alpha * l + jnp.sum(p, axis=1, keepdims=True)
            acc = alpha * acc + _split_dot(p, vb, 2)
            return acc, m_new, l

        acc, m, l = lax.fori_loop(0, nkb, step, (jnp.zeros((tq, HD), F32), jnp.full((tq, 1), NEG, F32),
                                                 jnp.zeros((tq, 1), F32)))
        o = acc / l
        opre_ref[...] = o
        on_ref[...] = _head_norm_fwd(o, g_ref[0]).astype(MXU)
        lse_ref[0] = m + jnp.log(l)

    return pl.pallas_call(
        body, name=name,
        out_shape=(jax.ShapeDtypeStruct((LP, cfg.WG), F32), jax.ShapeDtypeStruct((LP, cfg.WG), MXU),
                   jax.ShapeDtypeStruct((NH, LP, 1), F32)),
        grid=(NH, nq),
        in_specs=[pl.BlockSpec((tq, HD), lambda h, i: (i, cq + h)),
                  pl.BlockSpec((LP, HD), lambda h, i: (0, ck + h)),
                  pl.BlockSpec((LP, HD), lambda h, i: (0, cv + h)),
                  pl.BlockSpec((1, 1, HD), lambda h, i: (h, 0, 0)),
                  pl.BlockSpec((1, tq, 1), lambda h, i: (h, i, 0)),
                  pl.BlockSpec((1, LP // HD, HD), lambda h, i: (h, 0, 0))],
        out_specs=(pl.BlockSpec((tq, HD), lambda h, i: (i, h)),
                   pl.BlockSpec((tq, HD), lambda h, i: (i, h)),
                   pl.BlockSpec((1, tq, 1), lambda h, i: (h, i, 0))),
        compiler_params=_cp(("parallel", "parallel")))(qkv, qkv, qkv, g3, c_col, c_row)


def fox_bwd(cfg, qkv, g3, c_col, c_row, o_pre, d_on, lse, name):
    LP, NH, tq = cfg.LP, cfg.NH, cfg.tq
    nq, R = LP // tq, tq // HD
    scale = HD ** -0.5
    cq, ck, cv = _attn_specs(cfg, 1)

    def body(q_ref, k_ref, v_ref, g_ref, ccol_ref, crow_ref, o_ref, don_ref, lse_ref,
             dq_ref, dk_ref, dv_ref, dg_ref, dc_ref, dk_acc, dv_acc):
        i = pl.program_id(1)

        @pl.when(i == 0)
        def _():
            dk_acc[...] = jnp.zeros_like(dk_acc)
            dv_acc[...] = jnp.zeros_like(dv_acc)
            dg_ref[...] = jnp.zeros_like(dg_ref)
            dc_ref[...] = jnp.zeros_like(dc_ref)

        q = q_ref[...]
        ov = o_ref[...]
        d_o, dg = _head_norm_bwd(ov, g_ref[0], don_ref[...])
        dg_ref[0] += dg
        do_b = d_o.astype(MXU)
        delta = jnp.sum(do_b.astype(F32) * ov, axis=1, keepdims=True)
        cqv, lsev = ccol_ref[0], lse_ref[0]
        rows, cols = _tile_pos(i, tq)
        nkb = (i + 1) * R

        def step(j, dq):
            off = pl.multiple_of(j * HD, HD)
            kb, vb = k_ref[pl.ds(off, HD), :], v_ref[pl.ds(off, HD), :]
            s = lax.dot_general(q, kb, NT, preferred_element_type=F32) * scale + (cqv - crow_ref[0, pl.ds(j, 1), :])
            p = jnp.where((cols + j * HD) <= rows, jnp.exp(s - lsev), 0.0)
            dp = lax.dot_general(do_b, vb, NT, preferred_element_type=F32)
            ds = p * (dp - delta)
            dc_ref[0, pl.ds(j, 1), :] -= jnp.sum(ds, axis=0, keepdims=True)
            ds_b = (ds * scale).astype(MXU)
            dk_acc[pl.ds(off, HD), :] += lax.dot_general(ds_b, q, TN, preferred_element_type=F32)
            dv_acc[pl.ds(off, HD), :] += lax.dot_general(p.astype(MXU), do_b, TN, preferred_element_type=F32)
            return dq + jnp.dot(ds_b, kb, preferred_element_type=F32)

        dq = lax.fori_loop(0, nkb, step, jnp.zeros((tq, HD), F32))
        dq_ref[...] = dq.astype(MXU)

        @pl.when(i == nq - 1)
        def _():
            dk_ref[...] = dk_acc[...].astype(MXU)
            dv_ref[...] = dv_acc[...].astype(MXU)

    blk = pl.BlockSpec((tq, HD), lambda h, i: (i, h))
    full = pl.BlockSpec((LP, HD), lambda h, i: (0, h))
    gspec = pl.BlockSpec((1, 1, HD), lambda h, i: (h, 0, 0))
    col = pl.BlockSpec((1, tq, 1), lambda h, i: (h, i, 0))
    rowv = pl.BlockSpec((1, LP // HD, HD), lambda h, i: (h, 0, 0))
    return pl.pallas_call(
        body, name=name,
        out_shape=(jax.ShapeDtypeStruct((LP, cfg.WG), MXU),) * 3
        + (jax.ShapeDtypeStruct((NH, 1, HD), F32), jax.ShapeDtypeStruct((NH, LP // HD, HD), F32)),
        grid=(NH, nq),
        in_specs=[pl.BlockSpec((tq, HD), lambda h, i: (i, cq + h)),
                  pl.BlockSpec((LP, HD), lambda h, i: (0, ck + h)),
                  pl.BlockSpec((LP, HD), lambda h, i: (0, cv + h)),
                  gspec, col, rowv, blk, blk, col],
        out_specs=(blk, full, full, gspec, rowv),
        scratch_shapes=[pltpu.VMEM((LP, HD), F32), pltpu.VMEM((LP, HD), F32)],
        compiler_params=_cp(("parallel", "arbitrary")))(qkv, qkv, qkv, g3, c_col, c_row, o_pre, d_on, lse)


def _row_tile(rows, cols, n_arrays):
    budget = 24 * 1024 * 1024 // (2 * n_arrays * cols * 4)
    cap = max(8, min(rows, budget // 8 * 8))
    div = _tile(rows, cap, 8)
    return div if div <= cap and div * 4 >= cap else cap


def sum_slots(recv, name):
    S, rows, cols = recv.shape
    tr = _row_tile(rows, cols, S + 1)

    def body(r_ref, o_ref):
        acc = r_ref[0].astype(F32)
        for s in range(1, S):
            acc = acc + r_ref[s].astype(F32)
        o_ref[...] = acc

    return pl.pallas_call(
        body, name=name, out_shape=jax.ShapeDtypeStruct((rows, cols), F32), grid=(pl.cdiv(rows, tr),),
        in_specs=[pl.BlockSpec((S, tr, cols), lambda i: (0, i, 0))],
        out_specs=pl.BlockSpec((tr, cols), lambda i: (i, 0)), compiler_params=_cp(("parallel",)))(recv)


def adamw(w, m, v, g_parts, name):
    rows, cols = w.shape
    npart = len(g_parts)
    tr = _row_tile(rows, cols, 7 + npart)
    c1 = 1.0 - ADAM_B1 ** ADAM_STEP
    c2 = 1.0 - ADAM_B2 ** ADAM_STEP

    def body(*refs):
        w_ref, m_ref, v_ref = refs[:3]
        g_refs = refs[3:3 + npart]
        g_out, d_out, m_out, v_out = refs[3 + npart:]
        g = g_refs[0][...]
        for r in g_refs[1:]:
            g = g + r[...]
        g_out[...] = g
        mn = ADAM_B1 * m_ref[...] + (1.0 - ADAM_B1) * g
        vn = ADAM_B2 * v_ref[...] + (1.0 - ADAM_B2) * (g * g)
        m_out[...] = mn
        v_out[...] = vn
        d_out[...] = -ADAM_LR * ((mn / c1) / (jnp.sqrt(vn / c2) + ADAM_EPS) + ADAM_WD * w_ref[...])

    blk = pl.BlockSpec((tr, cols), lambda i: (i, 0))
    return pl.pallas_call(
        body, name=name, out_shape=(jax.ShapeDtypeStruct((rows, cols), F32),) * 4,
        grid=(pl.cdiv(rows, tr),), in_specs=[blk] * (3 + npart), out_specs=(blk,) * 4,
        compiler_params=_cp(("parallel",)))(w, m, v, *g_parts)


def _coords():
    return lax.axis_index("x"), lax.axis_index("y"), lax.axis_index("c")


def chip_exchange(arrs, scatter, name):
    n = len(arrs)

    def body(*refs):
        ins, outs = refs[:n], refs[n:2 * n]
        send_sems, recv_sems, local_sems = refs[2 * n:]
        x, y, c = _coords()
        me = 2 * x + y
        peers = [(1 - x, y), (x, 1 - y), (1 - x, 1 - y)]

        def src(a, to):
            return ins[a].at[to] if scatter else ins[a]

        local = [pltpu.make_async_copy(src(a, me), outs[a].at[me], local_sems.at[a]) for a in range(n)]
        for cp in local:
            cp.start()
        sends = []
        for a in range(n):
            for k, (px, py) in enumerate(peers):
                cp = pltpu.make_async_remote_copy(
                    src_ref=src(a, 2 * px + py), dst_ref=outs[a].at[me],
                    send_sem=send_sems.at[a, k], recv_sem=recv_sems.at[a, k],
                    device_id=(px, py, c), device_id_type=MESH)
                cp.start()
                sends.append(cp)
        for a in range(n):
            for k, (px, py) in enumerate(peers):
                frm = 2 * px + py
                pltpu.make_async_remote_copy(
                    src_ref=src(a, me), dst_ref=outs[a].at[frm],
                    send_sem=send_sems.at[a, k], recv_sem=recv_sems.at[a, k],
                    device_id=(px, py, c), device_id_type=MESH).wait_recv()
        for cp in sends:
            cp.wait_send()
        for cp in local:
            cp.wait()

    shapes = [a.shape[1:] if scatter else a.shape for a in arrs]
    any_spec = pl.BlockSpec(memory_space=pl.ANY)
    return pl.pallas_call(
        body, name=name,
        out_shape=tuple(jax.ShapeDtypeStruct((NCHIP,) + tuple(s), a.dtype) for s, a in zip(shapes, arrs)),
        in_specs=[any_spec] * n, out_specs=(any_spec,) * n,
        scratch_shapes=[pltpu.SemaphoreType.DMA((n, 3)), pltpu.SemaphoreType.DMA((n, 3)),
                        pltpu.SemaphoreType.DMA((n,))],
    )(*arrs)


def sibling_exchange(arrs, name):
    n = len(arrs)

    def body(*refs):
        ins, outs = refs[:n], refs[n:2 * n]
        send_sems, recv_sems = refs[2 * n:]
        x, y, c = _coords()
        cps = [pltpu.make_async_remote_copy(src_ref=ins[a], dst_ref=outs[a], send_sem=send_sems.at[a],
                                            recv_sem=recv_sems.at[a], device_id=(x, y, 1 - c),
                                            device_id_type=MESH) for a in range(n)]
        for cp in cps:
            cp.start()
        for cp in cps:
            cp.wait()

    any_spec = pl.BlockSpec(memory_space=pl.ANY)
    return pl.pallas_call(
        body, name=name, out_shape=tuple(jax.ShapeDtypeStruct(a.shape, a.dtype) for a in arrs),
        in_specs=[any_spec] * n, out_specs=(any_spec,) * n,
        scratch_shapes=[pltpu.SemaphoreType.DMA((n,)), pltpu.SemaphoreType.DMA((n,))],
    )(*arrs)


def all_reduce_small(pack, name):
    R = pack.shape[0]

    def body(p_ref, o_ref, slots, send_sems, recv_sems):
        x, y, c = _coords()
        me = 4 * x + 2 * y + c
        slots[me] = p_ref[...]
        cps = []
        for k in range(1, 8):
            kx, ky, kc = (k >> 2) & 1, (k >> 1) & 1, k & 1
            peer = ((1 - x) if kx else x, (1 - y) if ky else y, (1 - c) if kc else c)
            cp = pltpu.make_async_remote_copy(src_ref=p_ref, dst_ref=slots.at[me], send_sem=send_sems.at[k],
                                              recv_sem=recv_sems.at[k], device_id=peer, device_id_type=MESH)
            cp.start()
            cps.append((cp, peer))
        for k, (cp, peer) in enumerate(cps, start=1):
            frm = 4 * peer[0] + 2 * peer[1] + peer[2]
            pltpu.make_async_remote_copy(src_ref=p_ref, dst_ref=slots.at[frm], send_sem=send_sems.at[k],
                                         recv_sem=recv_sems.at[k], device_id=peer, device_id_type=MESH).wait_recv()
        for cp, _ in cps:
            cp.wait_send()
        acc = slots[0]
        for s in range(1, 8):
            acc = acc + slots[s]
        o_ref[...] = acc

    vm = pl.BlockSpec(memory_space=pltpu.VMEM)
    return pl.pallas_call(
        body, name=name, out_shape=jax.ShapeDtypeStruct((R, LANES), F32), in_specs=[vm], out_specs=vm,
        scratch_shapes=[pltpu.VMEM((8, R, LANES), F32), pltpu.SemaphoreType.DMA((8,)),
                        pltpu.SemaphoreType.DMA((8,))],
        compiler_params=pltpu.CompilerParams(vmem_limit_bytes=VMEM_LIMIT),
    )(pack)


def _ffperm(w, cfg):
    lead, nb = w.shape[:-1], cfg.F // cfg.FB
    return w.reshape(*lead, 2, nb, cfg.FB).swapaxes(-3, -2).reshape(*lead, cfg.F2)


def _ffunperm(w, cfg):
    lead, nb = w.shape[:-1], cfg.F // cfg.FB
    return w.reshape(*lead, nb, 2, cfg.FB).swapaxes(-3, -2).reshape(*lead, cfg.F2)


def _pack(arrs):
    parts = []
    for a in arrs:
        f = a.reshape(-1).astype(F32)
        parts.append(jnp.pad(f, (0, -f.shape[0] % 1024)).reshape(-1, LANES))
    return jnp.concatenate(parts, axis=0)


def _unpack(p, shapes):
    out, r = [], 0
    for s in shapes:
        n = math.prod(s)
        nr = (n + 1023) // 1024 * 8
        out.append(p[r:r + nr].reshape(-1)[:n].reshape(s))
        r += nr
    return out


def _vec(g):
    return g.reshape(1, -1)


def _layer_fwd(cfg, l, h, u, wl, last):
    tag = f"l{l}"
    qkv = matmul(u, wl['w_qkv'], 'nn', MXU, f"{tag}_qkv", tn_cap=1024)
    f_logit = matmul(u, wl['w_f'], 'nn', F32, f"{tag}_fproj")
    cpre = fox_gate_fwd(cfg, f_logit, wl['b_pad'], f"{tag}_gate_fwd")
    c_heads = cpre[:, :cfg.NH].T
    c_col = c_heads[:, :, None]
    c_row = c_heads.reshape(cfg.NH, cfg.LP // HD, HD)
    o_sb, on_sb, rtot = sb_fwd(cfg, qkv, wl['g_sb'], f"{tag}_sb_fwd")
    o_fx, on_fx, lse = fox_fwd(cfg, qkv, wl['g_fox'], c_col, c_row, f"{tag}_fox_fwd")
    mixin = jnp.concatenate([on_sb, on_fx], axis=1)
    mix = matmul(mixin, wl['w_out'], 'nn', F32, f"{tag}_out")
    h1, u2 = resid_norm(cfg, mix, h, wl['g_mix_post'], wl['g_ffn_pre'], f"{tag}_mixres")
    a = matmul(u2, wl['w_up'], 'nn', F32, f"{tag}_up")
    act = conv_act_fwd(cfg, a, wl['conv_w'], wl['conv_b'], f"{tag}_conv_fwd")
    ff = matmul(act, wl['w_down'], 'nn', F32, f"{tag}_down", tm_cap=704, tk_cap=2816)
    h2, u_next = resid_norm(cfg, ff, h1, wl['g_ffn_post'], wl['g_next'], f"{tag}_ffnres")
    saved = dict(h=h, u=u, qkv=qkv, f_logit=f_logit, c_col=c_col, c_row=c_row, o_sb=o_sb, o_fx=o_fx, rtot=rtot,
                 lse=lse, mixin=mixin, mix=mix, h1=h1, u2=u2, a=a, act=act, ff=ff)
    return h2, u_next, saved


def _layer_bwd(cfg, l, dh2, wl, sv):
    tag = f"l{l}"
    g = {}
    d_ff, g['g_ffn_post'] = norm_bwd(cfg, sv['ff'], wl['g_ffn_post'], dh2, None, MXU, f"{tag}_ffnpost_bwd")
    d_act = matmul(d_ff, wl['w_down'], 'nt', F32, f"{tag}_down_dx")
    g['w_down'] = matmul(sv['act'], d_ff, 'tn', F32, f"{tag}_down_dw", tm_cap=1408, tk_cap=1408)
    d_a, d_conv = conv_act_bwd(cfg, sv['a'], d_act, wl['conv_w'], wl['conv_b'], f"{tag}_conv_bwd")
    g['conv'] = d_conv
    g['w_up'] = matmul(sv['u2'], d_a, 'tn', F32, f"{tag}_up_dw", tm_cap=1024, tk_cap=1408)
    du2 = matmul(d_a, wl['w_up'], 'nt', F32, f"{tag}_up_dx", tm_cap=704, tk_cap=2816)
    dh1, g['g_ffn_pre'] = norm_bwd(cfg, sv['h1'], wl['g_ffn_pre'], du2, dh2, F32, f"{tag}_ffnpre_bwd")
    d_mix, g['g_mix_post'] = norm_bwd(cfg, sv['mix'], wl['g_mix_post'], dh1, None, MXU, f"{tag}_mixpost_bwd")
    d_mixin = matmul(d_mix, wl['w_out'], 'nt', F32, f"{tag}_out_dx")
    g['w_out'] = matmul(sv['mixin'], d_mix, 'tn', F32, f"{tag}_out_dw", tm_cap=1024, tk_cap=1408)
    WG = cfg.WG
    dq_s, dk_s, dv_s, g['g_sb'] = sb_bwd(cfg, sv['qkv'], wl['g_sb'], sv['o_sb'], d_mixin[:, :WG], sv['rtot'],
                                         f"{tag}_sb_bwd")
    dq_f, dk_f, dv_f, g['g_fox'], dc_row = fox_bwd(cfg, sv['qkv'], wl['g_fox'], sv['c_col'], sv['c_row'], sv['o_fx'],
                                                   d_mixin[:, WG:], sv['lse'], f"{tag}_fox_bwd")
    dc = jnp.pad(dc_row.reshape(cfg.NH, cfg.LP).T, ((0, 0), (0, LANES - cfg.NH)))
    d_f, g['b_f'] = fox_gate_bwd(cfg, dc, sv['f_logit'], wl['b_pad'], f"{tag}_gate_bwd")
    d_proj = jnp.concatenate([dq_s, dk_s, dv_s, dq_f, dk_f, dv_f, d_f], axis=1)
    g['w_in_ext'] = matmul(sv['u'], d_proj, 'tn', F32, f"{tag}_in_dw", tm_cap=1024, tn_cap=896, tk_cap=1408)
    du = matmul(d_proj, wl['w_in_ext'], 'nt', F32, f"{tag}_in_dx", tk_cap=3200)
    dh, g['g_mix_pre'] = norm_bwd(cfg, sv['h'], wl['g_mix_pre'], du, dh1, F32, f"{tag}_mixpre_bwd")
    return dh, g


def _local_step(cfg, h0, target_p, wf):
    layers = []
    for l in range(cfg.DEPTH):
        w_in = wf['w_in'][l]
        w_f = jnp.pad(w_in[:, cfg.NQKV:], ((0, 0), (0, LANES - cfg.NH)))
        layers.append(dict(
            w_qkv=w_in[:, :cfg.NQKV], w_f=w_f, w_in_ext=jnp.concatenate([w_in[:, :cfg.NQKV], w_f], axis=1),
            b_pad=jnp.pad(wf['b_f'][l], (0, LANES - cfg.NH)).reshape(1, LANES),
            g_sb=wf['g_sb'][l][:, None, :], g_fox=wf['g_fox'][l][:, None, :],
            w_out=wf['w_out'][l], w_up=_ffperm(wf['w_up'][l], cfg), w_down=wf['w_down'][l],
            conv_w=_ffperm(wf['conv_w'][l], cfg), conv_b=_ffperm(wf['conv_b'][l], cfg).reshape(1, -1),
            g_mix_pre=_vec(wf['g_mix_pre'][l]), g_mix_post=_vec(wf['g_mix_post'][l]),
            g_ffn_pre=_vec(wf['g_ffn_pre'][l]), g_ffn_post=_vec(wf['g_ffn_post'][l]),
            g_next=_vec(wf['g_mix_pre'][(l + 1) % cfg.DEPTH])))
    h = h0
    u = pre_norm(cfg, h0, layers[0]['g_mix_pre'], "l0_prenorm")
    saved = []
    for l in range(cfg.DEPTH):
        h, u, sv = _layer_fwd(cfg, l, h, u, layers[l], l == cfg.DEPTH - 1)
        saved.append(sv)
    dh, loss_blk = loss_head(cfg, h, target_p, "loss_head")
    grads = [None] * cfg.DEPTH
    for l in reversed(range(cfg.DEPTH)):
        dh, grads[l] = _layer_bwd(cfg, l, dh, layers[l], saved[l])
    return loss_blk, dh, grads


def _step(cfg, x, w, target, m, v):
    xi, yi, ci = _coords()
    chip = 2 * xi + yi
    D, LP, L, NM = cfg.D, cfg.LP, cfg.L, cfg.NMETA
    DEP = cfg.DEPTH

    shards = [w[n].astype(MXU).reshape(-1, w[n].shape[-1]) for n in BIG]
    gath = chip_exchange(shards, False, "gather_weights")
    gw = dict(zip(BIG, gath))
    n4 = cfg.N_IN // NCHIP
    wf = dict(
        w_in=gw['w_in'].reshape(NCHIP, DEP, D, n4).transpose(1, 2, 0, 3).reshape(DEP, D, cfg.N_IN),
        w_up=gw['w_up'].reshape(NCHIP, DEP, D, cfg.F2 // NCHIP).transpose(1, 2, 0, 3).reshape(DEP, D, cfg.F2),
        w_out=gw['w_out'].reshape(NCHIP, DEP, cfg.WMIX // NCHIP, D).transpose(1, 0, 2, 3).reshape(DEP, cfg.WMIX, D),
        w_down=gw['w_down'].reshape(NCHIP, DEP, cfg.F // NCHIP, D).transpose(1, 0, 2, 3).reshape(DEP, cfg.F, D))
    small_in = _pack([lax.dynamic_update_slice(jnp.zeros((NM, D), F32), w['meta'], (0, chip * (D // NCHIP))),
                      lax.dynamic_update_slice(jnp.zeros((DEP, 3, cfg.F2), F32), w['conv_w'],
                                               (0, 0, chip * (cfg.F2 // NCHIP)))])
    small_in = jnp.where(ci == 0, small_in, 0.0)
    meta_full, conv_w_full = _unpack(all_reduce_small(small_in, "gather_small"), [(NM, D), (DEP, 3, cfg.F2)])
    for n in SMALL:
        wf[n] = w[n]
    wf['conv_w'] = conv_w_full

    zpad = jnp.zeros((LP - L, D), F32)
    h0 = jnp.concatenate([meta_full, x[0], zpad], axis=0)
    target_p = jnp.concatenate([jnp.zeros((NM, D), F32), target[0], zpad], axis=0)
    loss_blk, dh0, grads = _local_step(cfg, h0, target_p, wf)

    def stack(key, shape):
        return jnp.stack([grads[l][key].reshape(shape) for l in range(DEP)])

    conv_g = jnp.stack([_ffunperm(grads[l]['conv'], cfg) for l in range(DEP)])
    small_g = dict(
        loss=loss_blk[0:1, 0:1], meta=dh0[:NM], g_mix_pre=stack('g_mix_pre', (D,)),
        b_f=stack('b_f', (LANES,))[:, :cfg.NH], g_sb=stack('g_sb', (cfg.NH, HD)), g_fox=stack('g_fox', (cfg.NH, HD)),
        g_mix_post=stack('g_mix_post', (D,)), g_ffn_pre=stack('g_ffn_pre', (D,)),
        conv_w=conv_g[:, 0:3], conv_b=conv_g[:, 3], g_ffn_post=stack('g_ffn_post', (D,)))
    keys = list(small_g)
    red = dict(zip(keys, _unpack(all_reduce_small(_pack([small_g[k] for k in keys]), "reduce_small"),
                                 [small_g[k].shape for k in keys])))
    loss = red['loss'].reshape(())
    red['meta'] = lax.dynamic_slice(red['meta'], (0, chip * (D // NCHIP)), (NM, D // NCHIP))
    red['conv_w'] = lax.dynamic_slice(red['conv_w'], (0, 0, chip * (cfg.F2 // NCHIP)), (DEP, 3, cfg.F2 // NCHIP))

    gin = jnp.stack([grads[l]['w_in_ext'][:, :cfg.N_IN] for l in range(DEP)])
    gup = jnp.stack([_ffunperm(grads[l]['w_up'], cfg) for l in range(DEP)])
    gout = jnp.stack([grads[l]['w_out'] for l in range(DEP)])
    gdown = jnp.stack([grads[l]['w_down'] for l in range(DEP)])
    send = [
        gin.reshape(DEP, D, NCHIP, n4).transpose(2, 0, 1, 3),
        gout.reshape(DEP, NCHIP, cfg.WMIX // NCHIP, D).transpose(1, 0, 2, 3),
        gup.reshape(DEP, D, NCHIP, cfg.F2 // NCHIP).transpose(2, 0, 1, 3),
        gdown.reshape(DEP, NCHIP, cfg.F // NCHIP, D).transpose(1, 0, 2, 3)]
    send = [s.astype(MXU).reshape(NCHIP, -1, s.shape[-1]) for s in send]
    recv = chip_exchange(send, True, "scatter_grads")
    part = [sum_slots(r, f"sum_{n}") for r, n in zip(recv, BIG)]
    other = sibling_exchange(part, "sibling_grads")

    outs = {}
    for n, p, q in zip(BIG, part, other):
        shp = w[n].shape
        s2 = (shp[0] * shp[1], shp[2])
        res = adamw(w[n].reshape(s2), m[n].reshape(s2), v[n].reshape(s2), [p, q], f"adamw_{n}")
        outs[n] = [r.reshape(shp) for r in res]

    shapes = [w[n].shape for n in SMALL]
    res = adamw(_pack([w[n] for n in SMALL]), _pack([m[n] for n in SMALL]), _pack([v[n] for n in SMALL]),
                [_pack([red[n] for n in SMALL])], "adamw_small")
    res = [_unpack(r, shapes) for r in res]
    for i, n in enumerate(SMALL):
        outs[n] = [res[k][i] for k in range(4)]

    grad_x = dh0[NM:L][None]
    return (loss, grad_x, *[outs[n][0] for n in WEIGHTS], *[outs[n][1] for n in WEIGHTS],
            *[outs[n][2] for n in WEIGHTS], *[outs[n][3] for n in WEIGHTS])


def kernel(x, meta, g_mix_pre, w_in, b_f, g_sb, g_fox, w_out, g_mix_post, g_ffn_pre, w_up, conv_w, conv_b, w_down, g_ffn_post, loss_target, m_meta, m_g_mix_pre, m_w_in, m_b_f, m_g_sb, m_g_fox, m_w_out, m_g_mix_post, m_g_ffn_pre, m_w_up, m_conv_w, m_conv_b, m_w_down, m_g_ffn_post, v_meta, v_g_mix_pre, v_w_in, v_b_f, v_g_sb, v_g_fox, v_w_out, v_g_mix_post, v_g_ffn_pre, v_w_up, v_conv_w, v_conv_b, v_w_down, v_g_ffn_post):
    w = dict(zip(WEIGHTS, (meta, g_mix_pre, w_in, b_f, g_sb, g_fox, w_out, g_mix_post, g_ffn_pre, w_up, conv_w,
                           conv_b, w_down, g_ffn_post)))
    m = dict(zip(WEIGHTS, (m_meta, m_g_mix_pre, m_w_in, m_b_f, m_g_sb, m_g_fox, m_w_out, m_g_mix_post, m_g_ffn_pre,
                           m_w_up, m_conv_w, m_conv_b, m_w_down, m_g_ffn_post)))
    v = dict(zip(WEIGHTS, (v_meta, v_g_mix_pre, v_w_in, v_b_f, v_g_sb, v_g_fox, v_w_out, v_g_mix_post, v_g_ffn_pre,
                           v_w_up, v_conv_w, v_conv_b, v_w_down, v_g_ffn_post)))
    return _step(PROD, x, w, loss_target, m, v)
```

```python
import functools
import math
from typing import NamedTuple

import jax
import jax.numpy as jnp
from jax import lax
from jax.experimental import pallas as pl
from jax.experimental.pallas import tpu as pltpu

F32 = jnp.float32
MXU = jnp.bfloat16
HD = 128
LANES = 128
EPS = 1e-6
NEG = -1e30
LK_PIECES = 2
ADAM_LR, ADAM_B1, ADAM_B2, ADAM_EPS, ADAM_WD, ADAM_STEP = 0.001, 0.9, 0.999, 1e-08, 0.01, 10
VMEM_LIMIT = 56 * 1024 * 1024
MESH = pl.DeviceIdType.MESH
NCHIP = 4

WEIGHTS = ['meta', 'g_mix_pre', 'w_in', 'b_f', 'g_sb', 'g_fox', 'w_out', 'g_mix_post', 'g_ffn_pre',
           'w_up', 'conv_w', 'conv_b', 'w_down', 'g_ffn_post']
BIG = ['w_in', 'w_out', 'w_up', 'w_down']
SMALL = [n for n in WEIGHTS if n not in BIG]

NT = (((1,), (1,)), ((), ()))
TN = (((0,), (0,)), ((), ()))
NN = (((1,), (0,)), ((), ()))


class Cfg(NamedTuple):
    D: int
    SEQ: int
    NMETA: int
    NH: int
    F: int
    LP: int
    tq: int
    tr: int
    FB: int
    DEPTH: int = 2

    @property
    def L(self): return self.SEQ + self.NMETA
    @property
    def WG(self): return self.NH * HD
    @property
    def WMIX(self): return 2 * self.WG
    @property
    def NQKV(self): return 6 * self.WG
    @property
    def N_IN(self): return self.NQKV + self.NH
    @property
    def NEXT(self): return self.NQKV + LANES
    @property
    def F2(self): return 2 * self.F


PROD = Cfg(D=2048, SEQ=4096, NMETA=16, NH=8, F=5632, LP=4224, tq=384, tr=192, FB=512)


def _tile(n, cap, mult=LANES):
    best = None
    for t in range(mult, min(n, cap) + 1, mult):
        if n % t == 0:
            best = t
    return best if best is not None else n


def _cp(sem):
    return pltpu.CompilerParams(dimension_semantics=sem, vmem_limit_bytes=VMEM_LIMIT)


def _split_dot(x, tri, pieces):
    acc, r = None, x
    for p in range(pieces):
        xp = r.astype(MXU)
        d = jnp.dot(xp, tri, preferred_element_type=F32)
        acc = d if acc is None else acc + d
        if p + 1 < pieces:
            r = r - xp.astype(F32)
    return acc


def _split_dot_left(tri, x, pieces):
    acc, r = None, x
    for p in range(pieces):
        xp = r.astype(MXU)
        d = jnp.dot(tri, xp, preferred_element_type=F32)
        acc = d if acc is None else acc + d
        if p + 1 < pieces:
            r = r - xp.astype(F32)
    return acc


def _softplus(z):
    return jnp.maximum(z, 0.0) + jnp.log(1.0 + jnp.exp(-jnp.abs(z)))


def matmul(a, b, mode, out_dtype, name, tm_cap=1408, tn_cap=1024, tk_cap=8192):
    if mode == 'tn':
        K, M = a.shape
    else:
        M, K = a.shape
    N = b.shape[0] if mode == 'nt' else b.shape[1]
    tm, tn, tk = _tile(M, tm_cap, 8), _tile(N, tn_cap), _tile(K, tk_cap)
    nk = K // tk
    dn = {'nn': NN, 'nt': NT, 'tn': TN}[mode]

    def body(a_ref, b_ref, o_ref, *scratch):
        d = lax.dot_general(a_ref[...], b_ref[...], dn, preferred_element_type=F32)
        if nk == 1:
            o_ref[...] = d.astype(out_dtype)
        else:
            acc_ref, = scratch
            k = pl.program_id(2)

            @pl.when(k == 0)
            def _():
                acc_ref[...] = d

            @pl.when(k > 0)
            def _():
                acc_ref[...] += d

            @pl.when(k == nk - 1)
            def _():
                o_ref[...] = acc_ref[...].astype(out_dtype)

    a_spec = (pl.BlockSpec((tk, tm), lambda i, j, k: (k, i)) if mode == 'tn'
              else pl.BlockSpec((tm, tk), lambda i, j, k: (i, k)))
    b_spec = (pl.BlockSpec((tn, tk), lambda i, j, k: (j, k)) if mode == 'nt'
              else pl.BlockSpec((tk, tn), lambda i, j, k: (k, j)))
    return pl.pallas_call(
        body, name=name, out_shape=jax.ShapeDtypeStruct((M, N), out_dtype),
        grid=(M // tm, N // tn, nk), in_specs=[a_spec, b_spec],
        out_specs=pl.BlockSpec((tm, tn), lambda i, j, k: (i, j)),
        scratch_shapes=[] if nk == 1 else [pltpu.VMEM((tm, tn), F32)],
        compiler_params=_cp(("parallel", "parallel", "arbitrary")),
    )(a, b)


def _rstd(x):
    return lax.rsqrt(jnp.mean(x * x, axis=-1, keepdims=True) + EPS)


def pre_norm(cfg, h, g, name):
    LP, D, tr = cfg.LP, cfg.D, cfg.tr

    def body(h_ref, g_ref, u_ref):
        x = h_ref[...]
        u_ref[...] = ((x * _rstd(x)) * g_ref[...]).astype(MXU)

    row = pl.BlockSpec((tr, D), lambda i: (i, 0))
    vec = pl.BlockSpec((1, D), lambda i: (0, 0))
    return pl.pallas_call(body, name=name, out_shape=jax.ShapeDtypeStruct((LP, D), MXU), grid=(LP // tr,),
                          in_specs=[row, vec], out_specs=row, compiler_params=_cp(("parallel",)))(h, g)


def resid_norm(cfg, y, h, g_post, g_next, name):
    LP, D, tr = cfg.LP, cfg.D, cfg.tr

    def body(y_ref, h_ref, gp_ref, gn_ref, hn_ref, u_ref):
        yv = y_ref[...]
        hn = h_ref[...] + (yv * _rstd(yv)) * gp_ref[...]
        hn_ref[...] = hn
        u_ref[...] = ((hn * _rstd(hn)) * gn_ref[...]).astype(MXU)

    row = pl.BlockSpec((tr, D), lambda i: (i, 0))
    vec = pl.BlockSpec((1, D), lambda i: (0, 0))
    return pl.pallas_call(
        body, name=name,
        out_shape=(jax.ShapeDtypeStruct((LP, D), F32), jax.ShapeDtypeStruct((LP, D), MXU)),
        grid=(LP // tr,), in_specs=[row, row, vec, vec], out_specs=(row, row),
        compiler_params=_cp(("parallel",)))(y, h, g_post, g_next)


def loss_head(cfg, h, target, name):
    LP, D, tr = cfg.LP, cfg.D, cfg.tr
    lo, hi = cfg.NMETA, cfg.L

    def body(h_ref, t_ref, dh_ref, loss_ref):
        i = pl.program_id(0)
        rows = lax.broadcasted_iota(jnp.int32, (tr, D), 0) + i * tr
        diff = jnp.where((rows >= lo) & (rows < hi), h_ref[...] - t_ref[...], 0.0)
        dh_ref[...] = diff * (1.0 / D)
        part = 0.5 * jnp.sum(jnp.sum(diff * diff, axis=1, keepdims=True), axis=0, keepdims=True) * (1.0 / D)

        @pl.when(i == 0)
        def _():
            loss_ref[...] = jnp.zeros_like(loss_ref)

        loss_ref[...] += jnp.broadcast_to(part, loss_ref.shape)

    row = pl.BlockSpec((tr, D), lambda i: (i, 0))
    return pl.pallas_call(
        body, name=name,
        out_shape=(jax.ShapeDtypeStruct((LP, D), F32), jax.ShapeDtypeStruct((8, LANES), F32)),
        grid=(LP // tr,), in_specs=[row, row],
        out_specs=(row, pl.BlockSpec((8, LANES), lambda i: (0, 0))),
        compiler_params=_cp(("arbitrary",)))(h, target)


def norm_bwd(cfg, x, g, dy, dres, out_dtype, name):
    LP, D, tr = cfg.LP, cfg.D, cfg.tr
    has_res = dres is not None

    def body(*refs):
        if has_res:
            x_ref, g_ref, dy_ref, dres_ref, dx_ref, dg_ref = refs
        else:
            x_ref, g_ref, dy_ref, dx_ref, dg_ref = refs
        xv, dyv = x_ref[...], dy_ref[...]
        r = _rstd(xv)
        xhat = xv * r
        gdy = dyv * g_ref[...]
        dx = r * (gdy - xhat * jnp.mean(gdy * xhat, axis=-1, keepdims=True))
        if has_res:
            dx = dx + dres_ref[...]
        dx_ref[...] = dx.astype(out_dtype)

        @pl.when(pl.program_id(0) == 0)
        def _():
            dg_ref[...] = jnp.zeros_like(dg_ref)

        dg_ref[...] += jnp.sum(dyv * xhat, axis=0, keepdims=True)

    row = pl.BlockSpec((tr, D), lambda i: (i, 0))
    vec = pl.BlockSpec((1, D), lambda i: (0, 0))
    ins = [x, g, dy] + ([dres] if has_res else [])
    return pl.pallas_call(
        body, name=name,
        out_shape=(jax.ShapeDtypeStruct((LP, D), out_dtype), jax.ShapeDtypeStruct((1, D), F32)),
        grid=(LP // tr,), in_specs=[row, vec, row] + ([row] if has_res else []), out_specs=(row, vec),
        compiler_params=_cp(("arbitrary",)))(*ins)


HALO = 8


def _conv3(ext, w_ref, b_ref):
    n = ext.shape[0]
    return (w_ref[0:1, :] * pltpu.roll(ext, 2, 0) + w_ref[1:2, :] * pltpu.roll(ext, 1, 0)
            + w_ref[2:3, :] * ext + b_ref[...])


def conv_act_fwd(cfg, a, cw, cb, name):
    LP, F, FB, tm = cfg.LP, cfg.F, cfg.FB, cfg.tr
    nb = tm // HALO

    def body(a_ref, prev_ref, w_ref, b_ref, act_ref):
        i = pl.program_id(1)
        prev = jnp.where(i > 0, prev_ref[...], 0.0)
        ext = jnp.concatenate([prev, a_ref[...]], axis=0)
        c = _conv3(ext, w_ref, b_ref)[HALO:, :]
        cg, cu = c[:, :FB], c[:, FB:]
        act_ref[...] = (cg * jax.nn.sigmoid(cg) * cu).astype(MXU)

    return pl.pallas_call(
        body, name=name, out_shape=jax.ShapeDtypeStruct((LP, F), MXU), grid=(F // FB, LP // tm),
        in_specs=[pl.BlockSpec((tm, 2 * FB), lambda j, i: (i, j)),
                  pl.BlockSpec((HALO, 2 * FB), lambda j, i: (jnp.maximum(i * nb - 1, 0), j)),
                  pl.BlockSpec((3, 2 * FB), lambda j, i: (0, j)),
                  pl.BlockSpec((1, 2 * FB), lambda j, i: (0, j))],
        out_specs=pl.BlockSpec((tm, FB), lambda j, i: (i, j)),
        compiler_params=_cp(("parallel", "parallel")))(a, a, cw, cb)


def conv_act_bwd(cfg, a, d_act, cw, cb, name):
    LP, F, FB, tm = cfg.LP, cfg.F, cfg.FB, cfg.tr
    nb, last = tm // HALO, LP // tm - 1
    nrow = LP // HALO

    def body(a_ref, aprev_ref, anext_ref, d_ref, dnext_ref, w_ref, b_ref, da_ref, dcv_ref):
        i = pl.program_id(1)
        prev = jnp.where(i > 0, aprev_ref[...], 0.0)
        ext = jnp.concatenate([prev, a_ref[...], anext_ref[...]], axis=0)
        c = _conv3(ext, w_ref, b_ref)
        cg, cu = c[:, :FB], c[:, FB:]
        dnext = jnp.where(i < last, dnext_ref[...], 0.0)
        dact = jnp.concatenate([jnp.zeros((HALO, FB), F32), d_ref[...], dnext], axis=0)
        sg = jax.nn.sigmoid(cg)
        silu = cg * sg
        d_cg = dact * cu * (sg * (1.0 + cg * (1.0 - sg)))
        d_cu = dact * silu
        dc = jnp.concatenate([d_cg, d_cu], axis=1)
        n = tm + 2 * HALO
        da = (w_ref[2:3, :] * dc + w_ref[1:2, :] * pltpu.roll(dc, n - 1, 0)
              + w_ref[0:1, :] * pltpu.roll(dc, n - 2, 0))
        da_ref[...] = da[HALO:HALO + tm, :].astype(MXU)
        dcb = dc[HALO:HALO + tm, :]

        @pl.when(i == 0)
        def _():
            dcv_ref[...] = jnp.zeros_like(dcv_ref)

        dcv_ref[0:1, :] += jnp.sum(dcb * pltpu.roll(ext, 2, 0)[HALO:HALO + tm, :], axis=0, keepdims=True)
        dcv_ref[1:2, :] += jnp.sum(dcb * pltpu.roll(ext, 1, 0)[HALO:HALO + tm, :], axis=0, keepdims=True)
        dcv_ref[2:3, :] += jnp.sum(dcb * ext[HALO:HALO + tm, :], axis=0, keepdims=True)
        dcv_ref[3:4, :] += jnp.sum(dcb, axis=0, keepdims=True)

    return pl.pallas_call(
        body, name=name,
        out_shape=(jax.ShapeDtypeStruct((LP, 2 * F), MXU), jax.ShapeDtypeStruct((8, 2 * F), F32)),
        grid=(F // FB, LP // tm),
        in_specs=[pl.BlockSpec((tm, 2 * FB), lambda j, i: (i, j)),
                  pl.BlockSpec((HALO, 2 * FB), lambda j, i: (jnp.maximum(i * nb - 1, 0), j)),
                  pl.BlockSpec((HALO, 2 * FB), lambda j, i: (jnp.minimum((i + 1) * nb, nrow - 1), j)),
                  pl.BlockSpec((tm, FB), lambda j, i: (i, j)),
                  pl.BlockSpec((HALO, FB), lambda j, i: (jnp.minimum((i + 1) * nb, nrow - 1), j)),
                  pl.BlockSpec((3, 2 * FB), lambda j, i: (0, j)),
                  pl.BlockSpec((1, 2 * FB), lambda j, i: (0, j))],
        out_specs=(pl.BlockSpec((tm, 2 * FB), lambda j, i: (i, j)),
                   pl.BlockSpec((8, 2 * FB), lambda j, i: (0, j))),
        compiler_params=_cp(("parallel", "arbitrary")))(a, a, a, d_act, d_act, cw, cb)


def fox_gate_fwd(cfg, f_logit, b_pad, name):
    LP, tb = cfg.LP, cfg.tq

    def body(f_ref, b_ref, c_ref, carry_ref):
        @pl.when(pl.program_id(0) == 0)
        def _():
            carry_ref[...] = jnp.zeros_like(carry_ref)

        xv = f_ref[...] + b_ref[...]
        lf = -_softplus(-xv)
        r = lax.broadcasted_iota(jnp.int32, (tb, tb), 0)
        s = lax.broadcasted_iota(jnp.int32, (tb, tb), 1)
        c = _split_dot_left((s <= r).astype(MXU), lf, 3) + carry_ref[0:1, :]
        c_ref[...] = c
        carry_ref[0:1, :] = c[tb - 1:tb, :]

    blk = pl.BlockSpec((tb, LANES), lambda i: (i, 0))
    return pl.pallas_call(
        body, name=name, out_shape=jax.ShapeDtypeStruct((LP, LANES), F32), grid=(LP // tb,),
        in_specs=[blk, pl.BlockSpec((1, LANES), lambda i: (0, 0))], out_specs=blk,
        scratch_shapes=[pltpu.VMEM((8, LANES), F32)], compiler_params=_cp(("arbitrary",)))(f_logit, b_pad)


def fox_gate_bwd(cfg, dc, f_logit, b_pad, name):
    LP, tb = cfg.LP, cfg.tq
    nblk = LP // tb

    def body(dc_ref, f_ref, b_ref, df_ref, db_ref, carry_ref):
        @pl.when(pl.program_id(0) == 0)
        def _():
            carry_ref[...] = jnp.zeros_like(carry_ref)
            db_ref[...] = jnp.zeros_like(db_ref)

        r = lax.broadcasted_iota(jnp.int32, (tb, tb), 0)
        s = lax.broadcasted_iota(jnp.int32, (tb, tb), 1)
        dlf = _split_dot_left((s >= r).astype(MXU), dc_ref[...], 3) + carry_ref[0:1, :]
        carry_ref[0:1, :] = dlf[0:1, :]
        df = dlf * jax.nn.sigmoid(-(f_ref[...] + b_ref[...]))
        df_ref[...] = df.astype(MXU)
        db_ref[...] += jnp.sum(df, axis=0, keepdims=True)

    blk = pl.BlockSpec((tb, LANES), lambda i: (nblk - 1 - i, 0))
    vec = pl.BlockSpec((1, LANES), lambda i: (0, 0))
    return pl.pallas_call(
        body, name=name,
        out_shape=(jax.ShapeDtypeStruct((LP, LANES), MXU), jax.ShapeDtypeStruct((1, LANES), F32)),
        grid=(nblk,), in_specs=[blk, blk, vec], out_specs=(blk, vec),
        scratch_shapes=[pltpu.VMEM((8, LANES), F32)], compiler_params=_cp(("arbitrary",)))(dc, f_logit, b_pad)


def _head_norm_fwd(o, g):
    return (o * lax.rsqrt(jnp.mean(o * o, axis=-1, keepdims=True) + EPS)) * g


def _head_norm_bwd(o, g, d_on):
    r = lax.rsqrt(jnp.mean(o * o, axis=-1, keepdims=True) + EPS)
    ohat = o * r
    gdy = d_on * g
    d_o = r * (gdy - ohat * jnp.mean(gdy * ohat, axis=-1, keepdims=True))
    return d_o, jnp.sum(d_on * ohat, axis=0, keepdims=True)


def _diag_masks(tq, strict):
    rows = lax.broadcasted_iota(jnp.int32, (tq, HD), 0)
    cols = lax.broadcasted_iota(jnp.int32, (tq, HD), 1)
    return [(cols + kk * HD < rows) if strict else (cols + kk * HD <= rows) for kk in range(tq // HD)]


def _tri(pred):
    a = lax.broadcasted_iota(jnp.int32, (HD, HD), 0)
    b = lax.broadcasted_iota(jnp.int32, (HD, HD), 1)
    return pred(a, b).astype(MXU)


def _attn_specs(cfg, group):
    base = 3 * cfg.NH * group
    return base, base + cfg.NH, base + 2 * cfg.NH


def sb_fwd(cfg, qkv, g3, name, xfer=None):
    LP, NH, tq = cfg.LP, cfg.NH, cfg.tq
    nq, R = LP // tq, tq // HD
    scale = HD ** -0.5
    cq, ck, cv = _attn_specs(cfg, 0)

    def body(q_ref, k_ref, v_ref, g_ref, opre_ref, on_ref, rtot_ref):
        i = pl.program_id(1)
        q = q_ref[...]
        masks = _diag_masks(tq, True)
        m_after = _tri(lambda a, b: a > b)

        def sup(J, carry, masked):
            acc, rc = carry
            order = range(R - 1, -1, -1)
            offs = [pl.multiple_of((J * R + kk) * HD, HD) for kk in range(R)]
            zs = [lax.dot_general(q, k_ref[pl.ds(offs[kk], HD), :], NT, preferred_element_type=F32) * scale
                  for kk in range(R)]
            ls, lks, tris = [None] * R, [None] * R, [None] * R
            for kk in order:
                sp = _softplus(zs[kk])
                lks[kk] = jnp.where(masks[kk], -sp, 0.0) if masked else -sp
                tris[kk] = _split_dot(lks[kk], m_after, LK_PIECES)
                ls[kk] = zs[kk] - sp
            for kk in order:
                a = jnp.exp(ls[kk] + (tris[kk] + rc))
                if masked:
                    a = jnp.where(masks[kk], a, 0.0)
                acc = acc + jnp.dot(a.astype(MXU), v_ref[pl.ds(offs[kk], HD), :], preferred_element_type=F32)
                rc = rc + jnp.sum(lks[kk], axis=1, keepdims=True)
            return acc, rc

        carry = sup(i, (jnp.zeros((tq, HD), F32), jnp.zeros((tq, 1), F32)), True)
        acc, rc = lax.fori_loop(0, i, lambda it, c: sup(i - 1 - it, c, False), carry)
        opre_ref[...] = acc
        on_ref[...] = _head_norm_fwd(acc, g_ref[0]).astype(MXU)
        rtot_ref[0] = rc

    return call_with_exchange(
        body, name, (NH, nq),
        [pl.BlockSpec((tq, HD), lambda h, i: (i, cq + h)),
         pl.BlockSpec((LP, HD), lambda h, i: (0, ck + h)),
         pl.BlockSpec((LP, HD), lambda h, i: (0, cv + h)),
         pl.BlockSpec((1, 1, HD), lambda h, i: (h, 0, 0))],
        (pl.BlockSpec((tq, HD), lambda h, i: (i, h)),
         pl.BlockSpec((tq, HD), lambda h, i: (i, h)),
         pl.BlockSpec((1, tq, 1), lambda h, i: (h, i, 0))),
        (jax.ShapeDtypeStruct((LP, cfg.WG), F32), jax.ShapeDtypeStruct((LP, cfg.WG), MXU),
         jax.ShapeDtypeStruct((NH, LP, 1), F32)),
        [], (qkv, qkv, qkv, g3), xfer)


def sb_bwd(cfg, qkv, g3, o_pre, d_on, rtot, name, xfer=None):
    LP, NH, tq = cfg.LP, cfg.NH, cfg.tq
    nq, R = LP // tq, tq // HD
    scale = HD ** -0.5
    cq, ck, cv = _attn_specs(cfg, 0)

    def body(q_ref, k_ref, v_ref, g_ref, o_ref, don_ref, rtot_ref, dq_ref, dk_ref, dv_ref, dg_ref,
             dk_acc, dv_acc):
        i = pl.program_id(1)

        @pl.when(i == 0)
        def _():
            dk_acc[...] = jnp.zeros_like(dk_acc)
            dv_acc[...] = jnp.zeros_like(dv_acc)
            dg_ref[...] = jnp.zeros_like(dg_ref)

        q = q_ref[...]
        d_o, dg = _head_norm_bwd(o_ref[...], g_ref[0], don_ref[...])
        dg_ref[0] += dg
        do_b = d_o.astype(MXU)
        rt = rtot_ref[0]
        masks = _diag_masks(tq, True)
        m_le = _tri(lambda a, b: a <= b)
        m_lt = _tri(lambda a, b: a < b)

        def sup(J, carry, masked):
            dq, lc, pc = carry
            offs = [pl.multiple_of((J * R + kk) * HD, HD) for kk in range(R)]
            zs = [lax.dot_general(q, k_ref[pl.ds(offs[kk], HD), :], NT, preferred_element_type=F32) * scale
                  for kk in range(R)]
            das = [lax.dot_general(do_b, v_ref[pl.ds(offs[kk], HD), :], NT, preferred_element_type=F32)
                   for kk in range(R)]
            ls, lks, tri1 = [], [], []
            for kk in range(R):
                sp = _softplus(zs[kk])
                lk = jnp.where(masks[kk], -sp, 0.0) if masked else -sp
                tri1.append(_split_dot(lk, m_le, LK_PIECES))
                lks.append(lk)
                ls.append(zs[kk] - sp)
            ggs, a_bs, tri2 = [], [], []
            for kk in range(R):
                a = jnp.exp(ls[kk] + ((rt - lc) - tri1[kk]))
                if masked:
                    a = jnp.where(masks[kk], a, 0.0)
                gg = a * das[kk]
                tri2.append(_split_dot(gg, m_lt, 2))
                ggs.append(gg)
                a_bs.append(a.astype(MXU))
                lc = lc + jnp.sum(lks[kk], axis=1, keepdims=True)
            for kk in range(R):
                sig = jnp.exp(ls[kk])
                dz = (ggs[kk] * (1.0 - sig) - sig * (pc + tri2[kk])) * scale
                if masked:
                    dz = jnp.where(masks[kk], dz, 0.0)
                dz_b = dz.astype(MXU)
                dq = dq + jnp.dot(dz_b, k_ref[pl.ds(offs[kk], HD), :], preferred_element_type=F32)
                dk_acc[pl.ds(offs[kk], HD), :] += lax.dot_general(dz_b, q, TN, preferred_element_type=F32)
                dv_acc[pl.ds(offs[kk], HD), :] += lax.dot_general(a_bs[kk], do_b, TN, preferred_element_type=F32)
                pc = pc + jnp.sum(ggs[kk], axis=1, keepdims=True)
            return dq, lc, pc

        zc = jnp.zeros((tq, 1), F32)
        carry = lax.fori_loop(0, i, lambda J, c: sup(J, c, False), (jnp.zeros((tq, HD), F32), zc, zc))
        dq, _, _ = sup(i, carry, True)
        dq_ref[...] = dq.astype(MXU)

        @pl.when(i == nq - 1)
        def _():
            dk_ref[...] = dk_acc[...].astype(MXU)
            dv_ref[...] = dv_acc[...].astype(MXU)

    blk = pl.BlockSpec((tq, HD), lambda h, i: (i, h))
    full = pl.BlockSpec((LP, HD), lambda h, i: (0, h))
    gspec = pl.BlockSpec((1, 1, HD), lambda h, i: (h, 0, 0))
    return call_with_exchange(
        body, name, (NH, nq),
        [pl.BlockSpec((tq, HD), lambda h, i: (i, cq + h)),
         pl.BlockSpec((LP, HD), lambda h, i: (0, ck + h)),
         pl.BlockSpec((LP, HD), lambda h, i: (0, cv + h)),
         gspec, blk, blk, pl.BlockSpec((1, tq, 1), lambda h, i: (h, i, 0))],
        (blk, full, full, gspec),
        (jax.ShapeDtypeStruct((LP, cfg.WG), MXU),) * 3 + (jax.ShapeDtypeStruct((NH, 1, HD), F32),),
        [pltpu.VMEM((LP, HD), F32), pltpu.VMEM((LP, HD), F32)],
        (qkv, qkv, qkv, g3, o_pre, d_on, rtot), xfer)


def fox_fwd(cfg, qkv, g3, c_col, c_row, name, xfer=None):
    LP, NH, tq = cfg.LP, cfg.NH, cfg.tq
    nq, R = LP // tq, tq // HD
    scale = HD ** -0.5
    cq, ck, cv = _attn_specs(cfg, 1)

    def body(q_ref, k_ref, v_ref, g_ref, ccol_ref, crow_ref, opre_ref, on_ref, lse_ref):
        i = pl.program_id(1)
        q = q_ref[...]
        cqv = ccol_ref[0]
        masks = _diag_masks(tq, False)

        def sup(J, carry, masked):
            acc, m, l = carry
            ss, offs = [], []
            for kk in range(R):
                j = J * R + kk
                off = pl.multiple_of(j * HD, HD)
                s = (lax.dot_general(q, k_ref[pl.ds(off, HD), :], NT, preferred_element_type=F32) * scale
                     + (cqv - crow_ref[0, pl.ds(j, 1), :]))
                ss.append(jnp.where(masks[kk], s, NEG) if masked else s)
                offs.append(off)
            mx = jnp.max(ss[0], axis=1, keepdims=True)
            for s in ss[1:]:
                mx = jnp.maximum(mx, jnp.max(s, axis=1, keepdims=True))
            m_new = jnp.maximum(m, mx)
            alpha = jnp.exp(m - m_new)
            acc, l = alpha * acc, alpha * l
            for s, off in zip(ss, offs):
                p = jnp.exp(s - m_new)
                l = l + jnp.sum(p, axis=1, keepdims=True)
                acc = acc + _split_dot(p, v_ref[pl.ds(off, HD), :], 2)
            return acc, m_new, l

        carry = (jnp.zeros((tq, HD), F32), jnp.full((tq, 1), NEG, F32), jnp.zeros((tq, 1), F32))
        carry = lax.fori_loop(0, i, lambda J, c: sup(J, c, False), carry)
        acc, m, l = sup(i, carry, True)
        o = acc / l
        opre_ref[...] = o
        on_ref[...] = _head_norm_fwd(o, g_ref[0]).astype(MXU)
        lse_ref[0] = m + jnp.log(l)

    return call_with_exchange(
        body, name, (NH, nq),
        [pl.BlockSpec((tq, HD), lambda h, i: (i, cq + h)),
         pl.BlockSpec((LP, HD), lambda h, i: (0, ck + h)),
         pl.BlockSpec((LP, HD), lambda h, i: (0, cv + h)),
         pl.BlockSpec((1, 1, HD), lambda h, i: (h, 0, 0)),
         pl.BlockSpec((1, tq, 1), lambda h, i: (h, i, 0)),
         pl.BlockSpec((1, LP // HD, HD), lambda h, i: (h, 0, 0))],
        (pl.BlockSpec((tq, HD), lambda h, i: (i, h)),
         pl.BlockSpec((tq, HD), lambda h, i: (i, h)),
         pl.BlockSpec((1, tq, 1), lambda h, i: (h, i, 0))),
        (jax.ShapeDtypeStruct((LP, cfg.WG), F32), jax.ShapeDtypeStruct((LP, cfg.WG), MXU),
         jax.ShapeDtypeStruct((NH, LP, 1), F32)),
        [], (qkv, qkv, qkv, g3, c_col, c_row), xfer)


def fox_bwd(cfg, qkv, g3, c_col, c_row, o_pre, d_on, lse, name, xfer=None):
    LP, NH, tq = cfg.LP, cfg.NH, cfg.tq
    nq, R = LP // tq, tq // HD
    scale = HD ** -0.5
    cq, ck, cv = _attn_specs(cfg, 1)

    def body(q_ref, k_ref, v_ref, g_ref, ccol_ref, crow_ref, o_ref, don_ref, lse_ref,
             dq_ref, dk_ref, dv_ref, dg_ref, dc_ref, dk_acc, dv_acc):
        i = pl.program_id(1)

        @pl.when(i == 0)
        def _():
            dk_acc[...] = jnp.zeros_like(dk_acc)
            dv_acc[...] = jnp.zeros_like(dv_acc)
            dg_ref[...] = jnp.zeros_like(dg_ref)
            dc_ref[...] = jnp.zeros_like(dc_ref)

        q = q_ref[...]
        ov = o_ref[...]
        d_o, dg = _head_norm_bwd(ov, g_ref[0], don_ref[...])
        dg_ref[0] += dg
        do_b = d_o.astype(MXU)
        delta = jnp.sum(do_b.astype(F32) * ov, axis=1, keepdims=True)
        cqv, lsev = ccol_ref[0], lse_ref[0]
        cl = cqv - lsev
        masks = _diag_masks(tq, False)

        def sup(J, dq, masked):
            offs = [pl.multiple_of((J * R + kk) * HD, HD) for kk in range(R)]
            ss = [lax.dot_general(q, k_ref[pl.ds(offs[kk], HD), :], NT, preferred_element_type=F32) * scale
                  + (cl - crow_ref[0, pl.ds(J * R + kk, 1), :]) for kk in range(R)]
            dps = [lax.dot_general(do_b, v_ref[pl.ds(offs[kk], HD), :], NT, preferred_element_type=F32)
                   for kk in range(R)]
            for kk in range(R):
                p = jnp.exp(ss[kk])
                if masked:
                    p = jnp.where(masks[kk], p, 0.0)
                ds = p * (dps[kk] - delta)
                dc_ref[0, pl.ds(J * R + kk, 1), :] -= jnp.sum(ds, axis=0, keepdims=True)
                ds_b = (ds * scale).astype(MXU)
                dk_acc[pl.ds(offs[kk], HD), :] += lax.dot_general(ds_b, q, TN, preferred_element_type=F32)
                dv_acc[pl.ds(offs[kk], HD), :] += lax.dot_general(p.astype(MXU), do_b, TN,
                                                                  preferred_element_type=F32)
                dq = dq + jnp.dot(ds_b, k_ref[pl.ds(offs[kk], HD), :], preferred_element_type=F32)
            return dq

        dq = lax.fori_loop(0, i, lambda J, c: sup(J, c, False), jnp.zeros((tq, HD), F32))
        dq = sup(i, dq, True)
        dq_ref[...] = dq.astype(MXU)

        @pl.when(i == nq - 1)
        def _():
            dk_ref[...] = dk_acc[...].astype(MXU)
            dv_ref[...] = dv_acc[...].astype(MXU)

    blk = pl.BlockSpec((tq, HD), lambda h, i: (i, h))
    full = pl.BlockSpec((LP, HD), lambda h, i: (0, h))
    gspec = pl.BlockSpec((1, 1, HD), lambda h, i: (h, 0, 0))
    col = pl.BlockSpec((1, tq, 1), lambda h, i: (h, i, 0))
    rowv = pl.BlockSpec((1, LP // HD, HD), lambda h, i: (h, 0, 0))
    return call_with_exchange(
        body, name, (NH, nq),
        [pl.BlockSpec((tq, HD), lambda h, i: (i, cq + h)),
         pl.BlockSpec((LP, HD), lambda h, i: (0, ck + h)),
         pl.BlockSpec((LP, HD), lambda h, i: (0, cv + h)),
         gspec, col, rowv, blk, blk, col],
        (blk, full, full, gspec, rowv),
        (jax.ShapeDtypeStruct((LP, cfg.WG), MXU),) * 3
        + (jax.ShapeDtypeStruct((NH, 1, HD), F32), jax.ShapeDtypeStruct((NH, LP // HD, HD), F32)),
        [pltpu.VMEM((LP, HD), F32), pltpu.VMEM((LP, HD), F32)],
        (qkv, qkv, qkv, g3, c_col, c_row, o_pre, d_on, lse), xfer)


def _row_tile(rows, cols, n_arrays):
    budget = 24 * 1024 * 1024 // (2 * n_arrays * cols * 4)
    cap = max(8, min(rows, budget // 8 * 8))
    div = _tile(rows, cap, 8)
    return div if div <= cap and div * 4 >= cap else cap


def sum_slots(recvs, name):
    S, rows, cols = recvs[0].shape
    nl = len(recvs)
    tr = _tile(rows, _row_tile(rows, cols, nl * S + 1), 8)
    nb = rows // tr

    def body(*refs):
        o_ref = refs[nl]
        layer = pl.program_id(0)
        for ll in range(nl):
            @pl.when(layer == ll)
            def _(r_ref=refs[ll]):
                acc = r_ref[0].astype(F32)
                for s in range(1, S):
                    acc = acc + r_ref[s].astype(F32)
                o_ref[...] = acc

    def spec(ll):
        return pl.BlockSpec((S, tr, cols), lambda l, i: (0, jnp.where(l == ll, i, jnp.where(l < ll, 0, nb - 1)), 0))

    return pl.pallas_call(
        body, name=name, out_shape=jax.ShapeDtypeStruct((nl * rows, cols), F32), grid=(nl, nb),
        in_specs=[spec(ll) for ll in range(nl)],
        out_specs=pl.BlockSpec((tr, cols), lambda l, i: (l * nb + i, 0)),
        compiler_params=_cp(("arbitrary", "arbitrary")))(*recvs)


def adamw(w, m, v, g_parts, name):
    rows, cols = w.shape
    npart = len(g_parts)
    tr = _row_tile(rows, cols, 7 + npart)
    c1 = 1.0 - ADAM_B1 ** ADAM_STEP
    c2 = 1.0 - ADAM_B2 ** ADAM_STEP

    def body(*refs):
        w_ref, m_ref, v_ref = refs[:3]
        g_refs = refs[3:3 + npart]
        g_out, d_out, m_out, v_out = refs[3 + npart:]
        g = g_refs[0][...]
        for r in g_refs[1:]:
            g = g + r[...]
        g_out[...] = g
        mn = ADAM_B1 * m_ref[...] + (1.0 - ADAM_B1) * g
        vn = ADAM_B2 * v_ref[...] + (1.0 - ADAM_B2) * (g * g)
        m_out[...] = mn
        v_out[...] = vn
        d_out[...] = -ADAM_LR * ((mn / c1) / (jnp.sqrt(vn / c2) + ADAM_EPS) + ADAM_WD * w_ref[...])

    blk = pl.BlockSpec((tr, cols), lambda i: (i, 0))
    return pl.pallas_call(
        body, name=name, out_shape=(jax.ShapeDtypeStruct((rows, cols), F32),) * 4,
        grid=(pl.cdiv(rows, tr),), in_specs=[blk] * (3 + npart), out_specs=(blk,) * 4,
        compiler_params=_cp(("parallel",)))(w, m, v, *g_parts)


def _coords():
    return lax.axis_index("x"), lax.axis_index("y"), lax.axis_index("c")


def chip_exchange(arrs, scatter, name):
    n = len(arrs)

    def body(*refs):
        ins, outs, sems = refs[:n], refs[n:2 * n], refs[2 * n:]
        _xfer_start(ins, outs, sems, scatter)
        _xfer_wait(ins, outs, sems, scatter)

    out_shape, scratch = _xfer_shapes(arrs, scatter)
    any_spec = pl.BlockSpec(memory_space=pl.ANY)
    return pl.pallas_call(body, name=name, out_shape=out_shape, in_specs=[any_spec] * n, out_specs=(any_spec,) * n,
                          scratch_shapes=scratch)(*arrs)


def _xfer_shapes(arrs, scatter):
    n = len(arrs)
    shapes = [a.shape[1:] if scatter else a.shape for a in arrs]
    out_shape = tuple(jax.ShapeDtypeStruct((NCHIP,) + tuple(s), a.dtype) for s, a in zip(shapes, arrs))
    scratch = [pltpu.SemaphoreType.DMA((n, 3)), pltpu.SemaphoreType.DMA((n, 3)), pltpu.SemaphoreType.DMA((n,))]
    return out_shape, scratch


def _xfer_copies(ins, outs, sems, scatter):
    send_sems, recv_sems, local_sems = sems
    n = len(ins)
    x, y, c = _coords()
    me = 2 * x + y
    peers = [(1 - x, y), (x, 1 - y), (1 - x, 1 - y)]

    def src(a, to):
        return ins[a].at[to] if scatter else ins[a]

    local = [pltpu.make_async_copy(src(a, me), outs[a].at[me], local_sems.at[a]) for a in range(n)]
    sends, recvs = [], []
    for a in range(n):
        for k, (px, py) in enumerate(peers):
            sends.append(pltpu.make_async_remote_copy(
                src_ref=src(a, 2 * px + py), dst_ref=outs[a].at[me], send_sem=send_sems.at[a, k],
                recv_sem=recv_sems.at[a, k], device_id=(px, py, c), device_id_type=MESH))
            recvs.append(pltpu.make_async_remote_copy(
                src_ref=src(a, me), dst_ref=outs[a].at[2 * px + py], send_sem=send_sems.at[a, k],
                recv_sem=recv_sems.at[a, k], device_id=(px, py, c), device_id_type=MESH))
    return local, sends, recvs


def _xfer_start(ins, outs, sems, scatter):
    local, sends, _ = _xfer_copies(ins, outs, sems, scatter)
    for cp in local + sends:
        cp.start()


def _xfer_wait(ins, outs, sems, scatter):
    local, sends, recvs = _xfer_copies(ins, outs, sems, scatter)
    for cp in recvs:
        cp.wait_recv()
    for cp in sends:
        cp.wait_send()
    for cp in local:
        cp.wait()


def call_with_exchange(core, name, grid, in_specs, out_specs, out_shape, scratch, args, xfer):
    n_in, n_out, n_scr = len(in_specs), len(out_specs), len(scratch)
    if xfer is None:
        res = pl.pallas_call(core, name=name, out_shape=out_shape, grid=grid, in_specs=in_specs,
                             out_specs=out_specs, scratch_shapes=scratch,
                             compiler_params=_cp(("arbitrary",) * len(grid)))(*args)
        return res, ()
    arrs, scatter = xfer
    nx = len(arrs)
    x_shape, x_scratch = _xfer_shapes(arrs, scatter)

    def body(*refs):
        a, xi = refs[:n_in], refs[n_in:n_in + nx]
        o, xo = refs[n_in + nx:n_in + nx + n_out], refs[n_in + nx + n_out:n_in + 2 * nx + n_out]
        rest = refs[n_in + 2 * nx + n_out:]
        scr, sems = rest[:n_scr], rest[n_scr:]
        first, last = None, None
        for d, g in enumerate(grid):
            f, l = pl.program_id(d) == 0, pl.program_id(d) == g - 1
            first = f if first is None else first & f
            last = l if last is None else last & l

        @pl.when(first)
        def _():
            _xfer_start(xi, xo, sems, scatter)

        core(*a, *o, *scr)

        @pl.when(last)
        def _():
            _xfer_wait(xi, xo, sems, scatter)

    any_spec = pl.BlockSpec(memory_space=pl.ANY)
    res = pl.pallas_call(
        body, name=name, out_shape=tuple(out_shape) + tuple(x_shape), grid=grid,
        in_specs=list(in_specs) + [any_spec] * nx, out_specs=tuple(out_specs) + (any_spec,) * nx,
        scratch_shapes=list(scratch) + x_scratch,
        compiler_params=_cp(("arbitrary",) * len(grid)))(*args, *arrs)
    return res[:n_out], res[n_out:]


def sibling_exchange(arrs, name):
    n = len(arrs)

    def body(*refs):
        ins, outs = refs[:n], refs[n:2 * n]
        send_sems, recv_sems = refs[2 * n:]
        x, y, c = _coords()
        cps = [pltpu.make_async_remote_copy(src_ref=ins[a], dst_ref=outs[a], send_sem=send_sems.at[a],
                                            recv_sem=recv_sems.at[a], device_id=(x, y, 1 - c),
                                            device_id_type=MESH) for a in range(n)]
        for cp in cps:
            cp.start()
        for cp in cps:
            cp.wait()

    any_spec = pl.BlockSpec(memory_space=pl.ANY)
    return pl.pallas_call(
        body, name=name, out_shape=tuple(jax.ShapeDtypeStruct(a.shape, a.dtype) for a in arrs),
        in_specs=[any_spec] * n, out_specs=(any_spec,) * n,
        scratch_shapes=[pltpu.SemaphoreType.DMA((n,)), pltpu.SemaphoreType.DMA((n,))],
    )(*arrs)


def all_reduce_small(pack, name):
    R = pack.shape[0]

    def body(p_ref, o_ref, slots, send_sems, recv_sems):
        x, y, c = _coords()
        me = 4 * x + 2 * y + c
        slots[me] = p_ref[...]
        cps = []
        for k in range(1, 8):
            kx, ky, kc = (k >> 2) & 1, (k >> 1) & 1, k & 1
            peer = ((1 - x) if kx else x, (1 - y) if ky else y, (1 - c) if kc else c)
            cp = pltpu.make_async_remote_copy(src_ref=p_ref, dst_ref=slots.at[me], send_sem=send_sems.at[k],
                                              recv_sem=recv_sems.at[k], device_id=peer, device_id_type=MESH)
            cp.start()
            cps.append((cp, peer))
        for k, (cp, peer) in enumerate(cps, start=1):
            frm = 4 * peer[0] + 2 * peer[1] + peer[2]
            pltpu.make_async_remote_copy(src_ref=p_ref, dst_ref=slots.at[frm], send_sem=send_sems.at[k],
                                         recv_sem=recv_sems.at[k], device_id=peer, device_id_type=MESH).wait_recv()
        for cp, _ in cps:
            cp.wait_send()
        acc = slots[0]
        for s in range(1, 8):
            acc = acc + slots[s]
        o_ref[...] = acc

    vm = pl.BlockSpec(memory_space=pltpu.VMEM)
    return pl.pallas_call(
        body, name=name, out_shape=jax.ShapeDtypeStruct((R, LANES), F32), in_specs=[vm], out_specs=vm,
        scratch_shapes=[pltpu.VMEM((8, R, LANES), F32), pltpu.SemaphoreType.DMA((8,)),
                        pltpu.SemaphoreType.DMA((8,))],
        compiler_params=pltpu.CompilerParams(vmem_limit_bytes=VMEM_LIMIT),
    )(pack)


def _ffperm(w, cfg):
    lead, nb = w.shape[:-1], cfg.F // cfg.FB
    return w.reshape(*lead, 2, nb, cfg.FB).swapaxes(-3, -2).reshape(*lead, cfg.F2)


def _ffunperm(w, cfg):
    lead, nb = w.shape[:-1], cfg.F // cfg.FB
    return w.reshape(*lead, nb, 2, cfg.FB).swapaxes(-3, -2).reshape(*lead, cfg.F2)


def _pack(arrs):
    parts = []
    for a in arrs:
        f = a.reshape(-1).astype(F32)
        parts.append(jnp.pad(f, (0, -f.shape[0] % 1024)).reshape(-1, LANES))
    return jnp.concatenate(parts, axis=0)


def _unpack(p, shapes):
    out, r = [], 0
    for s in shapes:
        n = math.prod(s)
        nr = (n + 1023) // 1024 * 8
        out.append(p[r:r + nr].reshape(-1)[:n].reshape(s))
        r += nr
    return out


def _vec(g):
    return g.reshape(1, -1)


class _LocalWeights:
    def __init__(self, cfg, wf):
        self.cfg, self.wf, self.grads = cfg, wf, {}

    def w_in(self, l):
        return self.wf['w_in'][l]

    def fwd_exchanges(self, l):
        return None, None

    def rest(self, l, got_sb, got_fox):
        return dict(w_out=self.wf['w_out'][l], w_up=_ffperm(self.wf['w_up'][l], self.cfg), w_down=self.wf['w_down'][l])

    def bwd_exchanges(self, l, g):
        return None, None

    def bwd_done(self, l, g, got_sb, got_fox):
        self.grads[l] = g


def _w_in_parts(cfg, w_in):
    w_f = jnp.pad(w_in[:, cfg.NQKV:], ((0, 0), (0, LANES - cfg.NH)))
    return dict(w_qkv=w_in[:, :cfg.NQKV], w_f=w_f, w_in_ext=jnp.concatenate([w_in[:, :cfg.NQKV], w_f], axis=1))


def _layer_fwd(cfg, l, h, u, wl, io):
    tag = f"l{l}"
    qkv = matmul(u, wl['w_qkv'], 'nn', MXU, f"{tag}_qkv", tn_cap=1024)
    f_logit = matmul(u, wl['w_f'], 'nn', F32, f"{tag}_fproj")
    cpre = fox_gate_fwd(cfg, f_logit, wl['b_pad'], f"{tag}_gate_fwd")
    c_heads = cpre[:, :cfg.NH].T
    c_col = c_heads[:, :, None]
    c_row = c_heads.reshape(cfg.NH, cfg.LP // HD, HD)
    x_sb, x_fox = io.fwd_exchanges(l)
    (o_sb, on_sb, rtot), got_sb = sb_fwd(cfg, qkv, wl['g_sb'], f"{tag}_sb_fwd", x_sb)
    (o_fx, on_fx, lse), got_fox = fox_fwd(cfg, qkv, wl['g_fox'], c_col, c_row, f"{tag}_fox_fwd", x_fox)
    wl.update(io.rest(l, got_sb, got_fox))
    mixin = jnp.concatenate([on_sb, on_fx], axis=1)
    mix = matmul(mixin, wl['w_out'], 'nn', F32, f"{tag}_out")
    h1, u2 = resid_norm(cfg, mix, h, wl['g_mix_post'], wl['g_ffn_pre'], f"{tag}_mixres")
    a = matmul(u2, wl['w_up'], 'nn', F32, f"{tag}_up")
    act = conv_act_fwd(cfg, a, wl['conv_w'], wl['conv_b'], f"{tag}_conv_fwd")
    ff = matmul(act, wl['w_down'], 'nn', F32, f"{tag}_down", tm_cap=704, tk_cap=2816)
    h2, u_next = resid_norm(cfg, ff, h1, wl['g_ffn_post'], wl['g_next'], f"{tag}_ffnres")
    saved = dict(h=h, u=u, qkv=qkv, f_logit=f_logit, c_col=c_col, c_row=c_row, o_sb=o_sb, o_fx=o_fx, rtot=rtot,
                 lse=lse, mixin=mixin, mix=mix, h1=h1, u2=u2, a=a, act=act, ff=ff)
    return h2, u_next, saved


def _layer_bwd(cfg, l, dh2, wl, sv, io):
    tag = f"l{l}"
    g = {}
    d_ff, g['g_ffn_post'] = norm_bwd(cfg, sv['ff'], wl['g_ffn_post'], dh2, None, MXU, f"{tag}_ffnpost_bwd")
    d_act = matmul(d_ff, wl['w_down'], 'nt', F32, f"{tag}_down_dx")
    g['w_down'] = matmul(sv['act'], d_ff, 'tn', F32, f"{tag}_down_dw", tm_cap=1408, tk_cap=1408)
    d_a, d_conv = conv_act_bwd(cfg, sv['a'], d_act, wl['conv_w'], wl['conv_b'], f"{tag}_conv_bwd")
    g['conv'] = d_conv
    g['w_up'] = matmul(sv['u2'], d_a, 'tn', F32, f"{tag}_up_dw", tm_cap=1024, tk_cap=1408)
    du2 = matmul(d_a, wl['w_up'], 'nt', F32, f"{tag}_up_dx", tm_cap=704, tk_cap=2816)
    dh1, g['g_ffn_pre'] = norm_bwd(cfg, sv['h1'], wl['g_ffn_pre'], du2, dh2, F32, f"{tag}_ffnpre_bwd")
    d_mix, g['g_mix_post'] = norm_bwd(cfg, sv['mix'], wl['g_mix_post'], dh1, None, MXU, f"{tag}_mixpost_bwd")
    d_mixin = matmul(d_mix, wl['w_out'], 'nt', F32, f"{tag}_out_dx")
    g['w_out'] = matmul(sv['mixin'], d_mix, 'tn', F32, f"{tag}_out_dw", tm_cap=1024, tk_cap=1408)
    WG = cfg.WG
    x_sb, x_fox = io.bwd_exchanges(l, g)
    (dq_s, dk_s, dv_s, g['g_sb']), got_sb = sb_bwd(cfg, sv['qkv'], wl['g_sb'], sv['o_sb'], d_mixin[:, :WG],
                                                   sv['rtot'], f"{tag}_sb_bwd", x_sb)
    (dq_f, dk_f, dv_f, g['g_fox'], dc_row), got_fox = fox_bwd(cfg, sv['qkv'], wl['g_fox'], sv['c_col'], sv['c_row'],
                                                              sv['o_fx'], d_mixin[:, WG:], sv['lse'],
                                                              f"{tag}_fox_bwd", x_fox)
    dc = jnp.pad(dc_row.reshape(cfg.NH, cfg.LP).T, ((0, 0), (0, LANES - cfg.NH)))
    d_f, g['b_f'] = fox_gate_bwd(cfg, dc, sv['f_logit'], wl['b_pad'], f"{tag}_gate_bwd")
    d_proj = jnp.concatenate([dq_s, dk_s, dv_s, dq_f, dk_f, dv_f, d_f], axis=1)
    g['w_in_ext'] = matmul(sv['u'], d_proj, 'tn', F32, f"{tag}_in_dw", tm_cap=1024, tn_cap=896, tk_cap=1408)
    du = matmul(d_proj, wl['w_in_ext'], 'nt', F32, f"{tag}_in_dx", tk_cap=3200)
    dh, g['g_mix_pre'] = norm_bwd(cfg, sv['h'], wl['g_mix_pre'], du, dh1, F32, f"{tag}_mixpre_bwd")
    io.bwd_done(l, g, got_sb, got_fox)
    return dh


def _local_step(cfg, h0, target_p, wf, io=None):
    io = _LocalWeights(cfg, wf) if io is None else io
    layers = []
    for l in range(cfg.DEPTH):
        layers.append(dict(
            b_pad=jnp.pad(wf['b_f'][l], (0, LANES - cfg.NH)).reshape(1, LANES),
            g_sb=wf['g_sb'][l][:, None, :], g_fox=wf['g_fox'][l][:, None, :],
            conv_w=_ffperm(wf['conv_w'][l], cfg), conv_b=_ffperm(wf['conv_b'][l], cfg).reshape(1, -1),
            g_mix_pre=_vec(wf['g_mix_pre'][l]), g_mix_post=_vec(wf['g_mix_post'][l]),
            g_ffn_pre=_vec(wf['g_ffn_pre'][l]), g_ffn_post=_vec(wf['g_ffn_post'][l]),
            g_next=_vec(wf['g_mix_pre'][(l + 1) % cfg.DEPTH])))
    h = h0
    u = pre_norm(cfg, h0, layers[0]['g_mix_pre'], "l0_prenorm")
    saved = []
    for l in range(cfg.DEPTH):
        layers[l].update(_w_in_parts(cfg, io.w_in(l)))
        h, u, sv = _layer_fwd(cfg, l, h, u, layers[l], io)
        saved.append(sv)
    dh, loss_blk = loss_head(cfg, h, target_p, "loss_head")
    for l in reversed(range(cfg.DEPTH)):
        dh = _layer_bwd(cfg, l, dh, layers[l], saved[l], io)
    return loss_blk, dh, [io.grads[l] for l in range(cfg.DEPTH)]


def _cols(g):
    return g.transpose(1, 0, 2).reshape(g.shape[1], -1)


def _rows(g):
    return g.reshape(-1, g.shape[2])


def _send_cols(g):
    r, c = g.shape
    return g.reshape(r, NCHIP, c // NCHIP).transpose(1, 0, 2).astype(MXU)


def _send_rows(g):
    r, c = g.shape
    return g.reshape(NCHIP, r // NCHIP, c).astype(MXU)


class _StreamedWeights:
    def __init__(self, cfg, w):
        self.cfg = cfg
        self.shard = {n: [w[n][l].astype(MXU) for l in range(cfg.DEPTH)] for n in BIG}
        self.grads, self.recv = {}, {n: [None] * cfg.DEPTH for n in BIG}
        self.win = {0: _cols(chip_exchange([self.shard['w_in'][0]], False, "gather_w_in0")[0])}
        self.pending = None

    def w_in(self, l):
        return self.win[l]

    def fwd_exchanges(self, l):
        s = self.shard
        fox = [s['w_down'][l]] + ([s['w_in'][l + 1]] if l + 1 < self.cfg.DEPTH else [])
        return ([s['w_out'][l], s['w_up'][l]], False), (fox, False)

    def rest(self, l, got_sb, got_fox):
        if l + 1 < self.cfg.DEPTH:
            self.win[l + 1] = _cols(got_fox[1])
        return dict(w_out=_rows(got_sb[0]), w_up=_ffperm(_cols(got_sb[1]), self.cfg), w_down=_rows(got_fox[0]))

    def bwd_exchanges(self, l, g):
        sb = [_send_cols(_ffunperm(g['w_up'], self.cfg))] + ([] if self.pending is None else [self.pending])
        return (sb, True), ([_send_rows(g['w_down']), _send_rows(g['w_out'])], True)

    def bwd_done(self, l, g, got_sb, got_fox):
        self.grads[l] = g
        self.recv['w_up'][l] = got_sb[0]
        if self.pending is not None:
            self.recv['w_in'][l + 1] = got_sb[1]
        self.recv['w_down'][l], self.recv['w_out'][l] = got_fox
        self.pending = _send_cols(g['w_in_ext'][:, :self.cfg.N_IN])

    def finish(self):
        self.recv['w_in'][0] = chip_exchange([self.pending], True, "scatter_w_in0")[0]
        return self.recv


def _step(cfg, x, w, target, m, v):
    xi, yi, ci = _coords()
    chip = 2 * xi + yi
    D, LP, L, NM = cfg.D, cfg.LP, cfg.L, cfg.NMETA
    DEP = cfg.DEPTH
    io = _StreamedWeights(cfg, w)
    wf = {}
    small_in = _pack([lax.dynamic_update_slice(jnp.zeros((NM, D), F32), w['meta'], (0, chip * (D // NCHIP))),
                      lax.dynamic_update_slice(jnp.zeros((DEP, 3, cfg.F2), F32), w['conv_w'],
                                               (0, 0, chip * (cfg.F2 // NCHIP)))])
    small_in = jnp.where(ci == 0, small_in, 0.0)
    meta_full, conv_w_full = _unpack(all_reduce_small(small_in, "gather_small"), [(NM, D), (DEP, 3, cfg.F2)])
    for n in SMALL:
        wf[n] = w[n]
    wf['conv_w'] = conv_w_full

    zpad = jnp.zeros((LP - L, D), F32)
    h0 = jnp.concatenate([meta_full, x[0], zpad], axis=0)
    target_p = jnp.concatenate([jnp.zeros((NM, D), F32), target[0], zpad], axis=0)
    loss_blk, dh0, grads = _local_step(cfg, h0, target_p, wf, io)

    def stack(key, shape):
        return jnp.stack([grads[l][key].reshape(shape) for l in range(DEP)])

    conv_g = jnp.stack([_ffunperm(grads[l]['conv'], cfg) for l in range(DEP)])
    small_g = dict(
        loss=loss_blk[0:1, 0:1], meta=dh0[:NM], g_mix_pre=stack('g_mix_pre', (D,)),
        b_f=stack('b_f', (LANES,))[:, :cfg.NH], g_sb=stack('g_sb', (cfg.NH, HD)), g_fox=stack('g_fox', (cfg.NH, HD)),
        g_mix_post=stack('g_mix_post', (D,)), g_ffn_pre=stack('g_ffn_pre', (D,)),
        conv_w=conv_g[:, 0:3], conv_b=conv_g[:, 3], g_ffn_post=stack('g_ffn_post', (D,)))
    keys = list(small_g)
    red = dict(zip(keys, _unpack(all_reduce_small(_pack([small_g[k] for k in keys]), "reduce_small"),
                                 [small_g[k].shape for k in keys])))
    loss = red['loss'].reshape(())
    red['meta'] = lax.dynamic_slice(red['meta'], (0, chip * (D // NCHIP)), (NM, D // NCHIP))
    red['conv_w'] = lax.dynamic_slice(red['conv_w'], (0, 0, chip * (cfg.F2 // NCHIP)), (DEP, 3, cfg.F2 // NCHIP))

    recv = io.finish()
    part = [sum_slots(recv[n], f"sum_{n}") for n in BIG]
    other = sibling_exchange(part, "sibling_grads")

    outs = {}
    for n, p, q in zip(BIG, part, other):
        shp = w[n].shape
        s2 = (shp[0] * shp[1], shp[2])
        res = adamw(w[n].reshape(s2), m[n].reshape(s2), v[n].reshape(s2), [p, q], f"adamw_{n}")
        outs[n] = [r.reshape(shp) for r in res]

    shapes = [w[n].shape for n in SMALL]
    res = adamw(_pack([w[n] for n in SMALL]), _pack([m[n] for n in SMALL]), _pack([v[n] for n in SMALL]),
                [_pack([red[n] for n in SMALL])], "adamw_small")
    res = [_unpack(r, shapes) for r in res]
    for i, n in enumerate(SMALL):
        outs[n] = [res[k][i] for k in range(4)]

    grad_x = dh0[NM:L][None]
    return (loss, grad_x, *[outs[n][0] for n in WEIGHTS], *[outs[n][1] for n in WEIGHTS],
            *[outs[n][2] for n in WEIGHTS], *[outs[n][3] for n in WEIGHTS])


def kernel(x, meta, g_mix_pre, w_in, b_f, g_sb, g_fox, w_out, g_mix_post, g_ffn_pre, w_up, conv_w, conv_b, w_down, g_ffn_post, loss_target, m_meta, m_g_mix_pre, m_w_in, m_b_f, m_g_sb, m_g_fox, m_w_out, m_g_mix_post, m_g_ffn_pre, m_w_up, m_conv_w, m_conv_b, m_w_down, m_g_ffn_post, v_meta, v_g_mix_pre, v_w_in, v_b_f, v_g_sb, v_g_fox, v_w_out, v_g_mix_post, v_g_ffn_pre, v_w_up, v_conv_w, v_conv_b, v_w_down, v_g_ffn_post):
    w = dict(zip(WEIGHTS, (meta, g_mix_pre, w_in, b_f, g_sb, g_fox, w_out, g_mix_post, g_ffn_pre, w_up, conv_w,
                           conv_b, w_down, g_ffn_post)))
    m = dict(zip(WEIGHTS, (m_meta, m_g_mix_pre, m_w_in, m_b_f, m_g_sb, m_g_fox, m_w_out, m_g_mix_post, m_g_ffn_pre,
                           m_w_up, m_conv_w, m_conv_b, m_w_down, m_g_ffn_post)))
    v = dict(zip(WEIGHTS, (v_meta, v_g_mix_pre, v_w_in, v_b_f, v_g_sb, v_g_fox, v_w_out, v_g_mix_post, v_g_ffn_pre,
                           v_w_up, v_conv_w, v_conv_b, v_w_down, v_g_ffn_post)))
    return _step(PROD, x, w, loss_target, m, v)
```

```python
import functools
import math
from typing import NamedTuple

import jax
import jax.numpy as jnp
from jax import lax
from jax.experimental import pallas as pl
from jax.experimental.pallas import tpu as pltpu

F32 = jnp.float32
MXU = jnp.bfloat16
HD = 128
LANES = 128
EPS = 1e-6
NEG = -1e30
LK_PIECES = 2
ADAM_LR, ADAM_B1, ADAM_B2, ADAM_EPS, ADAM_WD, ADAM_STEP = 0.001, 0.9, 0.999, 1e-08, 0.01, 10
VMEM_LIMIT = 56 * 1024 * 1024
MESH = pl.DeviceIdType.MESH
NCHIP = 4

WEIGHTS = ['meta', 'g_mix_pre', 'w_in', 'b_f', 'g_sb', 'g_fox', 'w_out', 'g_mix_post', 'g_ffn_pre',
           'w_up', 'conv_w', 'conv_b', 'w_down', 'g_ffn_post']
BIG = ['w_in', 'w_out', 'w_up', 'w_down']
SMALL = [n for n in WEIGHTS if n not in BIG]

NT = (((1,), (1,)), ((), ()))
TN = (((0,), (0,)), ((), ()))
NN = (((1,), (0,)), ((), ()))


class Cfg(NamedTuple):
    D: int
    SEQ: int
    NMETA: int
    NH: int
    F: int
    LP: int
    tq: int
    tr: int
    FB: int
    DEPTH: int = 2

    @property
    def L(self): return self.SEQ + self.NMETA
    @property
    def WG(self): return self.NH * HD
    @property
    def WMIX(self): return 2 * self.WG
    @property
    def NQKV(self): return 6 * self.WG
    @property
    def N_IN(self): return self.NQKV + self.NH
    @property
    def NEXT(self): return self.NQKV + LANES
    @property
    def F2(self): return 2 * self.F


PROD = Cfg(D=2048, SEQ=4096, NMETA=16, NH=8, F=5632, LP=4224, tq=384, tr=192, FB=512)


def _tile(n, cap, mult=LANES):
    best = None
    for t in range(mult, min(n, cap) + 1, mult):
        if n % t == 0:
            best = t
    return best if best is not None else n


def _cp(sem):
    return pltpu.CompilerParams(dimension_semantics=sem, vmem_limit_bytes=VMEM_LIMIT)


def _split_dot(x, tri, pieces):
    acc, r = None, x
    for p in range(pieces):
        xp = r.astype(MXU)
        d = jnp.dot(xp, tri, preferred_element_type=F32)
        acc = d if acc is None else acc + d
        if p + 1 < pieces:
            r = r - xp.astype(F32)
    return acc


def _split_dot_left(tri, x, pieces):
    acc, r = None, x
    for p in range(pieces):
        xp = r.astype(MXU)
        d = jnp.dot(tri, xp, preferred_element_type=F32)
        acc = d if acc is None else acc + d
        if p + 1 < pieces:
            r = r - xp.astype(F32)
    return acc


def _softplus(z):
    return jnp.maximum(z, 0.0) + jnp.log(1.0 + jnp.exp(-jnp.abs(z)))


def _shape2(x):
    return tuple(x.shape) if x.ndim == 2 else (x.shape[1], x.shape[0] * x.shape[2])


def _split_width(x):
    return x.shape[-1]


def _spec2(x, rb, cb, idx):
    if len(x.shape) == 2:
        return pl.BlockSpec((rb, cb), idx)
    per = x.shape[2] // cb

    def im(i, j, k):
        ri, ci = idx(i, j, k)
        return ci // per, ri, ci % per

    return pl.BlockSpec((None, rb, cb), im)


def matmul(a, b, mode, out_dtype, name, tm_cap=1408, tn_cap=1024, tk_cap=8192, out_split=None, xfer=None):
    (K, M) = _shape2(a) if mode == 'tn' else _shape2(a)[::-1]
    N = _shape2(b)[0] if mode == 'nt' else _shape2(b)[1]
    n_unit = math.gcd(N // (out_split or 1), _split_width(b) if mode != 'nt' else N)
    k_unit = math.gcd(_split_width(a) if mode != 'tn' else K, _split_width(b) if mode == 'nt' else K)
    tm, tn, tk = _tile(M, tm_cap, 8), _tile(n_unit, tn_cap), _tile(k_unit, tk_cap)
    nk = K // tk
    dn = {'nn': NN, 'nt': NT, 'tn': TN}[mode]

    def body(a_ref, b_ref, o_ref, *scratch):
        d = lax.dot_general(a_ref[...], b_ref[...], dn, preferred_element_type=F32)
        if nk == 1:
            o_ref[...] = d.astype(out_dtype)
        else:
            acc_ref, = scratch
            k = pl.program_id(2)

            @pl.when(k == 0)
            def _():
                acc_ref[...] = d

            @pl.when(k > 0)
            def _():
                acc_ref[...] += d

            @pl.when(k == nk - 1)
            def _():
                o_ref[...] = acc_ref[...].astype(out_dtype)

    a_spec = (_spec2(a, tk, tm, lambda i, j, k: (k, i)) if mode == 'tn'
              else _spec2(a, tm, tk, lambda i, j, k: (i, k)))
    b_spec = (_spec2(b, tn, tk, lambda i, j, k: (j, k)) if mode == 'nt'
              else _spec2(b, tk, tn, lambda i, j, k: (k, j)))
    out = (jax.ShapeDtypeStruct((M, N), out_dtype) if out_split is None
           else jax.ShapeDtypeStruct((out_split, M, N // out_split), out_dtype))
    (res,), got = call_with_exchange(
        body, name, (M // tm, N // tn, nk), [a_spec, b_spec], (_spec2(out, tm, tn, lambda i, j, k: (i, j)),),
        (out,), [] if nk == 1 else [pltpu.VMEM((tm, tn), F32)], (a, b), xfer)
    return res if xfer is None else (res, got)


def _rstd(x):
    return lax.rsqrt(jnp.mean(x * x, axis=-1, keepdims=True) + EPS)


def pre_norm(cfg, h, g, name):
    LP, D, tr = cfg.LP, cfg.D, cfg.tr

    def body(h_ref, g_ref, u_ref):
        x = h_ref[...]
        u_ref[...] = ((x * _rstd(x)) * g_ref[...]).astype(MXU)

    row = pl.BlockSpec((tr, D), lambda i: (i, 0))
    vec = pl.BlockSpec((1, D), lambda i: (0, 0))
    return pl.pallas_call(body, name=name, out_shape=jax.ShapeDtypeStruct((LP, D), MXU), grid=(LP // tr,),
                          in_specs=[row, vec], out_specs=row, compiler_params=_cp(("parallel",)))(h, g)


def resid_norm(cfg, y, h, g_post, g_next, name):
    LP, D, tr = cfg.LP, cfg.D, cfg.tr

    def body(y_ref, h_ref, gp_ref, gn_ref, hn_ref, u_ref):
        yv = y_ref[...]
        hn = h_ref[...] + (yv * _rstd(yv)) * gp_ref[...]
        hn_ref[...] = hn
        u_ref[...] = ((hn * _rstd(hn)) * gn_ref[...]).astype(MXU)

    row = pl.BlockSpec((tr, D), lambda i: (i, 0))
    vec = pl.BlockSpec((1, D), lambda i: (0, 0))
    return pl.pallas_call(
        body, name=name,
        out_shape=(jax.ShapeDtypeStruct((LP, D), F32), jax.ShapeDtypeStruct((LP, D), MXU)),
        grid=(LP // tr,), in_specs=[row, row, vec, vec], out_specs=(row, row),
        compiler_params=_cp(("parallel",)))(y, h, g_post, g_next)


def loss_head(cfg, h, target, name):
    LP, D, tr = cfg.LP, cfg.D, cfg.tr
    lo, hi = cfg.NMETA, cfg.L

    def body(h_ref, t_ref, dh_ref, loss_ref):
        i = pl.program_id(0)
        rows = lax.broadcasted_iota(jnp.int32, (tr, D), 0) + i * tr
        diff = jnp.where((rows >= lo) & (rows < hi), h_ref[...] - t_ref[...], 0.0)
        dh_ref[...] = diff * (1.0 / D)
        part = 0.5 * jnp.sum(jnp.sum(diff * diff, axis=1, keepdims=True), axis=0, keepdims=True) * (1.0 / D)

        @pl.when(i == 0)
        def _():
            loss_ref[...] = jnp.zeros_like(loss_ref)

        loss_ref[...] += jnp.broadcast_to(part, loss_ref.shape)

    row = pl.BlockSpec((tr, D), lambda i: (i, 0))
    return pl.pallas_call(
        body, name=name,
        out_shape=(jax.ShapeDtypeStruct((LP, D), F32), jax.ShapeDtypeStruct((8, LANES), F32)),
        grid=(LP // tr,), in_specs=[row, row],
        out_specs=(row, pl.BlockSpec((8, LANES), lambda i: (0, 0))),
        compiler_params=_cp(("arbitrary",)))(h, target)


def norm_bwd(cfg, x, g, dy, dres, out_dtype, name):
    LP, D, tr = cfg.LP, cfg.D, cfg.tr
    has_res = dres is not None

    def body(*refs):
        if has_res:
            x_ref, g_ref, dy_ref, dres_ref, dx_ref, dg_ref = refs
        else:
            x_ref, g_ref, dy_ref, dx_ref, dg_ref = refs
        xv, dyv = x_ref[...], dy_ref[...]
        r = _rstd(xv)
        xhat = xv * r
        gdy = dyv * g_ref[...]
        dx = r * (gdy - xhat * jnp.mean(gdy * xhat, axis=-1, keepdims=True))
        if has_res:
            dx = dx + dres_ref[...]
        dx_ref[...] = dx.astype(out_dtype)

        @pl.when(pl.program_id(0) == 0)
        def _():
            dg_ref[...] = jnp.zeros_like(dg_ref)

        dg_ref[...] += jnp.sum(dyv * xhat, axis=0, keepdims=True)

    row = pl.BlockSpec((tr, D), lambda i: (i, 0))
    vec = pl.BlockSpec((1, D), lambda i: (0, 0))
    ins = [x, g, dy] + ([dres] if has_res else [])
    return pl.pallas_call(
        body, name=name,
        out_shape=(jax.ShapeDtypeStruct((LP, D), out_dtype), jax.ShapeDtypeStruct((1, D), F32)),
        grid=(LP // tr,), in_specs=[row, vec, row] + ([row] if has_res else []), out_specs=(row, vec),
        compiler_params=_cp(("arbitrary",)))(*ins)


HALO = 8


def _conv3(ext, w, b):
    s1, s2 = pltpu.roll(ext, 1, 0), pltpu.roll(ext, 2, 0)
    return w[0:1, :] * s2 + w[1:2, :] * s1 + w[2:3, :] * ext + b, s1, s2


def conv_act_fwd(cfg, a, cw, cb, name):
    LP, F, FB, tm = cfg.LP, cfg.F, cfg.FB, cfg.tr
    nb = tm // HALO

    def body(a_ref, prev_ref, w_ref, b_ref, act_ref):
        i = pl.program_id(1)
        c = []
        for hh in range(2):
            prev = jnp.where(i > 0, prev_ref[hh], 0.0)
            ext = jnp.concatenate([prev, a_ref[hh]], axis=0)
            c.append(_conv3(ext, w_ref[hh], b_ref[hh])[0][HALO:, :])
        act_ref[...] = (c[0] * jax.nn.sigmoid(c[0]) * c[1]).astype(MXU)

    return pl.pallas_call(
        body, name=name, out_shape=jax.ShapeDtypeStruct((LP, F), MXU), grid=(F // FB, LP // tm),
        in_specs=[pl.BlockSpec((2, tm, FB), lambda j, i: (0, i, j)),
                  pl.BlockSpec((2, HALO, FB), lambda j, i: (0, jnp.maximum(i * nb - 1, 0), j)),
                  pl.BlockSpec((2, 3, FB), lambda j, i: (0, 0, j)),
                  pl.BlockSpec((2, 1, FB), lambda j, i: (0, 0, j))],
        out_specs=pl.BlockSpec((tm, FB), lambda j, i: (i, j)),
        compiler_params=_cp(("parallel", "parallel")))(a, a, cw, cb)


def conv_act_bwd(cfg, a, d_act, cw, cb, name):
    LP, F, FB, tm = cfg.LP, cfg.F, cfg.FB, cfg.tr
    nb, last = tm // HALO, LP // tm - 1
    nrow = LP // HALO
    n = tm + 2 * HALO

    def body(a_ref, aprev_ref, anext_ref, d_ref, dnext_ref, w_ref, b_ref, da_ref, dcv_ref):
        i = pl.program_id(1)

        @pl.when(i == 0)
        def _():
            dcv_ref[...] = jnp.zeros_like(dcv_ref)

        c, exts = [], []
        for hh in range(2):
            prev = jnp.where(i > 0, aprev_ref[hh], 0.0)
            ext = jnp.concatenate([prev, a_ref[hh], anext_ref[hh]], axis=0)
            chh, s1, s2 = _conv3(ext, w_ref[hh], b_ref[hh])
            c.append(chh)
            exts.append((ext, s1, s2))
        dnext = jnp.where(i < last, dnext_ref[...], 0.0)
        dact = jnp.concatenate([jnp.zeros((HALO, FB), F32), d_ref[...], dnext], axis=0)
        sg = jax.nn.sigmoid(c[0])
        d_c = [dact * c[1] * (sg * (1.0 + c[0] * (1.0 - sg))), dact * (c[0] * sg)]
        for hh in range(2):
            dc, w = d_c[hh], w_ref[hh]
            da = w[2:3, :] * dc + w[1:2, :] * pltpu.roll(dc, n - 1, 0) + w[0:1, :] * pltpu.roll(dc, n - 2, 0)
            da_ref[hh] = da[HALO:HALO + tm, :].astype(MXU)
            dcb = dc[HALO:HALO + tm, :]
            ext, s1, s2 = exts[hh]
            dcv_ref[hh, 0:1, :] += jnp.sum(dcb * s2[HALO:HALO + tm, :], axis=0, keepdims=True)
            dcv_ref[hh, 1:2, :] += jnp.sum(dcb * s1[HALO:HALO + tm, :], axis=0, keepdims=True)
            dcv_ref[hh, 2:3, :] += jnp.sum(dcb * ext[HALO:HALO + tm, :], axis=0, keepdims=True)
            dcv_ref[hh, 3:4, :] += jnp.sum(dcb, axis=0, keepdims=True)

    return pl.pallas_call(
        body, name=name,
        out_shape=(jax.ShapeDtypeStruct((2, LP, F), MXU), jax.ShapeDtypeStruct((2, 8, F), F32)),
        grid=(F // FB, LP // tm),
        in_specs=[pl.BlockSpec((2, tm, FB), lambda j, i: (0, i, j)),
                  pl.BlockSpec((2, HALO, FB), lambda j, i: (0, jnp.maximum(i * nb - 1, 0), j)),
                  pl.BlockSpec((2, HALO, FB), lambda j, i: (0, jnp.minimum((i + 1) * nb, nrow - 1), j)),
                  pl.BlockSpec((tm, FB), lambda j, i: (i, j)),
                  pl.BlockSpec((HALO, FB), lambda j, i: (jnp.minimum((i + 1) * nb, nrow - 1), j)),
                  pl.BlockSpec((2, 3, FB), lambda j, i: (0, 0, j)),
                  pl.BlockSpec((2, 1, FB), lambda j, i: (0, 0, j))],
        out_specs=(pl.BlockSpec((2, tm, FB), lambda j, i: (0, i, j)),
                   pl.BlockSpec((2, 8, FB), lambda j, i: (0, 0, j))),
        compiler_params=_cp(("parallel", "arbitrary")))(a, a, a, d_act, d_act, cw, cb)


def fox_gate_fwd(cfg, f_logit, b_pad, name):
    LP, tb = cfg.LP, cfg.tq

    def body(f_ref, b_ref, c_ref, carry_ref):
        @pl.when(pl.program_id(0) == 0)
        def _():
            carry_ref[...] = jnp.zeros_like(carry_ref)

        xv = f_ref[...] + b_ref[...]
        lf = -_softplus(-xv)
        r = lax.broadcasted_iota(jnp.int32, (tb, tb), 0)
        s = lax.broadcasted_iota(jnp.int32, (tb, tb), 1)
        c = _split_dot_left((s <= r).astype(MXU), lf, 3) + carry_ref[0:1, :]
        c_ref[...] = c
        carry_ref[0:1, :] = c[tb - 1:tb, :]

    blk = pl.BlockSpec((tb, LANES), lambda i: (i, 0))
    return pl.pallas_call(
        body, name=name, out_shape=jax.ShapeDtypeStruct((LP, LANES), F32), grid=(LP // tb,),
        in_specs=[blk, pl.BlockSpec((1, LANES), lambda i: (0, 0))], out_specs=blk,
        scratch_shapes=[pltpu.VMEM((8, LANES), F32)], compiler_params=_cp(("arbitrary",)))(f_logit, b_pad)


def fox_gate_bwd(cfg, dc, f_logit, b_pad, name):
    LP, tb = cfg.LP, cfg.tq
    nblk = LP // tb

    def body(dc_ref, f_ref, b_ref, df_ref, db_ref, carry_ref):
        @pl.when(pl.program_id(0) == 0)
        def _():
            carry_ref[...] = jnp.zeros_like(carry_ref)
            db_ref[...] = jnp.zeros_like(db_ref)

        r = lax.broadcasted_iota(jnp.int32, (tb, tb), 0)
        s = lax.broadcasted_iota(jnp.int32, (tb, tb), 1)
        dlf = _split_dot_left((s >= r).astype(MXU), dc_ref[...], 3) + carry_ref[0:1, :]
        carry_ref[0:1, :] = dlf[0:1, :]
        df = dlf * jax.nn.sigmoid(-(f_ref[...] + b_ref[...]))
        df_ref[...] = df.astype(MXU)
        db_ref[...] += jnp.sum(df, axis=0, keepdims=True)

    blk = pl.BlockSpec((tb, LANES), lambda i: (nblk - 1 - i, 0))
    vec = pl.BlockSpec((1, LANES), lambda i: (0, 0))
    return pl.pallas_call(
        body, name=name,
        out_shape=(jax.ShapeDtypeStruct((LP, LANES), MXU), jax.ShapeDtypeStruct((1, LANES), F32)),
        grid=(nblk,), in_specs=[blk, blk, vec], out_specs=(blk, vec),
        scratch_shapes=[pltpu.VMEM((8, LANES), F32)], compiler_params=_cp(("arbitrary",)))(dc, f_logit, b_pad)


def _head_norm_fwd(o, g):
    return (o * lax.rsqrt(jnp.mean(o * o, axis=-1, keepdims=True) + EPS)) * g


def _head_norm_bwd(o, g, d_on):
    r = lax.rsqrt(jnp.mean(o * o, axis=-1, keepdims=True) + EPS)
    ohat = o * r
    gdy = d_on * g
    d_o = r * (gdy - ohat * jnp.mean(gdy * ohat, axis=-1, keepdims=True))
    return d_o, jnp.sum(d_on * ohat, axis=0, keepdims=True)


def _diag_masks(tq, strict):
    rows = lax.broadcasted_iota(jnp.int32, (tq, HD), 0)
    cols = lax.broadcasted_iota(jnp.int32, (tq, HD), 1)
    return [(cols + kk * HD < rows) if strict else (cols + kk * HD <= rows) for kk in range(tq // HD)]


def _tri(pred):
    a = lax.broadcasted_iota(jnp.int32, (HD, HD), 0)
    b = lax.broadcasted_iota(jnp.int32, (HD, HD), 1)
    return pred(a, b).astype(MXU)


def _attn_specs(cfg, group):
    base = 3 * cfg.NH * group
    return base, base + cfg.NH, base + 2 * cfg.NH


def sb_fwd(cfg, qkv, g3, name, xfer=None):
    LP, NH, tq = cfg.LP, cfg.NH, cfg.tq
    nq, R = LP // tq, tq // HD
    scale = HD ** -0.5
    cq, ck, cv = _attn_specs(cfg, 0)

    def body(q_ref, k_ref, v_ref, g_ref, opre_ref, on_ref, rtot_ref):
        i = pl.program_id(1)
        q = q_ref[...]
        masks = _diag_masks(tq, True)
        m_after = _tri(lambda a, b: a > b)

        def sup(J, carry, masked):
            acc, rc = carry
            order = range(R - 1, -1, -1)
            offs = [pl.multiple_of((J * R + kk) * HD, HD) for kk in range(R)]
            zs = [lax.dot_general(q, k_ref[pl.ds(offs[kk], HD), :], NT, preferred_element_type=F32) * scale
                  for kk in range(R)]
            ls, lks, tris = [None] * R, [None] * R, [None] * R
            for kk in order:
                sp = _softplus(zs[kk])
                lks[kk] = jnp.where(masks[kk], -sp, 0.0) if masked else -sp
                tris[kk] = _split_dot(lks[kk], m_after, LK_PIECES)
                ls[kk] = zs[kk] - sp
            for kk in order:
                a = jnp.exp(ls[kk] + (tris[kk] + rc))
                if masked:
                    a = jnp.where(masks[kk], a, 0.0)
                acc = acc + jnp.dot(a.astype(MXU), v_ref[pl.ds(offs[kk], HD), :], preferred_element_type=F32)
                rc = rc + jnp.sum(lks[kk], axis=1, keepdims=True)
            return acc, rc

        carry = sup(i, (jnp.zeros((tq, HD), F32), jnp.zeros((tq, 1), F32)), True)
        acc, rc = lax.fori_loop(0, i, lambda it, c: sup(i - 1 - it, c, False), carry)
        opre_ref[...] = acc
        on_ref[...] = _head_norm_fwd(acc, g_ref[0]).astype(MXU)
        rtot_ref[0] = rc

    return call_with_exchange(
        body, name, (NH, nq),
        [pl.BlockSpec((tq, HD), lambda h, i: (i, cq + h)),
         pl.BlockSpec((LP, HD), lambda h, i: (0, ck + h)),
         pl.BlockSpec((LP, HD), lambda h, i: (0, cv + h)),
         pl.BlockSpec((1, 1, HD), lambda h, i: (h, 0, 0))],
        (pl.BlockSpec((tq, HD), lambda h, i: (i, h)),
         pl.BlockSpec((tq, HD), lambda h, i: (i, h)),
         pl.BlockSpec((1, tq, 1), lambda h, i: (h, i, 0))),
        (jax.ShapeDtypeStruct((LP, cfg.WG), F32), jax.ShapeDtypeStruct((LP, cfg.WG), MXU),
         jax.ShapeDtypeStruct((NH, LP, 1), F32)),
        [], (qkv, qkv, qkv, g3), xfer)


def sb_bwd(cfg, qkv, g3, o_pre, d_on, rtot, name, xfer=None):
    LP, NH, tq = cfg.LP, cfg.NH, cfg.tq
    nq, R = LP // tq, tq // HD
    scale = HD ** -0.5
    cq, ck, cv = _attn_specs(cfg, 0)

    def body(q_ref, k_ref, v_ref, g_ref, o_ref, don_ref, rtot_ref, dq_ref, dk_ref, dv_ref, dg_ref,
             dk_acc, dv_acc):
        i = pl.program_id(1)

        @pl.when(i == 0)
        def _():
            dk_acc[...] = jnp.zeros_like(dk_acc)
            dv_acc[...] = jnp.zeros_like(dv_acc)
            dg_ref[...] = jnp.zeros_like(dg_ref)

        q = q_ref[...]
        d_o, dg = _head_norm_bwd(o_ref[...], g_ref[0], don_ref[...])
        dg_ref[0] += dg
        do_b = d_o.astype(MXU)
        rt = rtot_ref[0]
        masks = _diag_masks(tq, True)
        m_le = _tri(lambda a, b: a <= b)
        m_lt = _tri(lambda a, b: a < b)

        def sup(J, carry, masked):
            dq, lc, pc = carry
            offs = [pl.multiple_of((J * R + kk) * HD, HD) for kk in range(R)]
            zs = [lax.dot_general(q, k_ref[pl.ds(offs[kk], HD), :], NT, preferred_element_type=F32) * scale
                  for kk in range(R)]
            das = [lax.dot_general(do_b, v_ref[pl.ds(offs[kk], HD), :], NT, preferred_element_type=F32)
                   for kk in range(R)]
            ls, lks, tri1 = [], [], []
            for kk in range(R):
                sp = _softplus(zs[kk])
                lk = jnp.where(masks[kk], -sp, 0.0) if masked else -sp
                tri1.append(_split_dot(lk, m_le, LK_PIECES))
                lks.append(lk)
                ls.append(zs[kk] - sp)
            ggs, a_bs, tri2 = [], [], []
            for kk in range(R):
                a = jnp.exp(ls[kk] + ((rt - lc) - tri1[kk]))
                if masked:
                    a = jnp.where(masks[kk], a, 0.0)
                gg = a * das[kk]
                tri2.append(_split_dot(gg, m_lt, 2))
                ggs.append(gg)
                a_bs.append(a.astype(MXU))
                lc = lc + jnp.sum(lks[kk], axis=1, keepdims=True)
            for kk in range(R):
                sig = jnp.exp(ls[kk])
                dz = (ggs[kk] * (1.0 - sig) - sig * (pc + tri2[kk])) * scale
                if masked:
                    dz = jnp.where(masks[kk], dz, 0.0)
                dz_b = dz.astype(MXU)
                dq = dq + jnp.dot(dz_b, k_ref[pl.ds(offs[kk], HD), :], preferred_element_type=F32)
                dk_acc[pl.ds(offs[kk], HD), :] += lax.dot_general(dz_b, q, TN, preferred_element_type=F32)
                dv_acc[pl.ds(offs[kk], HD), :] += lax.dot_general(a_bs[kk], do_b, TN, preferred_element_type=F32)
                pc = pc + jnp.sum(ggs[kk], axis=1, keepdims=True)
            return dq, lc, pc

        zc = jnp.zeros((tq, 1), F32)
        carry = lax.fori_loop(0, i, lambda J, c: sup(J, c, False), (jnp.zeros((tq, HD), F32), zc, zc))
        dq, _, _ = sup(i, carry, True)
        dq_ref[...] = dq.astype(MXU)

        @pl.when(i == nq - 1)
        def _():
            dk_ref[...] = dk_acc[...].astype(MXU)
            dv_ref[...] = dv_acc[...].astype(MXU)

    blk = pl.BlockSpec((tq, HD), lambda h, i: (i, h))
    full = pl.BlockSpec((LP, HD), lambda h, i: (0, h))
    gspec = pl.BlockSpec((1, 1, HD), lambda h, i: (h, 0, 0))
    return call_with_exchange(
        body, name, (NH, nq),
        [pl.BlockSpec((tq, HD), lambda h, i: (i, cq + h)),
         pl.BlockSpec((LP, HD), lambda h, i: (0, ck + h)),
         pl.BlockSpec((LP, HD), lambda h, i: (0, cv + h)),
         gspec, blk, blk, pl.BlockSpec((1, tq, 1), lambda h, i: (h, i, 0))],
        (blk, full, full, gspec),
        (jax.ShapeDtypeStruct((LP, cfg.WG), MXU),) * 3 + (jax.ShapeDtypeStruct((NH, 1, HD), F32),),
        [pltpu.VMEM((LP, HD), F32), pltpu.VMEM((LP, HD), F32)],
        (qkv, qkv, qkv, g3, o_pre, d_on, rtot), xfer)


def fox_fwd(cfg, qkv, g3, c_col, c_row, name, xfer=None):
    LP, NH, tq = cfg.LP, cfg.NH, cfg.tq
    nq, R = LP // tq, tq // HD
    scale = HD ** -0.5
    cq, ck, cv = _attn_specs(cfg, 1)

    def body(q_ref, k_ref, v_ref, g_ref, ccol_ref, crow_ref, opre_ref, on_ref, lse_ref):
        i = pl.program_id(1)
        q = q_ref[...]
        cqv = ccol_ref[0]
        masks = _diag_masks(tq, False)

        def sup(J, carry, masked):
            acc, m, l = carry
            ss, offs = [], []
            for kk in range(R):
                j = J * R + kk
                off = pl.multiple_of(j * HD, HD)
                s = (lax.dot_general(q, k_ref[pl.ds(off, HD), :], NT, preferred_element_type=F32) * scale
                     + (cqv - crow_ref[0, pl.ds(j, 1), :]))
                ss.append(jnp.where(masks[kk], s, NEG) if masked else s)
                offs.append(off)
            mx = jnp.max(ss[0], axis=1, keepdims=True)
            for s in ss[1:]:
                mx = jnp.maximum(mx, jnp.max(s, axis=1, keepdims=True))
            m_new = jnp.maximum(m, mx)
            alpha = jnp.exp(m - m_new)
            acc, l = alpha * acc, alpha * l
            for s, off in zip(ss, offs):
                p = jnp.exp(s - m_new)
                l = l + jnp.sum(p, axis=1, keepdims=True)
                acc = acc + _split_dot(p, v_ref[pl.ds(off, HD), :], 2)
            return acc, m_new, l

        carry = (jnp.zeros((tq, HD), F32), jnp.full((tq, 1), NEG, F32), jnp.zeros((tq, 1), F32))
        carry = lax.fori_loop(0, i, lambda J, c: sup(J, c, False), carry)
        acc, m, l = sup(i, carry, True)
        o = acc / l
        opre_ref[...] = o
        on_ref[...] = _head_norm_fwd(o, g_ref[0]).astype(MXU)
        lse_ref[0] = m + jnp.log(l)

    return call_with_exchange(
        body, name, (NH, nq),
        [pl.BlockSpec((tq, HD), lambda h, i: (i, cq + h)),
         pl.BlockSpec((LP, HD), lambda h, i: (0, ck + h)),
         pl.BlockSpec((LP, HD), lambda h, i: (0, cv + h)),
         pl.BlockSpec((1, 1, HD), lambda h, i: (h, 0, 0)),
         pl.BlockSpec((1, tq, 1), lambda h, i: (h, i, 0)),
         pl.BlockSpec((1, LP // HD, HD), lambda h, i: (h, 0, 0))],
        (pl.BlockSpec((tq, HD), lambda h, i: (i, h)),
         pl.BlockSpec((tq, HD), lambda h, i: (i, h)),
         pl.BlockSpec((1, tq, 1), lambda h, i: (h, i, 0))),
        (jax.ShapeDtypeStruct((LP, cfg.WG), F32), jax.ShapeDtypeStruct((LP, cfg.WG), MXU),
         jax.ShapeDtypeStruct((NH, LP, 1), F32)),
        [], (qkv, qkv, qkv, g3, c_col, c_row), xfer)


def fox_bwd(cfg, qkv, g3, c_col, c_row, o_pre, d_on, lse, name, xfer=None):
    LP, NH, tq = cfg.LP, cfg.NH, cfg.tq
    nq, R = LP // tq, tq // HD
    scale = HD ** -0.5
    cq, ck, cv = _attn_specs(cfg, 1)

    def body(q_ref, k_ref, v_ref, g_ref, ccol_ref, crow_ref, o_ref, don_ref, lse_ref,
             dq_ref, dk_ref, dv_ref, dg_ref, dc_ref, dk_acc, dv_acc):
        i = pl.program_id(1)

        @pl.when(i == 0)
        def _():
            dk_acc[...] = jnp.zeros_like(dk_acc)
            dv_acc[...] = jnp.zeros_like(dv_acc)
            dg_ref[...] = jnp.zeros_like(dg_ref)
            dc_ref[...] = jnp.zeros_like(dc_ref)

        q = q_ref[...]
        ov = o_ref[...]
        d_o, dg = _head_norm_bwd(ov, g_ref[0], don_ref[...])
        dg_ref[0] += dg
        do_b = d_o.astype(MXU)
        delta = jnp.sum(do_b.astype(F32) * ov, axis=1, keepdims=True)
        cqv, lsev = ccol_ref[0], lse_ref[0]
        cl = cqv - lsev
        masks = _diag_masks(tq, False)

        def sup(J, dq, masked):
            offs = [pl.multiple_of((J * R + kk) * HD, HD) for kk in range(R)]
            ss = [lax.dot_general(q, k_ref[pl.ds(offs[kk], HD), :], NT, preferred_element_type=F32) * scale
                  + (cl - crow_ref[0, pl.ds(J * R + kk, 1), :]) for kk in range(R)]
            dps = [lax.dot_general(do_b, v_ref[pl.ds(offs[kk], HD), :], NT, preferred_element_type=F32)
                   for kk in range(R)]
            for kk in range(R):
                p = jnp.exp(ss[kk])
                if masked:
                    p = jnp.where(masks[kk], p, 0.0)
                ds = p * (dps[kk] - delta)
                dc_ref[0, pl.ds(J * R + kk, 1), :] -= jnp.sum(ds, axis=0, keepdims=True)
                ds_b = (ds * scale).astype(MXU)
                dk_acc[pl.ds(offs[kk], HD), :] += lax.dot_general(ds_b, q, TN, preferred_element_type=F32)
                dv_acc[pl.ds(offs[kk], HD), :] += lax.dot_general(p.astype(MXU), do_b, TN,
                                                                  preferred_element_type=F32)
                dq = dq + jnp.dot(ds_b, k_ref[pl.ds(offs[kk], HD), :], preferred_element_type=F32)
            return dq

        dq = lax.fori_loop(0, i, lambda J, c: sup(J, c, False), jnp.zeros((tq, HD), F32))
        dq = sup(i, dq, True)
        dq_ref[...] = dq.astype(MXU)

        @pl.when(i == nq - 1)
        def _():
            dk_ref[...] = dk_acc[...].astype(MXU)
            dv_ref[...] = dv_acc[...].astype(MXU)

    blk = pl.BlockSpec((tq, HD), lambda h, i: (i, h))
    full = pl.BlockSpec((LP, HD), lambda h, i: (0, h))
    gspec = pl.BlockSpec((1, 1, HD), lambda h, i: (h, 0, 0))
    col = pl.BlockSpec((1, tq, 1), lambda h, i: (h, i, 0))
    rowv = pl.BlockSpec((1, LP // HD, HD), lambda h, i: (h, 0, 0))
    return call_with_exchange(
        body, name, (NH, nq),
        [pl.BlockSpec((tq, HD), lambda h, i: (i, cq + h)),
         pl.BlockSpec((LP, HD), lambda h, i: (0, ck + h)),
         pl.BlockSpec((LP, HD), lambda h, i: (0, cv + h)),
         gspec, col, rowv, blk, blk, col],
        (blk, full, full, gspec, rowv),
        (jax.ShapeDtypeStruct((LP, cfg.WG), MXU),) * 3
        + (jax.ShapeDtypeStruct((NH, 1, HD), F32), jax.ShapeDtypeStruct((NH, LP // HD, HD), F32)),
        [pltpu.VMEM((LP, HD), F32), pltpu.VMEM((LP, HD), F32)],
        (qkv, qkv, qkv, g3, c_col, c_row, o_pre, d_on, lse), xfer)


def _row_tile(rows, cols, n_arrays):
    budget = 24 * 1024 * 1024 // (2 * n_arrays * cols * 4)
    cap = max(8, min(rows, budget // 8 * 8))
    div = _tile(rows, cap, 8)
    return div if div <= cap and div * 4 >= cap else cap


def sum_slots(recvs, name):
    S, rows, cols = recvs[0].shape
    nl = len(recvs)
    tr = _tile(rows, _row_tile(rows, cols, nl * S + 1), 8)
    nb = rows // tr

    def body(*refs):
        o_ref = refs[nl]
        layer = pl.program_id(0)
        for ll in range(nl):
            @pl.when(layer == ll)
            def _(r_ref=refs[ll]):
                acc = r_ref[0].astype(F32)
                for s in range(1, S):
                    acc = acc + r_ref[s].astype(F32)
                o_ref[...] = acc

    def spec(ll):
        return pl.BlockSpec((S, tr, cols), lambda l, i: (0, jnp.where(l == ll, i, jnp.where(l < ll, 0, nb - 1)), 0))

    return pl.pallas_call(
        body, name=name, out_shape=jax.ShapeDtypeStruct((nl * rows, cols), F32), grid=(nl, nb),
        in_specs=[spec(ll) for ll in range(nl)],
        out_specs=pl.BlockSpec((tr, cols), lambda l, i: (l * nb + i, 0)),
        compiler_params=_cp(("arbitrary", "arbitrary")))(*recvs)


def adamw(w, m, v, g_parts, name):
    rows, cols = w.shape
    npart = len(g_parts)
    tr = _row_tile(rows, cols, 7 + npart)
    c1 = 1.0 - ADAM_B1 ** ADAM_STEP
    c2 = 1.0 - ADAM_B2 ** ADAM_STEP

    def body(*refs):
        w_ref, m_ref, v_ref = refs[:3]
        g_refs = refs[3:3 + npart]
        g_out, d_out, m_out, v_out = refs[3 + npart:]
        g = g_refs[0][...]
        for r in g_refs[1:]:
            g = g + r[...]
        g_out[...] = g
        mn = ADAM_B1 * m_ref[...] + (1.0 - ADAM_B1) * g
        vn = ADAM_B2 * v_ref[...] + (1.0 - ADAM_B2) * (g * g)
        m_out[...] = mn
        v_out[...] = vn
        d_out[...] = -ADAM_LR * ((mn / c1) / (jnp.sqrt(vn / c2) + ADAM_EPS) + ADAM_WD * w_ref[...])

    blk = pl.BlockSpec((tr, cols), lambda i: (i, 0))
    return pl.pallas_call(
        body, name=name, out_shape=(jax.ShapeDtypeStruct((rows, cols), F32),) * 4,
        grid=(pl.cdiv(rows, tr),), in_specs=[blk] * (3 + npart), out_specs=(blk,) * 4,
        compiler_params=_cp(("parallel",)))(w, m, v, *g_parts)


def _coords():
    return lax.axis_index("x"), lax.axis_index("y"), lax.axis_index("c")


def chip_exchange(arrs, scatter, name):
    n = len(arrs)

    def body(*refs):
        ins, outs, sems = refs[:n], refs[n:2 * n], refs[2 * n:]
        _xfer_start(ins, outs, sems, scatter)
        _xfer_wait(ins, outs, sems, scatter)

    out_shape, scratch = _xfer_shapes(arrs, scatter)
    any_spec = pl.BlockSpec(memory_space=pl.ANY)
    return pl.pallas_call(body, name=name, out_shape=out_shape, in_specs=[any_spec] * n, out_specs=(any_spec,) * n,
                          scratch_shapes=scratch)(*arrs)


def _xfer_shapes(arrs, scatter):
    n = len(arrs)
    shapes = [a.shape[1:] if scatter else a.shape for a in arrs]
    out_shape = tuple(jax.ShapeDtypeStruct((NCHIP,) + tuple(s), a.dtype) for s, a in zip(shapes, arrs))
    scratch = [pltpu.SemaphoreType.DMA((n, 3)), pltpu.SemaphoreType.DMA((n, 3)), pltpu.SemaphoreType.DMA((n,))]
    return out_shape, scratch


def _xfer_copies(ins, outs, sems, scatter):
    send_sems, recv_sems, local_sems = sems
    n = len(ins)
    x, y, c = _coords()
    me = 2 * x + y
    peers = [(1 - x, y), (x, 1 - y), (1 - x, 1 - y)]

    def src(a, to):
        return ins[a].at[to] if scatter else ins[a]

    local = [pltpu.make_async_copy(src(a, me), outs[a].at[me], local_sems.at[a]) for a in range(n)]
    sends, recvs = [], []
    for a in range(n):
        for k, (px, py) in enumerate(peers):
            sends.append(pltpu.make_async_remote_copy(
                src_ref=src(a, 2 * px + py), dst_ref=outs[a].at[me], send_sem=send_sems.at[a, k],
                recv_sem=recv_sems.at[a, k], device_id=(px, py, c), device_id_type=MESH))
            recvs.append(pltpu.make_async_remote_copy(
                src_ref=src(a, me), dst_ref=outs[a].at[2 * px + py], send_sem=send_sems.at[a, k],
                recv_sem=recv_sems.at[a, k], device_id=(px, py, c), device_id_type=MESH))
    return local, sends, recvs


def _xfer_start(ins, outs, sems, scatter):
    local, sends, _ = _xfer_copies(ins, outs, sems, scatter)
    for cp in local + sends:
        cp.start()


def _xfer_wait(ins, outs, sems, scatter):
    local, sends, recvs = _xfer_copies(ins, outs, sems, scatter)
    for cp in recvs:
        cp.wait_recv()
    for cp in sends:
        cp.wait_send()
    for cp in local:
        cp.wait()


def call_with_exchange(core, name, grid, in_specs, out_specs, out_shape, scratch, args, xfer):
    n_in, n_out, n_scr = len(in_specs), len(out_specs), len(scratch)
    if xfer is None:
        res = pl.pallas_call(core, name=name, out_shape=out_shape, grid=grid, in_specs=in_specs,
                             out_specs=out_specs, scratch_shapes=scratch,
                             compiler_params=_cp(("arbitrary",) * len(grid)))(*args)
        return res, ()
    arrs, scatter = xfer
    nx = len(arrs)
    x_shape, x_scratch = _xfer_shapes(arrs, scatter)

    def body(*refs):
        a, xi = refs[:n_in], refs[n_in:n_in + nx]
        o, xo = refs[n_in + nx:n_in + nx + n_out], refs[n_in + nx + n_out:n_in + 2 * nx + n_out]
        rest = refs[n_in + 2 * nx + n_out:]
        scr, sems = rest[:n_scr], rest[n_scr:]
        first, last = None, None
        for d, g in enumerate(grid):
            f, l = pl.program_id(d) == 0, pl.program_id(d) == g - 1
            first = f if first is None else first & f
            last = l if last is None else last & l

        @pl.when(first)
        def _():
            _xfer_start(xi, xo, sems, scatter)

        core(*a, *o, *scr)

        @pl.when(last)
        def _():
            _xfer_wait(xi, xo, sems, scatter)

    any_spec = pl.BlockSpec(memory_space=pl.ANY)
    res = pl.pallas_call(
        body, name=name, out_shape=tuple(out_shape) + tuple(x_shape), grid=grid,
        in_specs=list(in_specs) + [any_spec] * nx, out_specs=tuple(out_specs) + (any_spec,) * nx,
        scratch_shapes=list(scratch) + x_scratch,
        compiler_params=_cp(("arbitrary",) * len(grid)))(*args, *arrs)
    return res[:n_out], res[n_out:]


def sibling_exchange(arrs, name):
    n = len(arrs)

    def body(*refs):
        ins, outs = refs[:n], refs[n:2 * n]
        send_sems, recv_sems = refs[2 * n:]
        x, y, c = _coords()
        cps = [pltpu.make_async_remote_copy(src_ref=ins[a], dst_ref=outs[a], send_sem=send_sems.at[a],
                                            recv_sem=recv_sems.at[a], device_id=(x, y, 1 - c),
                                            device_id_type=MESH) for a in range(n)]
        for cp in cps:
            cp.start()
        for cp in cps:
            cp.wait()

    any_spec = pl.BlockSpec(memory_space=pl.ANY)
    return pl.pallas_call(
        body, name=name, out_shape=tuple(jax.ShapeDtypeStruct(a.shape, a.dtype) for a in arrs),
        in_specs=[any_spec] * n, out_specs=(any_spec,) * n,
        scratch_shapes=[pltpu.SemaphoreType.DMA((n,)), pltpu.SemaphoreType.DMA((n,))],
    )(*arrs)


def all_reduce_small(pack, name):
    R = pack.shape[0]

    def body(p_ref, o_ref, slots, send_sems, recv_sems):
        x, y, c = _coords()
        me = 4 * x + 2 * y + c
        slots[me] = p_ref[...]
        cps = []
        for k in range(1, 8):
            kx, ky, kc = (k >> 2) & 1, (k >> 1) & 1, k & 1
            peer = ((1 - x) if kx else x, (1 - y) if ky else y, (1 - c) if kc else c)
            cp = pltpu.make_async_remote_copy(src_ref=p_ref, dst_ref=slots.at[me], send_sem=send_sems.at[k],
                                              recv_sem=recv_sems.at[k], device_id=peer, device_id_type=MESH)
            cp.start()
            cps.append((cp, peer))
        for k, (cp, peer) in enumerate(cps, start=1):
            frm = 4 * peer[0] + 2 * peer[1] + peer[2]
            pltpu.make_async_remote_copy(src_ref=p_ref, dst_ref=slots.at[frm], send_sem=send_sems.at[k],
                                         recv_sem=recv_sems.at[k], device_id=peer, device_id_type=MESH).wait_recv()
        for cp, _ in cps:
            cp.wait_send()
        acc = slots[0]
        for s in range(1, 8):
            acc = acc + slots[s]
        o_ref[...] = acc

    vm = pl.BlockSpec(memory_space=pltpu.VMEM)
    return pl.pallas_call(
        body, name=name, out_shape=jax.ShapeDtypeStruct((R, LANES), F32), in_specs=[vm], out_specs=vm,
        scratch_shapes=[pltpu.VMEM((8, R, LANES), F32), pltpu.SemaphoreType.DMA((8,)),
                        pltpu.SemaphoreType.DMA((8,))],
        compiler_params=pltpu.CompilerParams(vmem_limit_bytes=VMEM_LIMIT),
    )(pack)


def _gate_up(w):
    r, c = w.shape
    return w.reshape(r, 2, c // 2).transpose(1, 0, 2)


def _gate_up_inv(w):
    return w.transpose(1, 0, 2).reshape(w.shape[1], -1)


def _pack(arrs):
    parts = []
    for a in arrs:
        f = a.reshape(-1).astype(F32)
        parts.append(jnp.pad(f, (0, -f.shape[0] % 1024)).reshape(-1, LANES))
    return jnp.concatenate(parts, axis=0)


def _unpack(p, shapes):
    out, r = [], 0
    for s in shapes:
        n = math.prod(s)
        nr = (n + 1023) // 1024 * 8
        out.append(p[r:r + nr].reshape(-1)[:n].reshape(s))
        r += nr
    return out


def _vec(g):
    return g.reshape(1, -1)


class _LocalWeights:
    def __init__(self, cfg, wf):
        self.cfg, self.wf, self.grads = cfg, wf, {}

    def w_in(self, l):
        return self.wf['w_in'][l]

    def fwd_exchanges(self, l):
        return None, None

    def rest(self, l, got_sb, got_fox):
        w_up = self.wf['w_up'][l]
        return dict(w_out=self.wf['w_out'][l], w_down=self.wf['w_down'][l],
                    w_up=w_up.reshape(w_up.shape[0], NCHIP, -1).transpose(1, 0, 2))

    def bwd_exchanges(self, l, g):
        return None, None

    def in_dx_exchange(self, l, g):
        return None

    def bwd_done(self, l, g, got_sb, got_fox, got_in):
        self.grads[l] = g


def _w_in_parts(cfg, w_in):
    w_f = jnp.pad(w_in[:, cfg.NQKV:], ((0, 0), (0, LANES - cfg.NH)))
    return dict(w_qkv=w_in[:, :cfg.NQKV], w_f=w_f, w_in_ext=jnp.concatenate([w_in[:, :cfg.NQKV], w_f], axis=1))


def _layer_fwd(cfg, l, h, u, wl, io):
    tag = f"l{l}"
    qkv = matmul(u, wl['w_qkv'], 'nn', MXU, f"{tag}_qkv", tn_cap=1024)
    f_logit = matmul(u, wl['w_f'], 'nn', F32, f"{tag}_fproj")
    cpre = fox_gate_fwd(cfg, f_logit, wl['b_pad'], f"{tag}_gate_fwd")
    c_heads = cpre[:, :cfg.NH].T
    c_col = c_heads[:, :, None]
    c_row = c_heads.reshape(cfg.NH, cfg.LP // HD, HD)
    x_sb, x_fox = io.fwd_exchanges(l)
    (o_sb, on_sb, rtot), got_sb = sb_fwd(cfg, qkv, wl['g_sb'], f"{tag}_sb_fwd", x_sb)
    (o_fx, on_fx, lse), got_fox = fox_fwd(cfg, qkv, wl['g_fox'], c_col, c_row, f"{tag}_fox_fwd", x_fox)
    wl.update(io.rest(l, got_sb, got_fox))
    mixin = jnp.concatenate([on_sb, on_fx], axis=1)
    mix = matmul(mixin, wl['w_out'], 'nn', F32, f"{tag}_out")
    h1, u2 = resid_norm(cfg, mix, h, wl['g_mix_post'], wl['g_ffn_pre'], f"{tag}_mixres")
    a = matmul(u2, wl['w_up'], 'nn', F32, f"{tag}_up", tn_cap=1408, out_split=2)
    act = conv_act_fwd(cfg, a, wl['conv_w'], wl['conv_b'], f"{tag}_conv_fwd")
    ff = matmul(act, wl['w_down'], 'nn', F32, f"{tag}_down", tm_cap=704, tk_cap=2816)
    h2, u_next = resid_norm(cfg, ff, h1, wl['g_ffn_post'], wl['g_next'], f"{tag}_ffnres")
    saved = dict(h=h, u=u, qkv=qkv, f_logit=f_logit, c_col=c_col, c_row=c_row, o_sb=o_sb, o_fx=o_fx, rtot=rtot,
                 lse=lse, mixin=mixin, mix=mix, h1=h1, u2=u2, a=a, act=act, ff=ff)
    return h2, u_next, saved


def _layer_bwd(cfg, l, dh2, wl, sv, io):
    tag = f"l{l}"
    g = {}
    d_ff, g['g_ffn_post'] = norm_bwd(cfg, sv['ff'], wl['g_ffn_post'], dh2, None, MXU, f"{tag}_ffnpost_bwd")
    d_act = matmul(d_ff, wl['w_down'], 'nt', F32, f"{tag}_down_dx")
    g['w_down'] = matmul(sv['act'], d_ff, 'tn', MXU, f"{tag}_down_dw", tm_cap=1408, tk_cap=1408)
    d_a, d_conv = conv_act_bwd(cfg, sv['a'], d_act, wl['conv_w'], wl['conv_b'], f"{tag}_conv_bwd")
    g['conv'] = d_conv
    g['w_up'] = matmul(sv['u2'], d_a, 'tn', MXU, f"{tag}_up_dw", tm_cap=1024, tn_cap=1408, tk_cap=1408,
                       out_split=NCHIP)
    du2 = matmul(d_a, wl['w_up'], 'nt', F32, f"{tag}_up_dx", tm_cap=704, tk_cap=2816)
    dh1, g['g_ffn_pre'] = norm_bwd(cfg, sv['h1'], wl['g_ffn_pre'], du2, dh2, F32, f"{tag}_ffnpre_bwd")
    d_mix, g['g_mix_post'] = norm_bwd(cfg, sv['mix'], wl['g_mix_post'], dh1, None, MXU, f"{tag}_mixpost_bwd")
    d_mixin = matmul(d_mix, wl['w_out'], 'nt', F32, f"{tag}_out_dx")
    g['w_out'] = matmul(sv['mixin'], d_mix, 'tn', MXU, f"{tag}_out_dw", tm_cap=1024, tk_cap=1408)
    WG = cfg.WG
    x_sb, x_fox = io.bwd_exchanges(l, g)
    (dq_s, dk_s, dv_s, g['g_sb']), got_sb = sb_bwd(cfg, sv['qkv'], wl['g_sb'], sv['o_sb'], d_mixin[:, :WG],
                                                   sv['rtot'], f"{tag}_sb_bwd", x_sb)
    (dq_f, dk_f, dv_f, g['g_fox'], dc_row), got_fox = fox_bwd(cfg, sv['qkv'], wl['g_fox'], sv['c_col'], sv['c_row'],
                                                              sv['o_fx'], d_mixin[:, WG:], sv['lse'],
                                                              f"{tag}_fox_bwd", x_fox)
    dc = jnp.pad(dc_row.reshape(cfg.NH, cfg.LP).T, ((0, 0), (0, LANES - cfg.NH)))
    d_f, g['b_f'] = fox_gate_bwd(cfg, dc, sv['f_logit'], wl['b_pad'], f"{tag}_gate_bwd")
    d_proj = jnp.concatenate([dq_s, dk_s, dv_s, dq_f, dk_f, dv_f, d_f], axis=1)
    g['w_in_ext'] = matmul(sv['u'], d_proj, 'tn', MXU, f"{tag}_in_dw", tm_cap=1024, tn_cap=896, tk_cap=1408)
    x_in = io.in_dx_exchange(l, g)
    du = matmul(d_proj, wl['w_in_ext'], 'nt', F32, f"{tag}_in_dx", tk_cap=3200, xfer=x_in)
    du, got_in = du if x_in is not None else (du, None)
    dh, g['g_mix_pre'] = norm_bwd(cfg, sv['h'], wl['g_mix_pre'], du, dh1, F32, f"{tag}_mixpre_bwd")
    io.bwd_done(l, g, got_sb, got_fox, got_in)
    return dh


def _local_step(cfg, h0, target_p, wf, io=None):
    io = _LocalWeights(cfg, wf) if io is None else io
    layers = []
    for l in range(cfg.DEPTH):
        layers.append(dict(
            b_pad=jnp.pad(wf['b_f'][l], (0, LANES - cfg.NH)).reshape(1, LANES),
            g_sb=wf['g_sb'][l][:, None, :], g_fox=wf['g_fox'][l][:, None, :],
            conv_w=_gate_up(wf['conv_w'][l]), conv_b=wf['conv_b'][l].reshape(2, 1, cfg.F),
            g_mix_pre=_vec(wf['g_mix_pre'][l]), g_mix_post=_vec(wf['g_mix_post'][l]),
            g_ffn_pre=_vec(wf['g_ffn_pre'][l]), g_ffn_post=_vec(wf['g_ffn_post'][l]),
            g_next=_vec(wf['g_mix_pre'][(l + 1) % cfg.DEPTH])))
    h = h0
    u = pre_norm(cfg, h0, layers[0]['g_mix_pre'], "l0_prenorm")
    saved = []
    for l in range(cfg.DEPTH):
        layers[l].update(_w_in_parts(cfg, io.w_in(l)))
        h, u, sv = _layer_fwd(cfg, l, h, u, layers[l], io)
        saved.append(sv)
    dh, loss_blk = loss_head(cfg, h, target_p, "loss_head")
    for l in reversed(range(cfg.DEPTH)):
        dh = _layer_bwd(cfg, l, dh, layers[l], saved[l], io)
    return loss_blk, dh, [io.grads[l] for l in range(cfg.DEPTH)]


def _cols(g):
    return g.transpose(1, 0, 2).reshape(g.shape[1], -1)


def _rows(g):
    return g.reshape(-1, g.shape[2])


def _send_cols(g):
    r, c = g.shape
    return g.reshape(r, NCHIP, c // NCHIP).transpose(1, 0, 2).astype(MXU)


def _send_rows(g):
    r, c = g.shape
    return g.reshape(NCHIP, r // NCHIP, c).astype(MXU)


class _StreamedWeights:
    def __init__(self, cfg, w):
        self.cfg = cfg
        self.shard = {n: [w[n][l].astype(MXU) for l in range(cfg.DEPTH)] for n in BIG}
        self.grads, self.recv = {}, {n: [None] * cfg.DEPTH for n in BIG}
        self.win = {0: _cols(chip_exchange([self.shard['w_in'][0]], False, "gather_w_in0")[0])}
        self.pending = None

    def w_in(self, l):
        return self.win[l]

    def fwd_exchanges(self, l):
        s = self.shard
        fox = [s['w_down'][l]] + ([s['w_in'][l + 1]] if l + 1 < self.cfg.DEPTH else [])
        return ([s['w_out'][l], s['w_up'][l]], False), (fox, False)

    def rest(self, l, got_sb, got_fox):
        if l + 1 < self.cfg.DEPTH:
            self.win[l + 1] = _cols(got_fox[1])
        return dict(w_out=_rows(got_sb[0]), w_up=got_sb[1], w_down=_rows(got_fox[0]))

    def bwd_exchanges(self, l, g):
        sb = [g['w_up']] + ([] if self.pending is None else [self.pending])
        self.pending = None
        return (sb, True), ([_send_rows(g['w_down']), _send_rows(g['w_out'])], True)

    def in_dx_exchange(self, l, g):
        send = _send_cols(g['w_in_ext'][:, :self.cfg.N_IN])
        if l == 0:
            return [send], True
        self.pending = send
        return None

    def bwd_done(self, l, g, got_sb, got_fox, got_in):
        self.grads[l] = g
        self.recv['w_up'][l] = got_sb[0]
        if len(got_sb) > 1:
            self.recv['w_in'][l + 1] = got_sb[1]
        self.recv['w_down'][l], self.recv['w_out'][l] = got_fox
        if got_in is not None:
            self.recv['w_in'][l] = got_in[0]


def _step(cfg, x, w, target, m, v):
    xi, yi, ci = _coords()
    chip = 2 * xi + yi
    D, LP, L, NM = cfg.D, cfg.LP, cfg.L, cfg.NMETA
    DEP = cfg.DEPTH
    io = _StreamedWeights(cfg, w)
    wf = {}
    small_in = _pack([lax.dynamic_update_slice(jnp.zeros((NM, D), F32), w['meta'], (0, chip * (D // NCHIP))),
                      lax.dynamic_update_slice(jnp.zeros((DEP, 3, cfg.F2), F32), w['conv_w'],
                                               (0, 0, chip * (cfg.F2 // NCHIP)))])
    small_in = jnp.where(ci == 0, small_in, 0.0)
    meta_full, conv_w_full = _unpack(all_reduce_small(small_in, "gather_small"), [(NM, D), (DEP, 3, cfg.F2)])
    for n in SMALL:
        wf[n] = w[n]
    wf['conv_w'] = conv_w_full

    zpad = jnp.zeros((LP - L, D), F32)
    h0 = jnp.concatenate([meta_full, x[0], zpad], axis=0)
    target_p = jnp.concatenate([jnp.zeros((NM, D), F32), target[0], zpad], axis=0)
    loss_blk, dh0, grads = _local_step(cfg, h0, target_p, wf, io)

    def stack(key, shape):
        return jnp.stack([grads[l][key].reshape(shape) for l in range(DEP)])

    conv_g = jnp.stack([_gate_up_inv(grads[l]['conv']) for l in range(DEP)])
    small_g = dict(
        loss=loss_blk[0:1, 0:1], meta=dh0[:NM], g_mix_pre=stack('g_mix_pre', (D,)),
        b_f=stack('b_f', (LANES,))[:, :cfg.NH], g_sb=stack('g_sb', (cfg.NH, HD)), g_fox=stack('g_fox', (cfg.NH, HD)),
        g_mix_post=stack('g_mix_post', (D,)), g_ffn_pre=stack('g_ffn_pre', (D,)),
        conv_w=conv_g[:, 0:3], conv_b=conv_g[:, 3], g_ffn_post=stack('g_ffn_post', (D,)))
    keys = list(small_g)
    red = dict(zip(keys, _unpack(all_reduce_small(_pack([small_g[k] for k in keys]), "reduce_small"),
                                 [small_g[k].shape for k in keys])))
    loss = red['loss'].reshape(())
    red['meta'] = lax.dynamic_slice(red['meta'], (0, chip * (D // NCHIP)), (NM, D // NCHIP))
    red['conv_w'] = lax.dynamic_slice(red['conv_w'], (0, 0, chip * (cfg.F2 // NCHIP)), (DEP, 3, cfg.F2 // NCHIP))

    recv = io.recv
    part =[sum_slots(recv[n], f"sum_{n}") for n in BIG]
    other = sibling_exchange(part, "sibling_grads")

    outs = {}
    for n, p, q in zip(BIG, part, other):
        shp = w[n].shape
        s2 = (shp[0] * shp[1], shp[2])
        res = adamw(w[n].reshape(s2), m[n].reshape(s2), v[n].reshape(s2), [p, q], f"adamw_{n}")
        outs[n] = [r.reshape(shp) for r in res]

    shapes = [w[n].shape for n in SMALL]
    res = adamw(_pack([w[n] for n in SMALL]), _pack([m[n] for n in SMALL]), _pack([v[n] for n in SMALL]),
                [_pack([red[n] for n in SMALL])], "adamw_small")
    res = [_unpack(r, shapes) for r in res]
    for i, n in enumerate(SMALL):
        outs[n] = [res[k][i] for k in range(4)]

    grad_x = dh0[NM:L][None]
    return (loss, grad_x, *[outs[n][0] for n in WEIGHTS], *[outs[n][1] for n in WEIGHTS],
            *[outs[n][2] for n in WEIGHTS], *[outs[n][3] for n in WEIGHTS])


def kernel(x, meta, g_mix_pre, w_in, b_f, g_sb, g_fox, w_out, g_mix_post, g_ffn_pre, w_up, conv_w, conv_b, w_down, g_ffn_post, loss_target, m_meta, m_g_mix_pre, m_w_in, m_b_f, m_g_sb, m_g_fox, m_w_out, m_g_mix_post, m_g_ffn_pre, m_w_up, m_conv_w, m_conv_b, m_w_down, m_g_ffn_post, v_meta, v_g_mix_pre, v_w_in, v_b_f, v_g_sb, v_g_fox, v_w_out, v_g_mix_post, v_g_ffn_pre, v_w_up, v_conv_w, v_conv_b, v_w_down, v_g_ffn_post):
    w = dict(zip(WEIGHTS, (meta, g_mix_pre, w_in, b_f, g_sb, g_fox, w_out, g_mix_post, g_ffn_pre, w_up, conv_w,
                           conv_b, w_down, g_ffn_post)))
    m = dict(zip(WEIGHTS, (m_meta, m_g_mix_pre, m_w_in, m_b_f, m_g_sb, m_g_fox, m_w_out, m_g_mix_post, m_g_ffn_pre,
                           m_w_up, m_conv_w, m_conv_b, m_w_down, m_g_ffn_post)))
    v = dict(zip(WEIGHTS, (v_meta, v_g_mix_pre, v_w_in, v_b_f, v_g_sb, v_g_fox, v_w_out, v_g_mix_post, v_g_ffn_pre,
                           v_w_up, v_conv_w, v_conv_b, v_w_down, v_g_ffn_post)))
    return _step(PROD, x, w, loss_target, m, v)
```

```python
import functools
import math
from typing import NamedTuple

import jax
import jax.numpy as jnp
from jax import lax
from jax.experimental import pallas as pl
from jax.experimental.pallas import tpu as pltpu

F32 = jnp.float32
MXU = jnp.bfloat16
HD = 128
LANES = 128
EPS = 1e-6
NEG = -1e30
LK_PIECES = 2
ADAM_LR, ADAM_B1, ADAM_B2, ADAM_EPS, ADAM_WD, ADAM_STEP = 0.001, 0.9, 0.999, 1e-08, 0.01, 10
VMEM_LIMIT = 56 * 1024 * 1024
MESH = pl.DeviceIdType.MESH
NCHIP = 4

WEIGHTS = ['meta', 'g_mix_pre', 'w_in', 'b_f', 'g_sb', 'g_fox', 'w_out', 'g_mix_post', 'g_ffn_pre',
           'w_up', 'conv_w', 'conv_b', 'w_down', 'g_ffn_post']
BIG = ['w_in', 'w_out', 'w_up', 'w_down']
SMALL = [n for n in WEIGHTS if n not in BIG]

NT = (((1,), (1,)), ((), ()))
TN = (((0,), (0,)), ((), ()))
NN = (((1,), (0,)), ((), ()))


class Cfg(NamedTuple):
    D: int
    SEQ: int
    NMETA: int
    NH: int
    F: int
    LP: int
    tq: int
    tr: int
    FB: int
    DEPTH: int = 2

    @property
    def L(self): return self.SEQ + self.NMETA
    @property
    def WG(self): return self.NH * HD
    @property
    def WMIX(self): return 2 * self.WG
    @property
    def NQKV(self): return 6 * self.WG
    @property
    def N_IN(self): return self.NQKV + self.NH
    @property
    def NEXT(self): return self.NQKV + LANES
    @property
    def F2(self): return 2 * self.F


PROD = Cfg(D=2048, SEQ=4096, NMETA=16, NH=8, F=5632, LP=4224, tq=384, tr=192, FB=512)


def _tile(n, cap, mult=LANES):
    best = None
    for t in range(mult, min(n, cap) + 1, mult):
        if n % t == 0:
            best = t
    return best if best is not None else n


def _cp(sem):
    return pltpu.CompilerParams(dimension_semantics=sem, vmem_limit_bytes=VMEM_LIMIT)


def _split_dot(x, tri, pieces):
    acc, r = None, x
    for p in range(pieces):
        xp = r.astype(MXU)
        d = jnp.dot(xp, tri, preferred_element_type=F32)
        acc = d if acc is None else acc + d
        if p + 1 < pieces:
            r = r - xp.astype(F32)
    return acc


def _split_dot_left(tri, x, pieces):
    acc, r = None, x
    for p in range(pieces):
        xp = r.astype(MXU)
        d = jnp.dot(tri, xp, preferred_element_type=F32)
        acc = d if acc is None else acc + d
        if p + 1 < pieces:
            r = r - xp.astype(F32)
    return acc


def _softplus(z):
    return jnp.maximum(z, 0.0) + jnp.log(1.0 + jnp.exp(-jnp.abs(z)))


def _shape2(x):
    return tuple(x.shape) if x.ndim == 2 else (x.shape[1], x.shape[0] * x.shape[2])


def _split_width(x):
    return x.shape[-1]


def _spec2(x, rb, cb, idx):
    if len(x.shape) == 2:
        return pl.BlockSpec((rb, cb), idx)
    per = x.shape[2] // cb

    def im(i, j, k):
        ri, ci = idx(i, j, k)
        return ci // per, ri, ci % per

    return pl.BlockSpec((None, rb, cb), im)


def matmul(a, b, mode, out_dtype, name, tm_cap=1408, tn_cap=1024, tk_cap=8192, out_split=None, xfer=None):
    (K, M) = _shape2(a) if mode == 'tn' else _shape2(a)[::-1]
    N = _shape2(b)[0] if mode == 'nt' else _shape2(b)[1]
    n_unit = math.gcd(N // (out_split or 1), _split_width(b) if mode != 'nt' else N)
    k_unit = math.gcd(_split_width(a) if mode != 'tn' else K, _split_width(b) if mode == 'nt' else K)
    tm = _tile(M, tm_cap, LANES if mode == 'tn' else 8)
    tn, tk = _tile(n_unit, tn_cap), _tile(k_unit, tk_cap)
    nk = K // tk
    dn = {'nn': NN, 'nt': NT, 'tn': TN}[mode]

    def body(a_ref, b_ref, o_ref, *scratch):
        d = lax.dot_general(a_ref[...], b_ref[...], dn, preferred_element_type=F32)
        if nk == 1:
            o_ref[...] = d.astype(out_dtype)
        else:
            acc_ref, = scratch
            k = pl.program_id(2)

            @pl.when(k == 0)
            def _():
                acc_ref[...] = d

            @pl.when(k > 0)
            def _():
                acc_ref[...] += d

            @pl.when(k == nk - 1)
            def _():
                o_ref[...] = acc_ref[...].astype(out_dtype)

    a_spec = (_spec2(a, tk, tm, lambda i, j, k: (k, i)) if mode == 'tn'
              else _spec2(a, tm, tk, lambda i, j, k: (i, k)))
    b_spec = (_spec2(b, tn, tk, lambda i, j, k: (j, k)) if mode == 'nt'
              else _spec2(b, tk, tn, lambda i, j, k: (k, j)))
    out = (jax.ShapeDtypeStruct((M, N), out_dtype) if out_split is None
           else jax.ShapeDtypeStruct((out_split, M, N // out_split), out_dtype))
    (res,), got = call_with_exchange(
        body, name, (M // tm, N // tn, nk), [a_spec, b_spec], (_spec2(out, tm, tn, lambda i, j, k: (i, j)),),
        (out,), [] if nk == 1 else [pltpu.VMEM((tm, tn), F32)], (a, b), xfer)
    return res if xfer is None else (res, got)


def _rstd(x):
    return lax.rsqrt(jnp.mean(x * x, axis=-1, keepdims=True) + EPS)


def pre_norm(cfg, h, g, name):
    LP, D, tr = cfg.LP, cfg.D, cfg.tr

    def body(h_ref, g_ref, u_ref):
        x = h_ref[...]
        u_ref[...] = ((x * _rstd(x)) * g_ref[...]).astype(MXU)

    row = pl.BlockSpec((tr, D), lambda i: (i, 0))
    vec = pl.BlockSpec((1, D), lambda i: (0, 0))
    return pl.pallas_call(body, name=name, out_shape=jax.ShapeDtypeStruct((LP, D), MXU), grid=(LP // tr,),
                          in_specs=[row, vec], out_specs=row, compiler_params=_cp(("parallel",)))(h, g)


def resid_norm(cfg, y, h, g_post, g_next, name):
    LP, D, tr = cfg.LP, cfg.D, cfg.tr

    def body(y_ref, h_ref, gp_ref, gn_ref, hn_ref, u_ref):
        yv = y_ref[...]
        hn = h_ref[...] + (yv * _rstd(yv)) * gp_ref[...]
        hn_ref[...] = hn
        u_ref[...] = ((hn * _rstd(hn)) * gn_ref[...]).astype(MXU)

    row = pl.BlockSpec((tr, D), lambda i: (i, 0))
    vec = pl.BlockSpec((1, D), lambda i: (0, 0))
    return pl.pallas_call(
        body, name=name,
        out_shape=(jax.ShapeDtypeStruct((LP, D), F32), jax.ShapeDtypeStruct((LP, D), MXU)),
        grid=(LP // tr,), in_specs=[row, row, vec, vec], out_specs=(row, row),
        compiler_params=_cp(("parallel",)))(y, h, g_post, g_next)


def loss_head(cfg, h, target, name):
    LP, D, tr = cfg.LP, cfg.D, cfg.tr
    lo, hi = cfg.NMETA, cfg.L

    def body(h_ref, t_ref, dh_ref, loss_ref):
        i = pl.program_id(0)
        rows = lax.broadcasted_iota(jnp.int32, (tr, D), 0) + i * tr
        diff = jnp.where((rows >= lo) & (rows < hi), h_ref[...] - t_ref[...], 0.0)
        dh_ref[...] = diff * (1.0 / D)
        part = 0.5 * jnp.sum(jnp.sum(diff * diff, axis=1, keepdims=True), axis=0, keepdims=True) * (1.0 / D)

        @pl.when(i == 0)
        def _():
            loss_ref[...] = jnp.zeros_like(loss_ref)

        loss_ref[...] += jnp.broadcast_to(part, loss_ref.shape)

    row = pl.BlockSpec((tr, D), lambda i: (i, 0))
    return pl.pallas_call(
        body, name=name,
        out_shape=(jax.ShapeDtypeStruct((LP, D), F32), jax.ShapeDtypeStruct((8, LANES), F32)),
        grid=(LP // tr,), in_specs=[row, row],
        out_specs=(row, pl.BlockSpec((8, LANES), lambda i: (0, 0))),
        compiler_params=_cp(("arbitrary",)))(h, target)


def norm_bwd(cfg, x, g, dy, dres, out_dtype, name):
    LP, D, tr = cfg.LP, cfg.D, cfg.tr
    has_res = dres is not None

    def body(*refs):
        if has_res:
            x_ref, g_ref, dy_ref, dres_ref, dx_ref, dg_ref = refs
        else:
            x_ref, g_ref, dy_ref, dx_ref, dg_ref = refs
        xv, dyv = x_ref[...], dy_ref[...]
        r = _rstd(xv)
        xhat = xv * r
        gdy = dyv * g_ref[...]
        dx = r * (gdy - xhat * jnp.mean(gdy * xhat, axis=-1, keepdims=True))
        if has_res:
            dx = dx + dres_ref[...]
        dx_ref[...] = dx.astype(out_dtype)

        @pl.when(pl.program_id(0) == 0)
        def _():
            dg_ref[...] = jnp.zeros_like(dg_ref)

        dg_ref[...] += jnp.sum(dyv * xhat, axis=0, keepdims=True)

    row = pl.BlockSpec((tr, D), lambda i: (i, 0))
    vec = pl.BlockSpec((1, D), lambda i: (0, 0))
    ins = [x, g, dy] + ([dres] if has_res else [])
    return pl.pallas_call(
        body, name=name,
        out_shape=(jax.ShapeDtypeStruct((LP, D), out_dtype), jax.ShapeDtypeStruct((1, D), F32)),
        grid=(LP // tr,), in_specs=[row, vec, row] + ([row] if has_res else []), out_specs=(row, vec),
        compiler_params=_cp(("arbitrary",)))(*ins)


HALO = 8


def _conv3(ext, w, b):
    s1, s2 = pltpu.roll(ext, 1, 0), pltpu.roll(ext, 2, 0)
    return w[0:1, :] * s2 + w[1:2, :] * s1 + w[2:3, :] * ext + b, s1, s2


def conv_act_fwd(cfg, a, cw, cb, name):
    LP, F, FB, tm = cfg.LP, cfg.F, cfg.FB, cfg.tr
    nb = tm // HALO

    def body(a_ref, prev_ref, w_ref, b_ref, act_ref):
        i = pl.program_id(1)
        c = []
        for hh in range(2):
            prev = jnp.where(i > 0, prev_ref[hh], 0.0)
            ext = jnp.concatenate([prev, a_ref[hh]], axis=0)
            c.append(_conv3(ext, w_ref[hh], b_ref[hh])[0][HALO:, :])
        act_ref[...] = (c[0] * jax.nn.sigmoid(c[0]) * c[1]).astype(MXU)

    return pl.pallas_call(
        body, name=name, out_shape=jax.ShapeDtypeStruct((LP, F), MXU), grid=(F // FB, LP // tm),
        in_specs=[pl.BlockSpec((2, tm, FB), lambda j, i: (0, i, j)),
                  pl.BlockSpec((2, HALO, FB), lambda j, i: (0, jnp.maximum(i * nb - 1, 0), j)),
                  pl.BlockSpec((2, 3, FB), lambda j, i: (0, 0, j)),
                  pl.BlockSpec((2, 1, FB), lambda j, i: (0, 0, j))],
        out_specs=pl.BlockSpec((tm, FB), lambda j, i: (i, j)),
        compiler_params=_cp(("parallel", "parallel")))(a, a, cw, cb)


def conv_act_bwd(cfg, a, d_act, cw, cb, name):
    LP, F, FB, tm = cfg.LP, cfg.F, cfg.FB, cfg.tr
    nb, last = tm // HALO, LP // tm - 1
    nrow = LP // HALO
    n = tm + 2 * HALO

    def body(a_ref, aprev_ref, anext_ref, d_ref, dnext_ref, w_ref, b_ref, da_ref, dcv_ref):
        i = pl.program_id(1)

        @pl.when(i == 0)
        def _():
            dcv_ref[...] = jnp.zeros_like(dcv_ref)

        c, exts = [], []
        for hh in range(2):
            prev = jnp.where(i > 0, aprev_ref[hh], 0.0)
            ext = jnp.concatenate([prev, a_ref[hh], anext_ref[hh]], axis=0)
            chh, s1, s2 = _conv3(ext, w_ref[hh], b_ref[hh])
            c.append(chh)
            exts.append((ext, s1, s2))
        dnext = jnp.where(i < last, dnext_ref[...], 0.0)
        dact = jnp.concatenate([jnp.zeros((HALO, FB), F32), d_ref[...], dnext], axis=0)
        sg = jax.nn.sigmoid(c[0])
        d_c = [dact * c[1] * (sg * (1.0 + c[0] * (1.0 - sg))), dact * (c[0] * sg)]
        for hh in range(2):
            dc, w = d_c[hh], w_ref[hh]
            da = w[2:3, :] * dc + w[1:2, :] * pltpu.roll(dc, n - 1, 0) + w[0:1, :] * pltpu.roll(dc, n - 2, 0)
            da_ref[hh] = da[HALO:HALO + tm, :].astype(MXU)
            dcb = dc[HALO:HALO + tm, :]
            ext, s1, s2 = exts[hh]
            dcv_ref[hh, 0:1, :] += jnp.sum(dcb * s2[HALO:HALO + tm, :], axis=0, keepdims=True)
            dcv_ref[hh, 1:2, :] += jnp.sum(dcb * s1[HALO:HALO + tm, :], axis=0, keepdims=True)
            dcv_ref[hh, 2:3, :] += jnp.sum(dcb * ext[HALO:HALO + tm, :], axis=0, keepdims=True)
            dcv_ref[hh, 3:4, :] += jnp.sum(dcb, axis=0, keepdims=True)

    return pl.pallas_call(
        body, name=name,
        out_shape=(jax.ShapeDtypeStruct((2, LP, F), MXU), jax.ShapeDtypeStruct((2, 8, F), F32)),
        grid=(F // FB, LP // tm),
        in_specs=[pl.BlockSpec((2, tm, FB), lambda j, i: (0, i, j)),
                  pl.BlockSpec((2, HALO, FB), lambda j, i: (0, jnp.maximum(i * nb - 1, 0), j)),
                  pl.BlockSpec((2, HALO, FB), lambda j, i: (0, jnp.minimum((i + 1) * nb, nrow - 1), j)),
                  pl.BlockSpec((tm, FB), lambda j, i: (i, j)),
                  pl.BlockSpec((HALO, FB), lambda j, i: (jnp.minimum((i + 1) * nb, nrow - 1), j)),
                  pl.BlockSpec((2, 3, FB), lambda j, i: (0, 0, j)),
                  pl.BlockSpec((2, 1, FB), lambda j, i: (0, 0, j))],
        out_specs=(pl.BlockSpec((2, tm, FB), lambda j, i: (0, i, j)),
                   pl.BlockSpec((2, 8, FB), lambda j, i: (0, 0, j))),
        compiler_params=_cp(("parallel", "arbitrary")))(a, a, a, d_act, d_act, cw, cb)


def fox_gate_fwd(cfg, f_logit, b_pad, name):
    LP, tb = cfg.LP, cfg.tq

    def body(f_ref, b_ref, c_ref, carry_ref):
        @pl.when(pl.program_id(0) == 0)
        def _():
            carry_ref[...] = jnp.zeros_like(carry_ref)

        xv = f_ref[...] + b_ref[...]
        lf = -_softplus(-xv)
        r = lax.broadcasted_iota(jnp.int32, (tb, tb), 0)
        s = lax.broadcasted_iota(jnp.int32, (tb, tb), 1)
        c = _split_dot_left((s <= r).astype(MXU), lf, 3) + carry_ref[0:1, :]
        c_ref[...] = c
        carry_ref[0:1, :] = c[tb - 1:tb, :]

    blk = pl.BlockSpec((tb, LANES), lambda i: (i, 0))
    return pl.pallas_call(
        body, name=name, out_shape=jax.ShapeDtypeStruct((LP, LANES), F32), grid=(LP // tb,),
        in_specs=[blk, pl.BlockSpec((1, LANES), lambda i: (0, 0))], out_specs=blk,
        scratch_shapes=[pltpu.VMEM((8, LANES), F32)], compiler_params=_cp(("arbitrary",)))(f_logit, b_pad)


def fox_gate_bwd(cfg, dc, f_logit, b_pad, name):
    LP, tb = cfg.LP, cfg.tq
    nblk = LP // tb

    def body(dc_ref, f_ref, b_ref, df_ref, db_ref, carry_ref):
        @pl.when(pl.program_id(0) == 0)
        def _():
            carry_ref[...] = jnp.zeros_like(carry_ref)
            db_ref[...] = jnp.zeros_like(db_ref)

        r = lax.broadcasted_iota(jnp.int32, (tb, tb), 0)
        s = lax.broadcasted_iota(jnp.int32, (tb, tb), 1)
        dlf = _split_dot_left((s >= r).astype(MXU), dc_ref[...], 3) + carry_ref[0:1, :]
        carry_ref[0:1, :] = dlf[0:1, :]
        df = dlf * jax.nn.sigmoid(-(f_ref[...] + b_ref[...]))
        df_ref[...] = df.astype(MXU)
        db_ref[...] += jnp.sum(df, axis=0, keepdims=True)

    blk = pl.BlockSpec((tb, LANES), lambda i: (nblk - 1 - i, 0))
    vec = pl.BlockSpec((1, LANES), lambda i: (0, 0))
    return pl.pallas_call(
        body, name=name,
        out_shape=(jax.ShapeDtypeStruct((LP, LANES), MXU), jax.ShapeDtypeStruct((1, LANES), F32)),
        grid=(nblk,), in_specs=[blk, blk, vec], out_specs=(blk, vec),
        scratch_shapes=[pltpu.VMEM((8, LANES), F32)], compiler_params=_cp(("arbitrary",)))(dc, f_logit, b_pad)


def _head_norm_fwd(o, g):
    return (o * lax.rsqrt(jnp.mean(o * o, axis=-1, keepdims=True) + EPS)) * g


def _head_norm_bwd(o, g, d_on):
    r = lax.rsqrt(jnp.mean(o * o, axis=-1, keepdims=True) + EPS)
    ohat = o * r
    gdy = d_on * g
    d_o = r * (gdy - ohat * jnp.mean(gdy * ohat, axis=-1, keepdims=True))
    return d_o, jnp.sum(d_on * ohat, axis=0, keepdims=True)


def _diag_masks(tq, strict):
    rows = lax.broadcasted_iota(jnp.int32, (tq, HD), 0)
    cols = lax.broadcasted_iota(jnp.int32, (tq, HD), 1)
    return [(cols + kk * HD < rows) if strict else (cols + kk * HD <= rows) for kk in range(tq // HD)]


def _walk_groups(i, R, group, carry, masks, descending):
    big = 2 * R
    if descending:
        carry = group(i * R, R, carry, masks)
        carry = lax.fori_loop(0, i % 2, lambda t, c: group((i - 1) * R, R, c, None), carry)
        return lax.fori_loop(0, i // 2, lambda t, c: group((i // 2 - 1 - t) * big, big, c, None), carry)
    carry = lax.fori_loop(0, i // 2, lambda t, c: group(t * big, big, c, None), carry)
    carry = lax.fori_loop(0, i % 2, lambda t, c: group((i // 2) * big, R, c, None), carry)
    return group(i * R, R, carry, masks)


def _tri(pred):
    a = lax.broadcasted_iota(jnp.int32, (HD, HD), 0)
    b = lax.broadcasted_iota(jnp.int32, (HD, HD), 1)
    return pred(a, b).astype(MXU)


def _attn_specs(cfg, group):
    base = 3 * cfg.NH * group
    return base, base + cfg.NH, base + 2 * cfg.NH


def sb_fwd(cfg, qkv, g3, name, xfer=None):
    LP, NH, tq = cfg.LP, cfg.NH, cfg.tq
    nq, R = LP // tq, tq // HD
    scale = HD ** -0.5
    cq, ck, cv = _attn_specs(cfg, 0)

    def body(q_ref, k_ref, v_ref, g_ref, opre_ref, on_ref, rtot_ref):
        i = pl.program_id(1)
        q = q_ref[...]
        masks = _diag_masks(tq, True)
        m_after = _tri(lambda a, b: a > b)

        def group(j0, n, carry, mk):
            acc, rc = carry
            masked = mk is not None
            order = range(n - 1, -1, -1)
            offs = [pl.multiple_of((j0 + kk) * HD, HD) for kk in range(n)]
            zs = [lax.dot_general(q, k_ref[pl.ds(offs[kk], HD), :], NT, preferred_element_type=F32) * scale
                  for kk in range(n)]
            ls, lks, tris = [None] * n, [None] * n, [None] * n
            for kk in order:
                sp = _softplus(zs[kk])
                lks[kk] = jnp.where(mk[kk], -sp, 0.0) if masked else -sp
                tris[kk] = _split_dot(lks[kk], m_after, LK_PIECES)
                ls[kk] = zs[kk] - sp
            for kk in order:
                a = jnp.exp(ls[kk] + (tris[kk] + rc))
                if masked:
                    a = jnp.where(mk[kk], a, 0.0)
                acc = acc + jnp.dot(a.astype(MXU), v_ref[pl.ds(offs[kk], HD), :], preferred_element_type=F32)
                rc = rc + jnp.sum(lks[kk], axis=1, keepdims=True)
            return acc, rc

        acc, rc = _walk_groups(i, R, group, (jnp.zeros((tq, HD), F32), jnp.zeros((tq, 1), F32)), masks, True)
        opre_ref[...] = acc
        on_ref[...] = _head_norm_fwd(acc, g_ref[0]).astype(MXU)
        rtot_ref[0] = rc

    return call_with_exchange(
        body, name, (NH, nq),
        [pl.BlockSpec((tq, HD), lambda h, i: (i, cq + h)),
         pl.BlockSpec((LP, HD), lambda h, i: (0, ck + h)),
         pl.BlockSpec((LP, HD), lambda h, i: (0, cv + h)),
         pl.BlockSpec((1, 1, HD), lambda h, i: (h, 0, 0))],
        (pl.BlockSpec((tq, HD), lambda h, i: (i, h)),
         pl.BlockSpec((tq, HD), lambda h, i: (i, h)),
         pl.BlockSpec((1, tq, 1), lambda h, i: (h, i, 0))),
        (jax.ShapeDtypeStruct((LP, cfg.WG), F32), jax.ShapeDtypeStruct((LP, cfg.WG), MXU),
         jax.ShapeDtypeStruct((NH, LP, 1), F32)),
        [], (qkv, qkv, qkv, g3), xfer)


def sb_bwd(cfg, qkv, g3, o_pre, d_on, rtot, name, xfer=None):
    LP, NH, tq = cfg.LP, cfg.NH, cfg.tq
    nq, R = LP // tq, tq // HD
    scale = HD ** -0.5
    cq, ck, cv = _attn_specs(cfg, 0)

    def body(q_ref, k_ref, v_ref, g_ref, o_ref, don_ref, rtot_ref, dq_ref, dk_ref, dv_ref, dg_ref,
             dk_acc, dv_acc):
        i = pl.program_id(1)

        @pl.when(i == 0)
        def _():
            dk_acc[...] = jnp.zeros_like(dk_acc)
            dv_acc[...] = jnp.zeros_like(dv_acc)
            dg_ref[...] = jnp.zeros_like(dg_ref)

        q = q_ref[...]
        d_o, dg = _head_norm_bwd(o_ref[...], g_ref[0], don_ref[...])
        dg_ref[0] += dg
        do_b = d_o.astype(MXU)
        rt = rtot_ref[0]
        masks = _diag_masks(tq, True)
        m_le = _tri(lambda a, b: a <= b)
        m_lt = _tri(lambda a, b: a < b)

        def group(j0, n, carry, mk):
            dq, lc, pc = carry
            masked = mk is not None
            offs = [pl.multiple_of((j0 + kk) * HD, HD) for kk in range(n)]
            zs = [lax.dot_general(q, k_ref[pl.ds(offs[kk], HD), :], NT, preferred_element_type=F32) * scale
                  for kk in range(n)]
            das = [lax.dot_general(do_b, v_ref[pl.ds(offs[kk], HD), :], NT, preferred_element_type=F32)
                   for kk in range(n)]
            ls, lks, tri1 = [], [], []
            for kk in range(n):
                sp = _softplus(zs[kk])
                lk = jnp.where(mk[kk], -sp, 0.0) if masked else -sp
                tri1.append(_split_dot(lk, m_le, LK_PIECES))
                lks.append(lk)
                ls.append(zs[kk] - sp)
            ggs, a_bs, tri2 = [], [], []
            for kk in range(n):
                a = jnp.exp(ls[kk] + ((rt - lc) - tri1[kk]))
                if masked:
                    a = jnp.where(mk[kk], a, 0.0)
                gg = a * das[kk]
                tri2.append(_split_dot(gg, m_lt, 2))
                ggs.append(gg)
                a_bs.append(a.astype(MXU))
                lc = lc + jnp.sum(lks[kk], axis=1, keepdims=True)
            for kk in range(n):
                sig = jnp.exp(ls[kk])
                dz = (ggs[kk] * (1.0 - sig) - sig * (pc + tri2[kk])) * scale
                if masked:
                    dz = jnp.where(mk[kk], dz, 0.0)
                dz_b = dz.astype(MXU)
                dq = dq + jnp.dot(dz_b, k_ref[pl.ds(offs[kk], HD), :], preferred_element_type=F32)
                dk_acc[pl.ds(offs[kk], HD), :] += lax.dot_general(dz_b, q, TN, preferred_element_type=F32)
                dv_acc[pl.ds(offs[kk], HD), :] += lax.dot_general(a_bs[kk], do_b, TN, preferred_element_type=F32)
                pc = pc + jnp.sum(ggs[kk], axis=1, keepdims=True)
            return dq, lc, pc

        zc = jnp.zeros((tq, 1), F32)
        dq, _, _ = _walk_groups(i, R, group, (jnp.zeros((tq, HD), F32), zc, zc), masks, False)
        dq_ref[...] = dq.astype(MXU)

        @pl.when(i == nq - 1)
        def _():
            dk_ref[...] = dk_acc[...].astype(MXU)
            dv_ref[...] = dv_acc[...].astype(MXU)

    blk = pl.BlockSpec((tq, HD), lambda h, i: (i, h))
    full = pl.BlockSpec((LP, HD), lambda h, i: (0, h))
    gspec = pl.BlockSpec((1, 1, HD), lambda h, i: (h, 0, 0))
    return call_with_exchange(
        body, name, (NH, nq),
        [pl.BlockSpec((tq, HD), lambda h, i: (i, cq + h)),
         pl.BlockSpec((LP, HD), lambda h, i: (0, ck + h)),
         pl.BlockSpec((LP, HD), lambda h, i: (0, cv + h)),
         gspec, blk, blk, pl.BlockSpec((1, tq, 1), lambda h, i: (h, i, 0))],
        (blk, full, full, gspec),
        (jax.ShapeDtypeStruct((LP, cfg.WG), MXU),) * 3 + (jax.ShapeDtypeStruct((NH, 1, HD), F32),),
        [pltpu.VMEM((LP, HD), F32), pltpu.VMEM((LP, HD), F32)],
        (qkv, qkv, qkv, g3, o_pre, d_on, rtot), xfer)


def fox_fwd(cfg, qkv, g3, c_col, c_row, name, xfer=None):
    LP, NH, tq = cfg.LP, cfg.NH, cfg.tq
    nq, R = LP // tq, tq // HD
    scale = HD ** -0.5
    cq, ck, cv = _attn_specs(cfg, 1)

    def body(q_ref, k_ref, v_ref, g_ref, ccol_ref, crow_ref, opre_ref, on_ref, lse_ref):
        i = pl.program_id(1)
        q = q_ref[...]
        cq_b = jnp.broadcast_to(ccol_ref[0], (tq, HD))
        masks = _diag_masks(tq, False)

        def group(j0, n, carry, mk):
            acc, m, l = carry
            ss, offs = [], []
            for kk in range(n):
                j = j0 + kk
                off = pl.multiple_of(j * HD, HD)
                s = (lax.dot_general(q, k_ref[pl.ds(off, HD), :], NT, preferred_element_type=F32) * scale
                     + (cq_b - crow_ref[0, pl.ds(j, 1), :]))
                ss.append(s if mk is None else jnp.where(mk[kk], s, NEG))
                offs.append(off)
            mx = jnp.max(ss[0], axis=1, keepdims=True)
            for s in ss[1:]:
                mx = jnp.maximum(mx, jnp.max(s, axis=1, keepdims=True))
            m_new = jnp.maximum(m, mx)
            alpha = jnp.exp(m - m_new)
            acc, l = alpha * acc, alpha * l
            m_b = jnp.broadcast_to(m_new, (tq, HD))
            for s, off in zip(ss, offs):
                p = jnp.exp(s - m_b)
                l = l + jnp.sum(p, axis=1, keepdims=True)
                acc = acc + _split_dot(p, v_ref[pl.ds(off, HD), :], 2)
            return acc, m_new, l

        carry = (jnp.zeros((tq, HD), F32), jnp.full((tq, 1), NEG, F32), jnp.zeros((tq, 1), F32))
        acc, m, l = _walk_groups(i, R, group, carry, masks, False)
        o = acc / l
        opre_ref[...] = o
        on_ref[...] = _head_norm_fwd(o, g_ref[0]).astype(MXU)
        lse_ref[0] = m + jnp.log(l)

    return call_with_exchange(
        body, name, (NH, nq),
        [pl.BlockSpec((tq, HD), lambda h, i: (i, cq + h)),
         pl.BlockSpec((LP, HD), lambda h, i: (0, ck + h)),
         pl.BlockSpec((LP, HD), lambda h, i: (0, cv + h)),
         pl.BlockSpec((1, 1, HD), lambda h, i: (h, 0, 0)),
         pl.BlockSpec((1, tq, 1), lambda h, i: (h, i, 0)),
         pl.BlockSpec((1, LP // HD, HD), lambda h, i: (h, 0, 0))],
        (pl.BlockSpec((tq, HD), lambda h, i: (i, h)),
         pl.BlockSpec((tq, HD), lambda h, i: (i, h)),
         pl.BlockSpec((1, tq, 1), lambda h, i: (h, i, 0))),
        (jax.ShapeDtypeStruct((LP, cfg.WG), F32), jax.ShapeDtypeStruct((LP, cfg.WG), MXU),
         jax.ShapeDtypeStruct((NH, LP, 1), F32)),
        [], (qkv, qkv, qkv, g3, c_col, c_row), xfer)


def fox_bwd(cfg, qkv, g3, c_col, c_row, o_pre, d_on, lse, name, xfer=None):
    LP, NH, tq = cfg.LP, cfg.NH, cfg.tq
    nq, R = LP // tq, tq // HD
    scale = HD ** -0.5
    cq, ck, cv = _attn_specs(cfg, 1)

    def body(q_ref, k_ref, v_ref, g_ref, ccol_ref, crow_ref, o_ref, don_ref, lse_ref,
             dq_ref, dk_ref, dv_ref, dg_ref, dc_ref, dk_acc, dv_acc):
        i = pl.program_id(1)

        @pl.when(i == 0)
        def _():
            dk_acc[...] = jnp.zeros_like(dk_acc)
            dv_acc[...] = jnp.zeros_like(dv_acc)
            dg_ref[...] = jnp.zeros_like(dg_ref)
            dc_ref[...] = jnp.zeros_like(dc_ref)

        q = q_ref[...]
        ov = o_ref[...]
        d_o, dg = _head_norm_bwd(ov, g_ref[0], don_ref[...])
        dg_ref[0] += dg
        do_b = d_o.astype(MXU)
        delta = jnp.sum(do_b.astype(F32) * ov, axis=1, keepdims=True)
        cqv, lsev = ccol_ref[0], lse_ref[0]
        cl = cqv - lsev
        masks = _diag_masks(tq, False)

        def group(j0, n, dq, mk):
            offs = [pl.multiple_of((j0 + kk) * HD, HD) for kk in range(n)]
            ss = [lax.dot_general(q, k_ref[pl.ds(offs[kk], HD), :], NT, preferred_element_type=F32) * scale
                  + (cl - crow_ref[0, pl.ds(j0 + kk, 1), :]) for kk in range(n)]
            dps = [lax.dot_general(do_b, v_ref[pl.ds(offs[kk], HD), :], NT, preferred_element_type=F32)
                   for kk in range(n)]
            for kk in range(n):
                p = jnp.exp(ss[kk])
                if mk is not None:
                    p = jnp.where(mk[kk], p, 0.0)
                ds = p * (dps[kk] - delta)
                dc_ref[0, pl.ds(j0 + kk, 1), :] -= jnp.sum(ds, axis=0, keepdims=True)
                ds_b = (ds * scale).astype(MXU)
                dk_acc[pl.ds(offs[kk], HD), :] += lax.dot_general(ds_b, q, TN, preferred_element_type=F32)
                dv_acc[pl.ds(offs[kk], HD), :] += lax.dot_general(p.astype(MXU), do_b, TN,
                                                                  preferred_element_type=F32)
                dq = dq + jnp.dot(ds_b, k_ref[pl.ds(offs[kk], HD), :], preferred_element_type=F32)
            return dq

        dq = _walk_groups(i, R, group, jnp.zeros((tq, HD), F32), masks, False)
        dq_ref[...] = dq.astype(MXU)

        @pl.when(i == nq - 1)
        def _():
            dk_ref[...] = dk_acc[...].astype(MXU)
            dv_ref[...] = dv_acc[...].astype(MXU)

    blk = pl.BlockSpec((tq, HD), lambda h, i: (i, h))
    full = pl.BlockSpec((LP, HD), lambda h, i: (0, h))
    gspec = pl.BlockSpec((1, 1, HD), lambda h, i: (h, 0, 0))
    col = pl.BlockSpec((1, tq, 1), lambda h, i: (h, i, 0))
    rowv = pl.BlockSpec((1, LP // HD, HD), lambda h, i: (h, 0, 0))
    return call_with_exchange(
        body, name, (NH, nq),
        [pl.BlockSpec((tq, HD), lambda h, i: (i, cq + h)),
         pl.BlockSpec((LP, HD), lambda h, i: (0, ck + h)),
         pl.BlockSpec((LP, HD), lambda h, i: (0, cv + h)),
         gspec, col, rowv, blk, blk, col],
        (blk, full, full, gspec, rowv),
        (jax.ShapeDtypeStruct((LP, cfg.WG), MXU),) * 3
        + (jax.ShapeDtypeStruct((NH, 1, HD), F32), jax.ShapeDtypeStruct((NH, LP // HD, HD), F32)),
        [pltpu.VMEM((LP, HD), F32), pltpu.VMEM((LP, HD), F32)],
        (qkv, qkv, qkv, g3, c_col, c_row, o_pre, d_on, lse), xfer)


def _row_tile(rows, cols, n_arrays):
    budget = 24 * 1024 * 1024 // (2 * n_arrays * cols * 4)
    cap = max(8, min(rows, budget // 8 * 8))
    div = _tile(rows, cap, 8)
    return div if div <= cap and div * 4 >= cap else cap


def sum_slots(recvs, name):
    S, rows, cols = recvs[0].shape
    nl = len(recvs)
    tr = _tile(rows, _row_tile(rows, cols, nl * S + 1), 8)
    nb = rows // tr

    def body(*refs):
        o_ref = refs[nl]
        layer = pl.program_id(0)
        for ll in range(nl):
            @pl.when(layer == ll)
            def _(r_ref=refs[ll]):
                acc = r_ref[0].astype(F32)
                for s in range(1, S):
                    acc = acc + r_ref[s].astype(F32)
                o_ref[...] = acc

    def spec(ll):
        return pl.BlockSpec((S, tr, cols), lambda l, i: (0, jnp.where(l == ll, i, jnp.where(l < ll, 0, nb - 1)), 0))

    return pl.pallas_call(
        body, name=name, out_shape=jax.ShapeDtypeStruct((nl * rows, cols), F32), grid=(nl, nb),
        in_specs=[spec(ll) for ll in range(nl)],
        out_specs=pl.BlockSpec((tr, cols), lambda l, i: (l * nb + i, 0)),
        compiler_params=_cp(("arbitrary", "arbitrary")))(*recvs)


def adamw(w, m, v, g_parts, name):
    rows, cols = w.shape
    npart = len(g_parts)
    tr = _row_tile(rows, cols, 7 + npart)
    c1 = 1.0 - ADAM_B1 ** ADAM_STEP
    c2 = 1.0 - ADAM_B2 ** ADAM_STEP

    def body(*refs):
        w_ref, m_ref, v_ref = refs[:3]
        g_refs = refs[3:3 + npart]
        g_out, d_out, m_out, v_out = refs[3 + npart:]
        g = g_refs[0][...]
        for r in g_refs[1:]:
            g = g + r[...]
        g_out[...] = g
        mn = ADAM_B1 * m_ref[...] + (1.0 - ADAM_B1) * g
        vn = ADAM_B2 * v_ref[...] + (1.0 - ADAM_B2) * (g * g)
        m_out[...] = mn
        v_out[...] = vn
        d_out[...] = -ADAM_LR * ((mn / c1) / (jnp.sqrt(vn / c2) + ADAM_EPS) + ADAM_WD * w_ref[...])

    blk = pl.BlockSpec((tr, cols), lambda i: (i, 0))
    return pl.pallas_call(
        body, name=name, out_shape=(jax.ShapeDtypeStruct((rows, cols), F32),) * 4,
        grid=(pl.cdiv(rows, tr),), in_specs=[blk] * (3 + npart), out_specs=(blk,) * 4,
        compiler_params=_cp(("parallel",)))(w, m, v, *g_parts)


def _coords():
    return lax.axis_index("x"), lax.axis_index("y"), lax.axis_index("c")


def chip_exchange(arrs, scatter, name):
    n = len(arrs)

    def body(*refs):
        ins, outs, sems = refs[:n], refs[n:2 * n], refs[2 * n:]
        _xfer_start(ins, outs, sems, scatter)
        _xfer_wait(ins, outs, sems, scatter)

    out_shape, scratch = _xfer_shapes(arrs, scatter)
    any_spec = pl.BlockSpec(memory_space=pl.ANY)
    return pl.pallas_call(body, name=name, out_shape=out_shape, in_specs=[any_spec] * n, out_specs=(any_spec,) * n,
                          scratch_shapes=scratch)(*arrs)


def _xfer_shapes(arrs, scatter):
    n = len(arrs)
    shapes = [a.shape[1:] if scatter else a.shape for a in arrs]
    out_shape = tuple(jax.ShapeDtypeStruct((NCHIP,) + tuple(s), a.dtype) for s, a in zip(shapes, arrs))
    scratch = [pltpu.SemaphoreType.DMA((n, 3)), pltpu.SemaphoreType.DMA((n, 3)), pltpu.SemaphoreType.DMA((n,))]
    return out_shape, scratch


def _xfer_copies(ins, outs, sems, scatter):
    send_sems, recv_sems, local_sems = sems
    n = len(ins)
    x, y, c = _coords()
    me = 2 * x + y
    peers = [(1 - x, y), (x, 1 - y), (1 - x, 1 - y)]

    def src(a, to):
        return ins[a].at[to] if scatter else ins[a]

    local = [pltpu.make_async_copy(src(a, me), outs[a].at[me], local_sems.at[a]) for a in range(n)]
    sends, recvs = [], []
    for a in range(n):
        for k, (px, py) in enumerate(peers):
            sends.append(pltpu.make_async_remote_copy(
                src_ref=src(a, 2 * px + py), dst_ref=outs[a].at[me], send_sem=send_sems.at[a, k],
                recv_sem=recv_sems.at[a, k], device_id=(px, py, c), device_id_type=MESH))
            recvs.append(pltpu.make_async_remote_copy(
                src_ref=src(a, me), dst_ref=outs[a].at[2 * px + py], send_sem=send_sems.at[a, k],
                recv_sem=recv_sems.at[a, k], device_id=(px, py, c), device_id_type=MESH))
    return local, sends, recvs


def _xfer_start(ins, outs, sems, scatter):
    local, sends, _ = _xfer_copies(ins, outs, sems, scatter)
    for cp in local + sends:
        cp.start()


def _xfer_wait(ins, outs, sems, scatter):
    local, sends, recvs = _xfer_copies(ins, outs, sems, scatter)
    for cp in recvs:
        cp.wait_recv()
    for cp in sends:
        cp.wait_send()
    for cp in local:
        cp.wait()


def call_with_exchange(core, name, grid, in_specs, out_specs, out_shape, scratch, args, xfer):
    n_in, n_out, n_scr = len(in_specs), len(out_specs), len(scratch)
    if xfer is None:
        res = pl.pallas_call(core, name=name, out_shape=out_shape, grid=grid, in_specs=in_specs,
                             out_specs=out_specs, scratch_shapes=scratch,
                             compiler_params=_cp(("arbitrary",) * len(grid)))(*args)
        return res, ()
    arrs, scatter = xfer
    nx = len(arrs)
    x_shape, x_scratch = _xfer_shapes(arrs, scatter)

    def body(*refs):
        a, xi = refs[:n_in], refs[n_in:n_in + nx]
        o, xo = refs[n_in + nx:n_in + nx + n_out], refs[n_in + nx + n_out:n_in + 2 * nx + n_out]
        rest = refs[n_in + 2 * nx + n_out:]
        scr, sems = rest[:n_scr], rest[n_scr:]
        first, last = None, None
        for d, g in enumerate(grid):
            f, l = pl.program_id(d) == 0, pl.program_id(d) == g - 1
            first = f if first is None else first & f
            last = l if last is None else last & l

        @pl.when(first)
        def _():
            _xfer_start(xi, xo, sems, scatter)

        core(*a, *o, *scr)

        @pl.when(last)
        def _():
            _xfer_wait(xi, xo, sems, scatter)

    any_spec = pl.BlockSpec(memory_space=pl.ANY)
    res = pl.pallas_call(
        body, name=name, out_shape=tuple(out_shape) + tuple(x_shape), grid=grid,
        in_specs=list(in_specs) + [any_spec] * nx, out_specs=tuple(out_specs) + (any_spec,) * nx,
        scratch_shapes=list(scratch) + x_scratch,
        compiler_params=_cp(("arbitrary",) * len(grid)))(*args, *arrs)
    return res[:n_out], res[n_out:]


def sibling_exchange(arrs, name):
    n = len(arrs)

    def body(*refs):
        ins, outs = refs[:n], refs[n:2 * n]
        send_sems, recv_sems = refs[2 * n:]
        x, y, c = _coords()
        cps = [pltpu.make_async_remote_copy(src_ref=ins[a], dst_ref=outs[a], send_sem=send_sems.at[a],
                                            recv_sem=recv_sems.at[a], device_id=(x, y, 1 - c),
                                            device_id_type=MESH) for a in range(n)]
        for cp in cps:
            cp.start()
        for cp in cps:
            cp.wait()

    any_spec = pl.BlockSpec(memory_space=pl.ANY)
    return pl.pallas_call(
        body, name=name, out_shape=tuple(jax.ShapeDtypeStruct(a.shape, a.dtype) for a in arrs),
        in_specs=[any_spec] * n, out_specs=(any_spec,) * n,
        scratch_shapes=[pltpu.SemaphoreType.DMA((n,)), pltpu.SemaphoreType.DMA((n,))],
    )(*arrs)


def all_reduce_small(pack, name):
    R = pack.shape[0]

    def body(p_ref, o_ref, slots, send_sems, recv_sems):
        x, y, c = _coords()
        me = 4 * x + 2 * y + c
        slots[me] = p_ref[...]
        cps = []
        for k in range(1, 8):
            kx, ky, kc = (k >> 2) & 1, (k >> 1) & 1, k & 1
            peer = ((1 - x) if kx else x, (1 - y) if ky else y, (1 - c) if kc else c)
            cp = pltpu.make_async_remote_copy(src_ref=p_ref, dst_ref=slots.at[me], send_sem=send_sems.at[k],
                                              recv_sem=recv_sems.at[k], device_id=peer, device_id_type=MESH)
            cp.start()
            cps.append((cp, peer))
        for k, (cp, peer) in enumerate(cps, start=1):
            frm = 4 * peer[0] + 2 * peer[1] + peer[2]
            pltpu.make_async_remote_copy(src_ref=p_ref, dst_ref=slots.at[frm], send_sem=send_sems.at[k],
                                         recv_sem=recv_sems.at[k], device_id=peer, device_id_type=MESH).wait_recv()
        for cp, _ in cps:
            cp.wait_send()
        acc = slots[0]
        for s in range(1, 8):
            acc = acc + slots[s]
        o_ref[...] = acc

    vm = pl.BlockSpec(memory_space=pltpu.VMEM)
    return pl.pallas_call(
        body, name=name, out_shape=jax.ShapeDtypeStruct((R, LANES), F32), in_specs=[vm], out_specs=vm,
        scratch_shapes=[pltpu.VMEM((8, R, LANES), F32), pltpu.SemaphoreType.DMA((8,)),
                        pltpu.SemaphoreType.DMA((8,))],
        compiler_params=pltpu.CompilerParams(vmem_limit_bytes=VMEM_LIMIT),
    )(pack)


def _gate_up(w):
    r, c = w.shape
    return w.reshape(r, 2, c // 2).transpose(1, 0, 2)


def _gate_up_inv(w):
    return w.transpose(1, 0, 2).reshape(w.shape[1], -1)


def _pack(arrs):
    parts = []
    for a in arrs:
        f = a.reshape(-1).astype(F32)
        parts.append(jnp.pad(f, (0, -f.shape[0] % 1024)).reshape(-1, LANES))
    return jnp.concatenate(parts, axis=0)


def _unpack(p, shapes):
    out, r = [], 0
    for s in shapes:
        n = math.prod(s)
        nr = (n + 1023) // 1024 * 8
        out.append(p[r:r + nr].reshape(-1)[:n].reshape(s))
        r += nr
    return out


def _vec(g):
    return g.reshape(1, -1)


class _LocalWeights:
    def __init__(self, cfg, wf):
        self.cfg, self.wf, self.grads = cfg, wf, {}

    def w_in(self, l):
        return self.wf['w_in'][l]

    def fwd_exchanges(self, l):
        return None, None

    def rest(self, l, got_sb, got_fox):
        w_up = self.wf['w_up'][l]
        return dict(w_out=self.wf['w_out'][l], w_down=self.wf['w_down'][l],
                    w_up=w_up.reshape(w_up.shape[0], NCHIP, -1).transpose(1, 0, 2))

    def bwd_exchanges(self, l, g):
        return None, None

    def in_dx_exchange(self, l, g):
        return None

    def bwd_done(self, l, g, got_sb, got_fox, got_in):
        self.grads[l] = g


def _w_in_parts(cfg, w_in):
    w_f = jnp.pad(w_in[:, cfg.NQKV:], ((0, 0), (0, LANES - cfg.NH)))
    return dict(w_qkv=w_in[:, :cfg.NQKV], w_f=w_f, w_in_ext=jnp.concatenate([w_in[:, :cfg.NQKV], w_f], axis=1))


def _layer_fwd(cfg, l, h, u, wl, io):
    tag = f"l{l}"
    qkv = matmul(u, wl['w_qkv'], 'nn', MXU, f"{tag}_qkv", tn_cap=1024)
    f_logit = matmul(u, wl['w_f'], 'nn', F32, f"{tag}_fproj")
    cpre = fox_gate_fwd(cfg, f_logit, wl['b_pad'], f"{tag}_gate_fwd")
    c_heads = cpre[:, :cfg.NH].T
    c_col = c_heads[:, :, None]
    c_row = c_heads.reshape(cfg.NH, cfg.LP // HD, HD)
    x_sb, x_fox = io.fwd_exchanges(l)
    (o_sb, on_sb, rtot), got_sb = sb_fwd(cfg, qkv, wl['g_sb'], f"{tag}_sb_fwd", x_sb)
    (o_fx, on_fx, lse), got_fox = fox_fwd(cfg, qkv, wl['g_fox'], c_col, c_row, f"{tag}_fox_fwd", x_fox)
    wl.update(io.rest(l, got_sb, got_fox))
    mixin = jnp.concatenate([on_sb, on_fx], axis=1)
    mix = matmul(mixin, wl['w_out'], 'nn', F32, f"{tag}_out")
    h1, u2 = resid_norm(cfg, mix, h, wl['g_mix_post'], wl['g_ffn_pre'], f"{tag}_mixres")
    a = matmul(u2, wl['w_up'], 'nn', F32, f"{tag}_up", tn_cap=1408, out_split=2)
    act = conv_act_fwd(cfg, a, wl['conv_w'], wl['conv_b'], f"{tag}_conv_fwd")
    ff = matmul(act, wl['w_down'], 'nn', F32, f"{tag}_down", tm_cap=704)
    h2, u_next = resid_norm(cfg, ff, h1, wl['g_ffn_post'], wl['g_next'], f"{tag}_ffnres")
    saved = dict(h=h, u=u, qkv=qkv, f_logit=f_logit, c_col=c_col, c_row=c_row, o_sb=o_sb, o_fx=o_fx, rtot=rtot,
                 lse=lse, mixin=mixin, mix=mix, h1=h1, u2=u2, a=a, act=act, ff=ff)
    return h2, u_next, saved


def _layer_bwd(cfg, l, dh2, wl, sv, io):
    tag = f"l{l}"
    g = {}
    d_ff, g['g_ffn_post'] = norm_bwd(cfg, sv['ff'], wl['g_ffn_post'], dh2, None, MXU, f"{tag}_ffnpost_bwd")
    d_act = matmul(d_ff, wl['w_down'], 'nt', F32, f"{tag}_down_dx")
    g['w_down'] = matmul(sv['act'], d_ff, 'tn', MXU, f"{tag}_down_dw", tm_cap=704, tk_cap=4224)
    d_a, d_conv = conv_act_bwd(cfg, sv['a'], d_act, wl['conv_w'], wl['conv_b'], f"{tag}_conv_bwd")
    g['conv'] = d_conv
    g['w_up'] = matmul(sv['u2'], d_a, 'tn', MXU, f"{tag}_up_dw", tm_cap=512, tn_cap=1408, tk_cap=4224,
                       out_split=NCHIP)
    du2 = matmul(d_a, wl['w_up'], 'nt', F32, f"{tag}_up_dx", tm_cap=704, tk_cap=5632)
    dh1, g['g_ffn_pre'] = norm_bwd(cfg, sv['h1'], wl['g_ffn_pre'], du2, dh2, F32, f"{tag}_ffnpre_bwd")
    d_mix, g['g_mix_post'] = norm_bwd(cfg, sv['mix'], wl['g_mix_post'], dh1, None, MXU, f"{tag}_mixpost_bwd")
    d_mixin = matmul(d_mix, wl['w_out'], 'nt', F32, f"{tag}_out_dx")
    g['w_out'] = matmul(sv['mixin'], d_mix, 'tn', MXU, f"{tag}_out_dw", tm_cap=1024, tk_cap=4224)
    WG = cfg.WG
    x_sb, x_fox = io.bwd_exchanges(l, g)
    (dq_s, dk_s, dv_s, g['g_sb']), got_sb = sb_bwd(cfg, sv['qkv'], wl['g_sb'], sv['o_sb'], d_mixin[:, :WG],
                                                   sv['rtot'], f"{tag}_sb_bwd", x_sb)
    (dq_f, dk_f, dv_f, g['g_fox'], dc_row), got_fox = fox_bwd(cfg, sv['qkv'], wl['g_fox'], sv['c_col'], sv['c_row'],
                                                              sv['o_fx'], d_mixin[:, WG:], sv['lse'],
                                                              f"{tag}_fox_bwd", x_fox)
    dc = jnp.pad(dc_row.reshape(cfg.NH, cfg.LP).T, ((0, 0), (0, LANES - cfg.NH)))
    d_f, g['b_f'] = fox_gate_bwd(cfg, dc, sv['f_logit'], wl['b_pad'], f"{tag}_gate_bwd")
    d_proj = jnp.concatenate([dq_s, dk_s, dv_s, dq_f, dk_f, dv_f, d_f], axis=1)
    g['w_in_ext'] = matmul(sv['u'], d_proj, 'tn', MXU, f"{tag}_in_dw", tm_cap=1024, tn_cap=896, tk_cap=4224)
    x_in = io.in_dx_exchange(l, g)
    du = matmul(d_proj, wl['w_in_ext'], 'nt', F32, f"{tag}_in_dx", tm_cap=704, xfer=x_in)
    du, got_in = du if x_in is not None else (du, None)
    dh, g['g_mix_pre'] = norm_bwd(cfg, sv['h'], wl['g_mix_pre'], du, dh1, F32, f"{tag}_mixpre_bwd")
    io.bwd_done(l, g, got_sb, got_fox, got_in)
    return dh


def _local_step(cfg, h0, target_p, wf, io=None):
    io = _LocalWeights(cfg, wf) if io is None else io
    layers = []
    for l in range(cfg.DEPTH):
        layers.append(dict(
            b_pad=jnp.pad(wf['b_f'][l], (0, LANES - cfg.NH)).reshape(1, LANES),
            g_sb=wf['g_sb'][l][:, None, :], g_fox=wf['g_fox'][l][:, None, :],
            conv_w=_gate_up(wf['conv_w'][l]), conv_b=wf['conv_b'][l].reshape(2, 1, cfg.F),
            g_mix_pre=_vec(wf['g_mix_pre'][l]), g_mix_post=_vec(wf['g_mix_post'][l]),
            g_ffn_pre=_vec(wf['g_ffn_pre'][l]), g_ffn_post=_vec(wf['g_ffn_post'][l]),
            g_next=_vec(wf['g_mix_pre'][(l + 1) % cfg.DEPTH])))
    h = h0
    u = pre_norm(cfg, h0, layers[0]['g_mix_pre'], "l0_prenorm")
    saved = []
    for l in range(cfg.DEPTH):
        layers[l].update(_w_in_parts(cfg, io.w_in(l)))
        h, u, sv = _layer_fwd(cfg, l, h, u, layers[l], io)
        saved.append(sv)
    dh, loss_blk = loss_head(cfg, h, target_p, "loss_head")
    for l in reversed(range(cfg.DEPTH)):
        dh = _layer_bwd(cfg, l, dh, layers[l], saved[l], io)
    return loss_blk, dh, [io.grads[l] for l in range(cfg.DEPTH)]


def _cols(g):
    return g.transpose(1, 0, 2).reshape(g.shape[1], -1)


def _rows(g):
    return g.reshape(-1, g.shape[2])


def _send_cols(g):
    r, c = g.shape
    return g.reshape(r, NCHIP, c // NCHIP).transpose(1, 0, 2).astype(MXU)


def _send_rows(g):
    r, c = g.shape
    return g.reshape(NCHIP, r // NCHIP, c).astype(MXU)


class _StreamedWeights:
    def __init__(self, cfg, w):
        self.cfg = cfg
        self.shard = {n: [w[n][l].astype(MXU) for l in range(cfg.DEPTH)] for n in BIG}
        self.grads, self.recv = {}, {n: [None] * cfg.DEPTH for n in BIG}
        self.win = {0: _cols(chip_exchange([self.shard['w_in'][0]], False, "gather_w_in0")[0])}
        self.pending = None

    def w_in(self, l):
        return self.win[l]

    def fwd_exchanges(self, l):
        s = self.shard
        fox = [s['w_down'][l]] + ([s['w_in'][l + 1]] if l + 1 < self.cfg.DEPTH else [])
        return ([s['w_out'][l], s['w_up'][l]], False), (fox, False)

    def rest(self, l, got_sb, got_fox):
        if l + 1 < self.cfg.DEPTH:
            self.win[l + 1] = _cols(got_fox[1])
        return dict(w_out=_rows(got_sb[0]), w_up=got_sb[1], w_down=_rows(got_fox[0]))

    def bwd_exchanges(self, l, g):
        sb = [g['w_up']] + ([] if self.pending is None else [self.pending])
        self.pending = None
        return (sb, True), ([_send_rows(g['w_down']), _send_rows(g['w_out'])], True)

    def in_dx_exchange(self, l, g):
        send = _send_cols(g['w_in_ext'][:, :self.cfg.N_IN])
        if l == 0:
            return [send], True
        self.pending = send
        return None

    def bwd_done(self, l, g, got_sb, got_fox, got_in):
        self.grads[l] = g
        self.recv['w_up'][l] = got_sb[0]
        if len(got_sb) > 1:
            self.recv['w_in'][l + 1] = got_sb[1]
        self.recv['w_down'][l], self.recv['w_out'][l] = got_fox
        if got_in is not None:
            self.recv['w_in'][l] = got_in[0]


def _step(cfg, x, w, target, m, v):
    xi, yi, ci = _coords()
    chip = 2 * xi + yi
    D, LP, L, NM = cfg.D, cfg.LP, cfg.L, cfg.NMETA
    DEP = cfg.DEPTH
    io = _StreamedWeights(cfg, w)
    wf = {}
    small_in = _pack([lax.dynamic_update_slice(jnp.zeros((NM, D), F32), w['meta'], (0, chip * (D // NCHIP))),
                      lax.dynamic_update_slice(jnp.zeros((DEP, 3, cfg.F2), F32), w['conv_w'],
                                               (0, 0, chip * (cfg.F2 // NCHIP)))])
    small_in = jnp.where(ci == 0, small_in, 0.0)
    meta_full, conv_w_full = _unpack(all_reduce_small(small_in, "gather_small"), [(NM, D), (DEP, 3, cfg.F2)])
    for n in SMALL:
        wf[n] = w[n]
    wf['conv_w'] = conv_w_full

    zpad = jnp.zeros((LP - L, D), F32)
    h0 = jnp.concatenate([meta_full, x[0], zpad], axis=0)
    target_p = jnp.concatenate([jnp.zeros((NM, D), F32), target[0], zpad], axis=0)
    loss_blk, dh0, grads = _local_step(cfg, h0, target_p, wf, io)

    def stack(key, shape):
        return jnp.stack([grads[l][key].reshape(shape) for l in range(DEP)])

    conv_g = jnp.stack([_gate_up_inv(grads[l]['conv']) for l in range(DEP)])
    small_g = dict(
        loss=loss_blk[0:1, 0:1], meta=dh0[:NM], g_mix_pre=stack('g_mix_pre', (D,)),
        b_f=stack('b_f', (LANES,))[:, :cfg.NH], g_sb=stack('g_sb', (cfg.NH, HD)), g_fox=stack('g_fox', (cfg.NH, HD)),
        g_mix_post=stack('g_mix_post', (D,)), g_ffn_pre=stack('g_ffn_pre', (D,)),
        conv_w=conv_g[:, 0:3], conv_b=conv_g[:, 3], g_ffn_post=stack('g_ffn_post', (D,)))
    keys = list(small_g)
    red = dict(zip(keys, _unpack(all_reduce_small(_pack([small_g[k] for k in keys]), "reduce_small"),
                                 [small_g[k].shape for k in keys])))
    loss = red['loss'].reshape(())
    red['meta'] = lax.dynamic_slice(red['meta'], (0, chip * (D // NCHIP)), (NM, D // NCHIP))
    red['conv_w'] = lax.dynamic_slice(red['conv_w'], (0, 0, chip * (cfg.F2 // NCHIP)), (DEP, 3, cfg.F2 // NCHIP))

    recv = io.recv
    part =[sum_slots(recv[n], f"sum_{n}") for n in BIG]
    other = sibling_exchange(part, "sibling_grads")

    outs = {}
    for n, p, q in zip(BIG, part, other):
        shp = w[n].shape
        s2 = (shp[0] * shp[1], shp[2])
        res = adamw(w[n].reshape(s2), m[n].reshape(s2), v[n].reshape(s2), [p, q], f"adamw_{n}")
        outs[n] = [r.reshape(shp) for r in res]

    shapes = [w[n].shape for n in SMALL]
    res = adamw(_pack([w[n] for n in SMALL]), _pack([m[n] for n in SMALL]), _pack([v[n] for n in SMALL]),
                [_pack([red[n] for n in SMALL])], "adamw_small")
    res = [_unpack(r, shapes) for r in res]
    for i, n in enumerate(SMALL):
        outs[n] = [res[k][i] for k in range(4)]

    grad_x = dh0[NM:L][None]
    return (loss, grad_x, *[outs[n][0] for n in WEIGHTS], *[outs[n][1] for n in WEIGHTS],
            *[outs[n][2] for n in WEIGHTS], *[outs[n][3] for n in WEIGHTS])


def kernel(x, meta, g_mix_pre, w_in, b_f, g_sb, g_fox, w_out, g_mix_post, g_ffn_pre, w_up, conv_w, conv_b, w_down, g_ffn_post, loss_target, m_meta, m_g_mix_pre, m_w_in, m_b_f, m_g_sb, m_g_fox, m_w_out, m_g_mix_post, m_g_ffn_pre, m_w_up, m_conv_w, m_conv_b, m_w_down, m_g_ffn_post, v_meta, v_g_mix_pre, v_w_in, v_b_f, v_g_sb, v_g_fox, v_w_out, v_g_mix_post, v_g_ffn_pre, v_w_up, v_conv_w, v_conv_b, v_w_down, v_g_ffn_post):
    w = dict(zip(WEIGHTS, (meta, g_mix_pre, w_in, b_f, g_sb, g_fox, w_out, g_mix_post, g_ffn_pre, w_up, conv_w,
                           conv_b, w_down, g_ffn_post)))
    m = dict(zip(WEIGHTS, (m_meta, m_g_mix_pre, m_w_in, m_b_f, m_g_sb, m_g_fox, m_w_out, m_g_mix_post, m_g_ffn_pre,
                           m_w_up, m_conv_w, m_conv_b, m_w_down, m_g_ffn_post)))
    v = dict(zip(WEIGHTS, (v_meta, v_g_mix_pre, v_w_in, v_b_f, v_g_sb, v_g_fox, v_w_out, v_g_mix_post, v_g_ffn_pre,
                           v_w_up, v_conv_w, v_conv_b, v_w_down, v_g_ffn_post)))
    return _step(PROD, x, w, loss_target, m, v)
```

```python
import functools
import math
from typing import NamedTuple

import jax
import jax.numpy as jnp
from jax import lax
from jax.experimental import pallas as pl
from jax.experimental.pallas import tpu as pltpu

F32 = jnp.float32
MXU = jnp.bfloat16
HD = 128
LANES = 128
EPS = 1e-6
NEG = -1e30
LK_PIECES = 2
ADAM_LR, ADAM_B1, ADAM_B2, ADAM_EPS, ADAM_WD, ADAM_STEP = 0.001, 0.9, 0.999, 1e-08, 0.01, 10
VMEM_LIMIT = 56 * 1024 * 1024
MESH = pl.DeviceIdType.MESH
NCHIP = 4

WEIGHTS = ['meta', 'g_mix_pre', 'w_in', 'b_f', 'g_sb', 'g_fox', 'w_out', 'g_mix_post', 'g_ffn_pre',
           'w_up', 'conv_w', 'conv_b', 'w_down', 'g_ffn_post']
BIG = ['w_in', 'w_out', 'w_up', 'w_down']
SMALL = [n for n in WEIGHTS if n not in BIG]

NT = (((1,), (1,)), ((), ()))
TN = (((0,), (0,)), ((), ()))
NN = (((1,), (0,)), ((), ()))


class Cfg(NamedTuple):
    D: int
    SEQ: int
    NMETA: int
    NH: int
    F: int
    LP: int
    tq: int
    tr: int
    FB: int
    DEPTH: int = 2

    @property
    def L(self): return self.SEQ + self.NMETA
    @property
    def WG(self): return self.NH * HD
    @property
    def WMIX(self): return 2 * self.WG
    @property
    def NQKV(self): return 6 * self.WG
    @property
    def N_IN(self): return self.NQKV + self.NH
    @property
    def NEXT(self): return self.NQKV + LANES
    @property
    def F2(self): return 2 * self.F


PROD = Cfg(D=2048, SEQ=4096, NMETA=16, NH=8, F=5632, LP=4224, tq=384, tr=192, FB=512)


def _tile(n, cap, mult=LANES):
    best = None
    for t in range(mult, min(n, cap) + 1, mult):
        if n % t == 0:
            best = t
    return best if best is not None else n


def _cp(sem):
    return pltpu.CompilerParams(dimension_semantics=sem, vmem_limit_bytes=VMEM_LIMIT)


def _split_dot(x, tri, pieces):
    acc, r = None, x
    for p in range(pieces):
        xp = r.astype(MXU)
        d = jnp.dot(xp, tri, preferred_element_type=F32)
        acc = d if acc is None else acc + d
        if p + 1 < pieces:
            r = r - xp.astype(F32)
    return acc


def _split_dot_left(tri, x, pieces):
    acc, r = None, x
    for p in range(pieces):
        xp = r.astype(MXU)
        d = jnp.dot(tri, xp, preferred_element_type=F32)
        acc = d if acc is None else acc + d
        if p + 1 < pieces:
            r = r - xp.astype(F32)
    return acc


def _softplus(z):
    return jnp.maximum(z, 0.0) + jnp.log(1.0 + jnp.exp(-jnp.abs(z)))


def _shape2(x):
    return tuple(x.shape) if x.ndim == 2 else (x.shape[1], x.shape[0] * x.shape[2])


def _split_width(x):
    return x.shape[-1]


def _spec2(x, rb, cb, idx):
    if len(x.shape) == 2:
        return pl.BlockSpec((rb, cb), idx)
    per = x.shape[2] // cb

    def im(i, j, k):
        ri, ci = idx(i, j, k)
        return ci // per, ri, ci % per

    return pl.BlockSpec((None, rb, cb), im)


def matmul(a, b, mode, out_dtype, name, tm_cap=1408, tn_cap=1024, tk_cap=8192, out_split=None, xfer=None):
    (K, M) = _shape2(a) if mode == 'tn' else _shape2(a)[::-1]
    N = _shape2(b)[0] if mode == 'nt' else _shape2(b)[1]
    n_unit = math.gcd(N // (out_split or 1), _split_width(b) if mode != 'nt' else N)
    k_unit = math.gcd(_split_width(a) if mode != 'tn' else K, _split_width(b) if mode == 'nt' else K)
    tm = _tile(M, tm_cap, LANES if mode == 'tn' else 8)
    tn, tk = _tile(n_unit, tn_cap), _tile(k_unit, tk_cap)
    nk = K // tk
    dn = {'nn': NN, 'nt': NT, 'tn': TN}[mode]

    def body(a_ref, b_ref, o_ref, *scratch):
        d = lax.dot_general(a_ref[...], b_ref[...], dn, preferred_element_type=F32)
        if nk == 1:
            o_ref[...] = d.astype(out_dtype)
        else:
            acc_ref, = scratch
            k = pl.program_id(2)

            @pl.when(k == 0)
            def _():
                acc_ref[...] = d

            @pl.when(k > 0)
            def _():
                acc_ref[...] += d

            @pl.when(k == nk - 1)
            def _():
                o_ref[...] = acc_ref[...].astype(out_dtype)

    a_spec = (_spec2(a, tk, tm, lambda i, j, k: (k, i)) if mode == 'tn'
              else _spec2(a, tm, tk, lambda i, j, k: (i, k)))
    b_spec = (_spec2(b, tn, tk, lambda i, j, k: (j, k)) if mode == 'nt'
              else _spec2(b, tk, tn, lambda i, j, k: (k, j)))
    out = (jax.ShapeDtypeStruct((M, N), out_dtype) if out_split is None
           else jax.ShapeDtypeStruct((out_split, M, N // out_split), out_dtype))
    (res,), got = call_with_exchange(
        body, name, (M // tm, N // tn, nk), [a_spec, b_spec], (_spec2(out, tm, tn, lambda i, j, k: (i, j)),),
        (out,), [] if nk == 1 else [pltpu.VMEM((tm, tn), F32)], (a, b), xfer)
    return res if xfer is None else (res, got)


def _rstd(x):
    return lax.rsqrt(jnp.mean(x * x, axis=-1, keepdims=True) + EPS)


def pre_norm(cfg, h, g, name):
    LP, D, tr = cfg.LP, cfg.D, cfg.tr

    def body(h_ref, g_ref, u_ref):
        x = h_ref[...]
        u_ref[...] = ((x * _rstd(x)) * g_ref[...]).astype(MXU)

    row = pl.BlockSpec((tr, D), lambda i: (i, 0))
    vec = pl.BlockSpec((1, D), lambda i: (0, 0))
    return pl.pallas_call(body, name=name, out_shape=jax.ShapeDtypeStruct((LP, D), MXU), grid=(LP // tr,),
                          in_specs=[row, vec], out_specs=row, compiler_params=_cp(("parallel",)))(h, g)


def resid_norm(cfg, y, h, g_post, g_next, name):
    LP, D, tr = cfg.LP, cfg.D, cfg.tr

    def body(y_ref, h_ref, gp_ref, gn_ref, hn_ref, u_ref):
        yv = y_ref[...]
        hn = h_ref[...] + (yv * _rstd(yv)) * gp_ref[...]
        hn_ref[...] = hn
        u_ref[...] = ((hn * _rstd(hn)) * gn_ref[...]).astype(MXU)

    row = pl.BlockSpec((tr, D), lambda i: (i, 0))
    vec = pl.BlockSpec((1, D), lambda i: (0, 0))
    return pl.pallas_call(
        body, name=name,
        out_shape=(jax.ShapeDtypeStruct((LP, D), F32), jax.ShapeDtypeStruct((LP, D), MXU)),
        grid=(LP // tr,), in_specs=[row, row, vec, vec], out_specs=(row, row),
        compiler_params=_cp(("parallel",)))(y, h, g_post, g_next)


def loss_head(cfg, h, target, name):
    LP, D, tr = cfg.LP, cfg.D, cfg.tr
    lo, hi = cfg.NMETA, cfg.L

    def body(h_ref, t_ref, dh_ref, loss_ref):
        i = pl.program_id(0)
        rows = lax.broadcasted_iota(jnp.int32, (tr, D), 0) + i * tr
        diff = jnp.where((rows >= lo) & (rows < hi), h_ref[...] - t_ref[...], 0.0)
        dh_ref[...] = diff * (1.0 / D)
        part = 0.5 * jnp.sum(jnp.sum(diff * diff, axis=1, keepdims=True), axis=0, keepdims=True) * (1.0 / D)

        @pl.when(i == 0)
        def _():
            loss_ref[...] = jnp.zeros_like(loss_ref)

        loss_ref[...] += jnp.broadcast_to(part, loss_ref.shape)

    row = pl.BlockSpec((tr, D), lambda i: (i, 0))
    return pl.pallas_call(
        body, name=name,
        out_shape=(jax.ShapeDtypeStruct((LP, D), F32), jax.ShapeDtypeStruct((8, LANES), F32)),
        grid=(LP // tr,), in_specs=[row, row],
        out_specs=(row, pl.BlockSpec((8, LANES), lambda i: (0, 0))),
        compiler_params=_cp(("arbitrary",)))(h, target)


def norm_bwd(cfg, x, g, dy, dres, out_dtype, name):
    LP, D, tr = cfg.LP, cfg.D, cfg.tr
    has_res = dres is not None

    def body(*refs):
        if has_res:
            x_ref, g_ref, dy_ref, dres_ref, dx_ref, dg_ref = refs
        else:
            x_ref, g_ref, dy_ref, dx_ref, dg_ref = refs
        xv, dyv = x_ref[...], dy_ref[...]
        r = _rstd(xv)
        xhat = xv * r
        gdy = dyv * g_ref[...]
        dx = r * (gdy - xhat * jnp.mean(gdy * xhat, axis=-1, keepdims=True))
        if has_res:
            dx = dx + dres_ref[...]
        dx_ref[...] = dx.astype(out_dtype)

        @pl.when(pl.program_id(0) == 0)
        def _():
            dg_ref[...] = jnp.zeros_like(dg_ref)

        dg_ref[...] += jnp.sum(dyv * xhat, axis=0, keepdims=True)

    row = pl.BlockSpec((tr, D), lambda i: (i, 0))
    vec = pl.BlockSpec((1, D), lambda i: (0, 0))
    ins = [x, g, dy] + ([dres] if has_res else [])
    return pl.pallas_call(
        body, name=name,
        out_shape=(jax.ShapeDtypeStruct((LP, D), out_dtype), jax.ShapeDtypeStruct((1, D), F32)),
        grid=(LP // tr,), in_specs=[row, vec, row] + ([row] if has_res else []), out_specs=(row, vec),
        compiler_params=_cp(("arbitrary",)))(*ins)


HALO = 8


def _conv3(ext, w, b):
    s1, s2 = pltpu.roll(ext, 1, 0), pltpu.roll(ext, 2, 0)
    return w[0:1, :] * s2 + w[1:2, :] * s1 + w[2:3, :] * ext + b, s1, s2


def conv_act_fwd(cfg, a, cw, cb, name):
    LP, F, FB, tm = cfg.LP, cfg.F, cfg.FB, cfg.tr
    nb = tm // HALO

    def body(a_ref, prev_ref, w_ref, b_ref, act_ref):
        i = pl.program_id(1)
        c = []
        for hh in range(2):
            prev = jnp.where(i > 0, prev_ref[hh], 0.0)
            ext = jnp.concatenate([prev, a_ref[hh]], axis=0)
            c.append(_conv3(ext, w_ref[hh], b_ref[hh])[0][HALO:, :])
        act_ref[...] = (c[0] * jax.nn.sigmoid(c[0]) * c[1]).astype(MXU)

    return pl.pallas_call(
        body, name=name, out_shape=jax.ShapeDtypeStruct((LP, F), MXU), grid=(F // FB, LP // tm),
        in_specs=[pl.BlockSpec((2, tm, FB), lambda j, i: (0, i, j)),
                  pl.BlockSpec((2, HALO, FB), lambda j, i: (0, jnp.maximum(i * nb - 1, 0), j)),
                  pl.BlockSpec((2, 3, FB), lambda j, i: (0, 0, j)),
                  pl.BlockSpec((2, 1, FB), lambda j, i: (0, 0, j))],
        out_specs=pl.BlockSpec((tm, FB), lambda j, i: (i, j)),
        compiler_params=_cp(("parallel", "parallel")))(a, a, cw, cb)


def conv_act_bwd(cfg, a, d_act, cw, cb, name):
    LP, F, FB, tm = cfg.LP, cfg.F, cfg.FB, cfg.tr
    nb, last = tm // HALO, LP // tm - 1
    nrow = LP // HALO
    n = tm + 2 * HALO

    def body(a_ref, aprev_ref, anext_ref, d_ref, dnext_ref, w_ref, b_ref, da_ref, dcv_ref):
        i = pl.program_id(1)

        @pl.when(i == 0)
        def _():
            dcv_ref[...] = jnp.zeros_like(dcv_ref)

        c, exts = [], []
        for hh in range(2):
            prev = jnp.where(i > 0, aprev_ref[hh], 0.0)
            ext = jnp.concatenate([prev, a_ref[hh], anext_ref[hh]], axis=0)
            chh, s1, s2 = _conv3(ext, w_ref[hh], b_ref[hh])
            c.append(chh)
            exts.append((ext, s1, s2))
        dnext = jnp.where(i < last, dnext_ref[...], 0.0)
        dact = jnp.concatenate([jnp.zeros((HALO, FB), F32), d_ref[...], dnext], axis=0)
        sg = jax.nn.sigmoid(c[0])
        d_c = [dact * c[1] * (sg * (1.0 + c[0] * (1.0 - sg))), dact * (c[0] * sg)]
        for hh in range(2):
            dc, w = d_c[hh], w_ref[hh]
            da = w[2:3, :] * dc + w[1:2, :] * pltpu.roll(dc, n - 1, 0) + w[0:1, :] * pltpu.roll(dc, n - 2, 0)
            da_ref[hh] = da[HALO:HALO + tm, :].astype(MXU)
            dcb = dc[HALO:HALO + tm, :]
            ext, s1, s2 = exts[hh]
            dcv_ref[hh, 0:1, :] += jnp.sum(dcb * s2[HALO:HALO + tm, :], axis=0, keepdims=True)
            dcv_ref[hh, 1:2, :] += jnp.sum(dcb * s1[HALO:HALO + tm, :], axis=0, keepdims=True)
            dcv_ref[hh, 2:3, :] += jnp.sum(dcb * ext[HALO:HALO + tm, :], axis=0, keepdims=True)
            dcv_ref[hh, 3:4, :] += jnp.sum(dcb, axis=0, keepdims=True)

    return pl.pallas_call(
        body, name=name,
        out_shape=(jax.ShapeDtypeStruct((2, LP, F), MXU), jax.ShapeDtypeStruct((2, 8, F), F32)),
        grid=(F // FB, LP // tm),
        in_specs=[pl.BlockSpec((2, tm, FB), lambda j, i: (0, i, j)),
                  pl.BlockSpec((2, HALO, FB), lambda j, i: (0, jnp.maximum(i * nb - 1, 0), j)),
                  pl.BlockSpec((2, HALO, FB), lambda j, i: (0, jnp.minimum((i + 1) * nb, nrow - 1), j)),
                  pl.BlockSpec((tm, FB), lambda j, i: (i, j)),
                  pl.BlockSpec((HALO, FB), lambda j, i: (jnp.minimum((i + 1) * nb, nrow - 1), j)),
                  pl.BlockSpec((2, 3, FB), lambda j, i: (0, 0, j)),
                  pl.BlockSpec((2, 1, FB), lambda j, i: (0, 0, j))],
        out_specs=(pl.BlockSpec((2, tm, FB), lambda j, i: (0, i, j)),
                   pl.BlockSpec((2, 8, FB), lambda j, i: (0, 0, j))),
        compiler_params=_cp(("parallel", "arbitrary")))(a, a, a, d_act, d_act, cw, cb)


def fox_gate_fwd(cfg, f_logit, b_pad, name):
    LP, tb = cfg.LP, cfg.tq

    def body(f_ref, b_ref, c_ref, carry_ref):
        @pl.when(pl.program_id(0) == 0)
        def _():
            carry_ref[...] = jnp.zeros_like(carry_ref)

        xv = f_ref[...] + b_ref[...]
        lf = -_softplus(-xv)
        r = lax.broadcasted_iota(jnp.int32, (tb, tb), 0)
        s = lax.broadcasted_iota(jnp.int32, (tb, tb), 1)
        c = _split_dot_left((s <= r).astype(MXU), lf, 3) + carry_ref[0:1, :]
        c_ref[...] = c
        carry_ref[0:1, :] = c[tb - 1:tb, :]

    blk = pl.BlockSpec((tb, LANES), lambda i: (i, 0))
    return pl.pallas_call(
        body, name=name, out_shape=jax.ShapeDtypeStruct((LP, LANES), F32), grid=(LP // tb,),
        in_specs=[blk, pl.BlockSpec((1, LANES), lambda i: (0, 0))], out_specs=blk,
        scratch_shapes=[pltpu.VMEM((8, LANES), F32)], compiler_params=_cp(("arbitrary",)))(f_logit, b_pad)


def fox_gate_bwd(cfg, dc, f_logit, b_pad, name):
    LP, tb = cfg.LP, cfg.tq
    nblk = LP // tb

    def body(dc_ref, f_ref, b_ref, df_ref, db_ref, carry_ref):
        @pl.when(pl.program_id(0) == 0)
        def _():
            carry_ref[...] = jnp.zeros_like(carry_ref)
            db_ref[...] = jnp.zeros_like(db_ref)

        r = lax.broadcasted_iota(jnp.int32, (tb, tb), 0)
        s = lax.broadcasted_iota(jnp.int32, (tb, tb), 1)
        dlf = _split_dot_left((s >= r).astype(MXU), dc_ref[...], 3) + carry_ref[0:1, :]
        carry_ref[0:1, :] = dlf[0:1, :]
        df = dlf * jax.nn.sigmoid(-(f_ref[...] + b_ref[...]))
        df_ref[...] = df.astype(MXU)
        db_ref[...] += jnp.sum(df, axis=0, keepdims=True)

    blk = pl.BlockSpec((tb, LANES), lambda i: (nblk - 1 - i, 0))
    vec = pl.BlockSpec((1, LANES), lambda i: (0, 0))
    return pl.pallas_call(
        body, name=name,
        out_shape=(jax.ShapeDtypeStruct((LP, LANES), MXU), jax.ShapeDtypeStruct((1, LANES), F32)),
        grid=(nblk,), in_specs=[blk, blk, vec], out_specs=(blk, vec),
        scratch_shapes=[pltpu.VMEM((8, LANES), F32)], compiler_params=_cp(("arbitrary",)))(dc, f_logit, b_pad)


def _head_norm_fwd(o, g):
    return (o * lax.rsqrt(jnp.mean(o * o, axis=-1, keepdims=True) + EPS)) * g


def _head_norm_bwd(o, g, d_on):
    r = lax.rsqrt(jnp.mean(o * o, axis=-1, keepdims=True) + EPS)
    ohat = o * r
    gdy = d_on * g
    d_o = r * (gdy - ohat * jnp.mean(gdy * ohat, axis=-1, keepdims=True))
    return d_o, jnp.sum(d_on * ohat, axis=0, keepdims=True)


def _diag_masks(tq, strict):
    rows = lax.broadcasted_iota(jnp.int32, (tq, HD), 0)
    cols = lax.broadcasted_iota(jnp.int32, (tq, HD), 1)
    return [(cols + kk * HD < rows) if strict else (cols + kk * HD <= rows) for kk in range(tq // HD)]


def _walk_groups(i, R, group, carry, masks, descending):
    big = 2 * R
    if descending:
        carry = group(i * R, R, carry, masks)
        carry = lax.fori_loop(0, i % 2, lambda t, c: group((i - 1) * R, R, c, None), carry)
        return lax.fori_loop(0, i // 2, lambda t, c: group((i // 2 - 1 - t) * big, big, c, None), carry)
    carry = lax.fori_loop(0, i // 2, lambda t, c: group(t * big, big, c, None), carry)
    carry = lax.fori_loop(0, i % 2, lambda t, c: group((i // 2) * big, R, c, None), carry)
    return group(i * R, R, carry, masks)


def _tri(pred):
    a = lax.broadcasted_iota(jnp.int32, (HD, HD), 0)
    b = lax.broadcasted_iota(jnp.int32, (HD, HD), 1)
    return pred(a, b).astype(MXU)


def _attn_specs(cfg, group):
    base = 3 * cfg.NH * group
    return base, base + cfg.NH, base + 2 * cfg.NH


def sb_fwd(cfg, qkv, g3, name, xfer=None):
    LP, NH, tq = cfg.LP, cfg.NH, cfg.tq
    nq, R = LP // tq, tq // HD
    scale = HD ** -0.5
    cq, ck, cv = _attn_specs(cfg, 0)

    def body(q_ref, k_ref, v_ref, g_ref, opre_ref, on_ref, rtot_ref):
        i = pl.program_id(1)
        q = q_ref[...]
        masks = _diag_masks(tq, True)
        m_after = _tri(lambda a, b: a > b)

        def group(j0, n, carry, mk):
            acc, rc = carry
            masked = mk is not None
            order = range(n - 1, -1, -1)
            offs = [pl.multiple_of((j0 + kk) * HD, HD) for kk in range(n)]
            zs = [lax.dot_general(q, k_ref[pl.ds(offs[kk], HD), :], NT, preferred_element_type=F32) * scale
                  for kk in range(n)]
            ls, lks, tris = [None] * n, [None] * n, [None] * n
            for kk in order:
                sp = _softplus(zs[kk])
                lks[kk] = jnp.where(mk[kk], -sp, 0.0) if masked else -sp
                tris[kk] = _split_dot(lks[kk], m_after, LK_PIECES)
                ls[kk] = zs[kk] - sp
            for kk in order:
                a = jnp.exp(ls[kk] + (tris[kk] + rc))
                if masked:
                    a = jnp.where(mk[kk], a, 0.0)
                acc = acc + jnp.dot(a.astype(MXU), v_ref[pl.ds(offs[kk], HD), :], preferred_element_type=F32)
                rc = rc + jnp.sum(lks[kk], axis=1, keepdims=True)
            return acc, rc

        acc, rc = _walk_groups(i, R, group, (jnp.zeros((tq, HD), F32), jnp.zeros((tq, 1), F32)), masks, True)
        opre_ref[...] = acc
        on_ref[...] = _head_norm_fwd(acc, g_ref[0]).astype(MXU)
        rtot_ref[0] = rc

    return call_with_exchange(
        body, name, (NH, nq),
        [pl.BlockSpec((tq, HD), lambda h, i: (i, cq + h)),
         pl.BlockSpec((LP, HD), lambda h, i: (0, ck + h)),
         pl.BlockSpec((LP, HD), lambda h, i: (0, cv + h)),
         pl.BlockSpec((1, 1, HD), lambda h, i: (h, 0, 0))],
        (pl.BlockSpec((tq, HD), lambda h, i: (i, h)),
         pl.BlockSpec((tq, HD), lambda h, i: (i, h)),
         pl.BlockSpec((1, tq, 1), lambda h, i: (h, i, 0))),
        (jax.ShapeDtypeStruct((LP, cfg.WG), F32), jax.ShapeDtypeStruct((LP, cfg.WG), MXU),
         jax.ShapeDtypeStruct((NH, LP, 1), F32)),
        [], (qkv, qkv, qkv, g3), xfer)


def sb_bwd(cfg, qkv, g3, o_pre, d_on, rtot, name, xfer=None):
    LP, NH, tq = cfg.LP, cfg.NH, cfg.tq
    nq, R = LP // tq, tq // HD
    scale = HD ** -0.5
    cq, ck, cv = _attn_specs(cfg, 0)

    def body(q_ref, k_ref, v_ref, g_ref, o_ref, don_ref, rtot_ref, dq_ref, dk_ref, dv_ref, dg_ref,
             dk_acc, dv_acc):
        i = pl.program_id(1)

        @pl.when(i == 0)
        def _():
            dk_acc[...] = jnp.zeros_like(dk_acc)
            dv_acc[...] = jnp.zeros_like(dv_acc)
            dg_ref[...] = jnp.zeros_like(dg_ref)

        q = q_ref[...]
        d_o, dg = _head_norm_bwd(o_ref[...], g_ref[0], don_ref[...])
        dg_ref[0] += dg
        do_b = d_o.astype(MXU)
        rt = rtot_ref[0]
        masks = _diag_masks(tq, True)
        m_le = _tri(lambda a, b: a <= b)
        m_lt = _tri(lambda a, b: a < b)

        def group(j0, n, carry, mk):
            dq, lc, pc = carry
            masked = mk is not None
            offs = [pl.multiple_of((j0 + kk) * HD, HD) for kk in range(n)]
            zs = [lax.dot_general(q, k_ref[pl.ds(offs[kk], HD), :], NT, preferred_element_type=F32) * scale
                  for kk in range(n)]
            das = [lax.dot_general(do_b, v_ref[pl.ds(offs[kk], HD), :], NT, preferred_element_type=F32)
                   for kk in range(n)]
            ls, lks, tri1 = [], [], []
            for kk in range(n):
                sp = _softplus(zs[kk])
                lk = jnp.where(mk[kk], -sp, 0.0) if masked else -sp
                tri1.append(_split_dot(lk, m_le, LK_PIECES))
                lks.append(lk)
                ls.append(zs[kk] - sp)
            ggs, a_bs, tri2 = [], [], []
            for kk in range(n):
                a = jnp.exp(ls[kk] + ((rt - lc) - tri1[kk]))
                if masked:
                    a = jnp.where(mk[kk], a, 0.0)
                gg = a * das[kk]
                tri2.append(_split_dot(gg, m_lt, 2))
                ggs.append(gg)
                a_bs.append(a.astype(MXU))
                lc = lc + jnp.sum(lks[kk], axis=1, keepdims=True)
            for kk in range(n):
                sig = jnp.exp(ls[kk])
                dz = (ggs[kk] * (1.0 - sig) - sig * (pc + tri2[kk])) * scale
                if masked:
                    dz = jnp.where(mk[kk], dz, 0.0)
                dz_b = dz.astype(MXU)
                dq = dq + jnp.dot(dz_b, k_ref[pl.ds(offs[kk], HD), :], preferred_element_type=F32)
                dk_acc[pl.ds(offs[kk], HD), :] += lax.dot_general(dz_b, q, TN, preferred_element_type=F32)
                dv_acc[pl.ds(offs[kk], HD), :] += lax.dot_general(a_bs[kk], do_b, TN, preferred_element_type=F32)
                pc = pc + jnp.sum(ggs[kk], axis=1, keepdims=True)
            return dq, lc, pc

        zc = jnp.zeros((tq, 1), F32)
        dq, _, _ = _walk_groups(i, R, group, (jnp.zeros((tq, HD), F32), zc, zc), masks, False)
        dq_ref[...] = dq.astype(MXU)

        @pl.when(i == nq - 1)
        def _():
            dk_ref[...] = dk_acc[...].astype(MXU)
            dv_ref[...] = dv_acc[...].astype(MXU)

    blk = pl.BlockSpec((tq, HD), lambda h, i: (i, h))
    full = pl.BlockSpec((LP, HD), lambda h, i: (0, h))
    gspec = pl.BlockSpec((1, 1, HD), lambda h, i: (h, 0, 0))
    return call_with_exchange(
        body, name, (NH, nq),
        [pl.BlockSpec((tq, HD), lambda h, i: (i, cq + h)),
         pl.BlockSpec((LP, HD), lambda h, i: (0, ck + h)),
         pl.BlockSpec((LP, HD), lambda h, i: (0, cv + h)),
         gspec, blk, blk, pl.BlockSpec((1, tq, 1), lambda h, i: (h, i, 0))],
        (blk, full, full, gspec),
        (jax.ShapeDtypeStruct((LP, cfg.WG), MXU),) * 3 + (jax.ShapeDtypeStruct((NH, 1, HD), F32),),
        [pltpu.VMEM((LP, HD), F32), pltpu.VMEM((LP, HD), F32)],
        (qkv, qkv, qkv, g3, o_pre, d_on, rtot), xfer)


def fox_fwd(cfg, qkv, g3, c_col, c_row, name, xfer=None):
    LP, NH, tq = cfg.LP, cfg.NH, cfg.tq
    nq, R = LP // tq, tq // HD
    scale = HD ** -0.5
    cq, ck, cv = _attn_specs(cfg, 1)

    def body(q_ref, k_ref, v_ref, g_ref, ccol_ref, crow_ref, opre_ref, on_ref, lse_ref):
        i = pl.program_id(1)
        q = q_ref[...]
        cq_b = jnp.broadcast_to(ccol_ref[0], (tq, HD))
        masks = _diag_masks(tq, False)

        def group(j0, n, carry, mk):
            acc, m, l = carry
            ss, offs = [], []
            for kk in range(n):
                j = j0 + kk
                off = pl.multiple_of(j * HD, HD)
                s = (lax.dot_general(q, k_ref[pl.ds(off, HD), :], NT, preferred_element_type=F32) * scale
                     + (cq_b - crow_ref[0, pl.ds(j, 1), :]))
                ss.append(s if mk is None else jnp.where(mk[kk], s, NEG))
                offs.append(off)
            mx = jnp.max(ss[0], axis=1, keepdims=True)
            for s in ss[1:]:
                mx = jnp.maximum(mx, jnp.max(s, axis=1, keepdims=True))
            m_new = jnp.maximum(m, mx)
            alpha = jnp.exp(m - m_new)
            acc, l = alpha * acc, alpha * l
            m_b = jnp.broadcast_to(m_new, (tq, HD))
            for s, off in zip(ss, offs):
                p = jnp.exp(s - m_b)
                l = l + jnp.sum(p, axis=1, keepdims=True)
                acc = acc + _split_dot(p, v_ref[pl.ds(off, HD), :], 2)
            return acc, m_new, l

        carry = (jnp.zeros((tq, HD), F32), jnp.full((tq, 1), NEG, F32), jnp.zeros((tq, 1), F32))
        acc, m, l = _walk_groups(i, R, group, carry, masks, False)
        o = acc / l
        opre_ref[...] = o
        on_ref[...] = _head_norm_fwd(o, g_ref[0]).astype(MXU)
        lse_ref[0] = m + jnp.log(l)

    return call_with_exchange(
        body, name, (NH, nq),
        [pl.BlockSpec((tq, HD), lambda h, i: (i, cq + h)),
         pl.BlockSpec((LP, HD), lambda h, i: (0, ck + h)),
         pl.BlockSpec((LP, HD), lambda h, i: (0, cv + h)),
         pl.BlockSpec((1, 1, HD), lambda h, i: (h, 0, 0)),
         pl.BlockSpec((1, tq, 1), lambda h, i: (h, i, 0)),
         pl.BlockSpec((1, LP // HD, HD), lambda h, i: (h, 0, 0))],
        (pl.BlockSpec((tq, HD), lambda h, i: (i, h)),
         pl.BlockSpec((tq, HD), lambda h, i: (i, h)),
         pl.BlockSpec((1, tq, 1), lambda h, i: (h, i, 0))),
        (jax.ShapeDtypeStruct((LP, cfg.WG), F32), jax.ShapeDtypeStruct((LP, cfg.WG), MXU),
         jax.ShapeDtypeStruct((NH, LP, 1), F32)),
        [], (qkv, qkv, qkv, g3, c_col, c_row), xfer)


def fox_bwd(cfg, qkv, g3, c_col, c_row, o_pre, d_on, lse, name, xfer=None):
    LP, NH, tq = cfg.LP, cfg.NH, cfg.tq
    nq, R = LP // tq, tq // HD
    scale = HD ** -0.5
    cq, ck, cv = _attn_specs(cfg, 1)

    def body(q_ref, k_ref, v_ref, g_ref, ccol_ref, crow_ref, o_ref, don_ref, lse_ref,
             dq_ref, dk_ref, dv_ref, dg_ref, dc_ref, dk_acc, dv_acc):
        i = pl.program_id(1)

        @pl.when(i == 0)
        def _():
            dk_acc[...] = jnp.zeros_like(dk_acc)
            dv_acc[...] = jnp.zeros_like(dv_acc)
            dg_ref[...] = jnp.zeros_like(dg_ref)
            dc_ref[...] = jnp.zeros_like(dc_ref)

        q = q_ref[...]
        ov = o_ref[...]
        d_o, dg = _head_norm_bwd(ov, g_ref[0], don_ref[...])
        dg_ref[0] += dg
        do_b = d_o.astype(MXU)
        delta = jnp.sum(do_b.astype(F32) * ov, axis=1, keepdims=True)
        cqv, lsev = ccol_ref[0], lse_ref[0]
        cl = cqv - lsev
        masks = _diag_masks(tq, False)

        def group(j0, n, dq, mk):
            offs = [pl.multiple_of((j0 + kk) * HD, HD) for kk in range(n)]
            ss = [lax.dot_general(q, k_ref[pl.ds(offs[kk], HD), :], NT, preferred_element_type=F32) * scale
                  + (cl - crow_ref[0, pl.ds(j0 + kk, 1), :]) for kk in range(n)]
            dps = [lax.dot_general(do_b, v_ref[pl.ds(offs[kk], HD), :], NT, preferred_element_type=F32)
                   for kk in range(n)]
            for kk in range(n):
                p = jnp.exp(ss[kk])
                if mk is not None:
                    p = jnp.where(mk[kk], p, 0.0)
                ds = p * (dps[kk] - delta)
                dc_ref[0, pl.ds(j0 + kk, 1), :] -= jnp.sum(ds, axis=0, keepdims=True)
                ds_b = (ds * scale).astype(MXU)
                dk_acc[pl.ds(offs[kk], HD), :] += lax.dot_general(ds_b, q, TN, preferred_element_type=F32)
                dv_acc[pl.ds(offs[kk], HD), :] += lax.dot_general(p.astype(MXU), do_b, TN,
                                                                  preferred_element_type=F32)
                dq = dq + jnp.dot(ds_b, k_ref[pl.ds(offs[kk], HD), :], preferred_element_type=F32)
            return dq

        dq = _walk_groups(i, R, group, jnp.zeros((tq, HD), F32), masks, False)
        dq_ref[...] = dq.astype(MXU)

        @pl.when(i == nq - 1)
        def _():
            dk_ref[...] = dk_acc[...].astype(MXU)
            dv_ref[...] = dv_acc[...].astype(MXU)

    blk = pl.BlockSpec((tq, HD), lambda h, i: (i, h))
    full = pl.BlockSpec((LP, HD), lambda h, i: (0, h))
    gspec = pl.BlockSpec((1, 1, HD), lambda h, i: (h, 0, 0))
    col = pl.BlockSpec((1, tq, 1), lambda h, i: (h, i, 0))
    rowv = pl.BlockSpec((1, LP // HD, HD), lambda h, i: (h, 0, 0))
    return call_with_exchange(
        body, name, (NH, nq),
        [pl.BlockSpec((tq, HD), lambda h, i: (i, cq + h)),
         pl.BlockSpec((LP, HD), lambda h, i: (0, ck + h)),
         pl.BlockSpec((LP, HD), lambda h, i: (0, cv + h)),
         gspec, col, rowv, blk, blk, col],
        (blk, full, full, gspec, rowv),
        (jax.ShapeDtypeStruct((LP, cfg.WG), MXU),) * 3
        + (jax.ShapeDtypeStruct((NH, 1, HD), F32), jax.ShapeDtypeStruct((NH, LP // HD, HD), F32)),
        [pltpu.VMEM((LP, HD), F32), pltpu.VMEM((LP, HD), F32)],
        (qkv, qkv, qkv, g3, c_col, c_row, o_pre, d_on, lse), xfer)


def _row_tile(rows, cols, n_arrays):
    budget = 24 * 1024 * 1024 // (2 * n_arrays * cols * 4)
    cap = max(8, min(rows, budget // 8 * 8))
    div = _tile(rows, cap, 8)
    return div if div <= cap and div * 4 >= cap else cap


def sum_slots(recvs, name):
    S, rows, cols = recvs[0].shape
    nl = len(recvs)
    tr = _tile(rows, _row_tile(rows, cols, nl * S + 1), 8)
    nb = rows // tr

    def body(*refs):
        o_ref = refs[nl]
        layer = pl.program_id(0)
        for ll in range(nl):
            @pl.when(layer == ll)
            def _(r_ref=refs[ll]):
                acc = r_ref[0].astype(F32)
                for s in range(1, S):
                    acc = acc + r_ref[s].astype(F32)
                o_ref[...] = acc

    def spec(ll):
        return pl.BlockSpec((S, tr, cols), lambda l, i: (0, jnp.where(l == ll, i, jnp.where(l < ll, 0, nb - 1)), 0))

    return pl.pallas_call(
        body, name=name, out_shape=jax.ShapeDtypeStruct((nl * rows, cols), F32), grid=(nl, nb),
        in_specs=[spec(ll) for ll in range(nl)],
        out_specs=pl.BlockSpec((tr, cols), lambda l, i: (l * nb + i, 0)),
        compiler_params=_cp(("arbitrary", "arbitrary")))(*recvs)


def adamw(w, m, v, g_parts, name):
    rows, cols = w.shape
    npart = len(g_parts)
    tr = _row_tile(rows, cols, 7 + npart)
    c1 = 1.0 - ADAM_B1 ** ADAM_STEP
    c2 = 1.0 - ADAM_B2 ** ADAM_STEP

    def body(*refs):
        w_ref, m_ref, v_ref = refs[:3]
        g_refs = refs[3:3 + npart]
        g_out, d_out, m_out, v_out = refs[3 + npart:]
        g = g_refs[0][...]
        for r in g_refs[1:]:
            g = g + r[...]
        g_out[...] = g
        mn = ADAM_B1 * m_ref[...] + (1.0 - ADAM_B1) * g
        vn = ADAM_B2 * v_ref[...] + (1.0 - ADAM_B2) * (g * g)
        m_out[...] = mn
        v_out[...] = vn
        d_out[...] = -ADAM_LR * ((mn / c1) / (jnp.sqrt(vn / c2) + ADAM_EPS) + ADAM_WD * w_ref[...])

    blk = pl.BlockSpec((tr, cols), lambda i: (i, 0))
    return pl.pallas_call(
        body, name=name, out_shape=(jax.ShapeDtypeStruct((rows, cols), F32),) * 4,
        grid=(pl.cdiv(rows, tr),), in_specs=[blk] * (3 + npart), out_specs=(blk,) * 4,
        compiler_params=_cp(("parallel",)))(w, m, v, *g_parts)


def _coords():
    return lax.axis_index("x"), lax.axis_index("y"), lax.axis_index("c")


def chip_exchange(arrs, scatter, name):
    n = len(arrs)

    def body(*refs):
        ins, outs, sems = refs[:n], refs[n:2 * n], refs[2 * n:]
        _xfer_start(ins, outs, sems, scatter)
        _xfer_forward(ins, outs, sems, scatter)
        _xfer_finish(ins, outs, sems, scatter)

    out_shape, scratch = _xfer_shapes(arrs, scatter)
    any_spec = pl.BlockSpec(memory_space=pl.ANY)
    return pl.pallas_call(body, name=name, out_shape=out_shape, in_specs=[any_spec] * n, out_specs=(any_spec,) * n,
                          scratch_shapes=scratch)(*arrs)


def _xfer_shapes(arrs, scatter):
    n = len(arrs)
    shapes = [a.shape[1:] if scatter else a.shape for a in arrs]
    out_shape = tuple(jax.ShapeDtypeStruct((NCHIP,) + tuple(s), a.dtype) for s, a in zip(shapes, arrs))
    scratch = [pltpu.SemaphoreType.DMA((n, 6)), pltpu.SemaphoreType.DMA((n, 6)), pltpu.SemaphoreType.DMA((n,))]
    return out_shape, scratch


def _xfer_copies(ins, outs, sems, scatter):
    send_sems, recv_sems, local_sems = sems
    n = len(ins)
    x, y, c = _coords()
    me = 2 * x + y
    peers = [(1 - x, y), (x, 1 - y), (1 - x, 1 - y)]

    def rdma(src, dst, a, j, dev):
        return pltpu.make_async_remote_copy(src_ref=src, dst_ref=dst, send_sem=send_sems.at[a, j],
                                            recv_sem=recv_sems.at[a, j], device_id=dev, device_id_type=MESH)

    def half(ref, hc):
        hr = ref.shape[0] // 2
        return ref.at[pl.ds(pl.multiple_of(hc * hr, 8), hr)]

    local, sends, recvs, fwds, fwd_recvs = [], [], [], [], []
    for a in range(n):
        local.append(pltpu.make_async_copy(ins[a].at[me] if scatter else ins[a], outs[a].at[me], local_sems.at[a]))
        for k, (px, py) in enumerate(peers):
            there = 2 * px + py
            if scatter:
                sends.append(rdma(ins[a].at[there], outs[a].at[me], a, k, (px, py, c)))
                recvs.append(rdma(ins[a].at[me], outs[a].at[there], a, k, (px, py, c)))
            else:
                mine, landed = half(outs[a].at[me], c), half(outs[a].at[there], c)
                sends.append(rdma(half(ins[a], c), mine, a, k, (px, py, c)))
                recvs.append(rdma(half(ins[a], c), landed, a, k, (px, py, c)))
                fwds.append(rdma(landed, landed, a, 3 + k, (x, y, 1 - c)))
                other = half(outs[a].at[there], 1 - c)
                fwd_recvs.append(rdma(other, other, a, 3 + k, (x, y, 1 - c)))
    return local, sends, recvs, fwds, fwd_recvs


def _xfer_start(ins, outs, sems, scatter):
    local, sends, _, _, _ = _xfer_copies(ins, outs, sems, scatter)
    for cp in local + sends:
        cp.start()


def _xfer_forward(ins, outs, sems, scatter):
    _, _, recvs, fwds, _ = _xfer_copies(ins, outs, sems, scatter)
    for r, f in zip(recvs, fwds):
        r.wait_recv()
        f.start()


def _xfer_finish(ins, outs, sems, scatter):
    local, sends, recvs, fwds, fwd_recvs = _xfer_copies(ins, outs, sems, scatter)
    for cp in (recvs if scatter else fwd_recvs):
        cp.wait_recv()
    for cp in sends + fwds:
        cp.wait_send()
    for cp in local:
        cp.wait()


def call_with_exchange(core, name, grid, in_specs, out_specs, out_shape, scratch, args, xfer):
    n_in, n_out, n_scr = len(in_specs), len(out_specs), len(scratch)
    if xfer is None:
        res = pl.pallas_call(core, name=name, out_shape=out_shape, grid=grid, in_specs=in_specs,
                             out_specs=out_specs, scratch_shapes=scratch,
                             compiler_params=_cp(("arbitrary",) * len(grid)))(*args)
        return res, ()
    arrs, scatter = xfer
    nx = len(arrs)
    x_shape, x_scratch = _xfer_shapes(arrs, scatter)

    def body(*refs):
        a, xi = refs[:n_in], refs[n_in:n_in + nx]
        o, xo = refs[n_in + nx:n_in + nx + n_out], refs[n_in + nx + n_out:n_in + 2 * nx + n_out]
        rest = refs[n_in + 2 * nx + n_out:]
        scr, sems = rest[:n_scr], rest[n_scr:]
        first, late, last = None, None, None
        for d, g in enumerate(grid):
            pid = pl.program_id(d)
            f, m, l = pid == 0, pid == ((3 * g) // 4 if d == 0 else 0), pid == g - 1
            first = f if first is None else first & f
            late = m if late is None else late & m
            last = l if last is None else last & l

        @pl.when(first)
        def _():
            _xfer_start(xi, xo, sems, scatter)

        core(*a, *o, *scr)

        if not scatter:
            @pl.when(late)
            def _():
                _xfer_forward(xi, xo, sems, scatter)

        @pl.when(last)
        def _():
            _xfer_finish(xi, xo, sems, scatter)

    any_spec = pl.BlockSpec(memory_space=pl.ANY)
    res = pl.pallas_call(
        body, name=name, out_shape=tuple(out_shape) + tuple(x_shape), grid=grid,
        in_specs=list(in_specs) + [any_spec] * nx, out_specs=tuple(out_specs) + (any_spec,) * nx,
        scratch_shapes=list(scratch) + x_scratch,
        compiler_params=_cp(("arbitrary",) * len(grid)))(*args, *arrs)
    return res[:n_out], res[n_out:]


def sibling_exchange(arrs, name):
    n = len(arrs)

    def body(*refs):
        ins, outs = refs[:n], refs[n:2 * n]
        send_sems, recv_sems = refs[2 * n:]
        x, y, c = _coords()
        cps = [pltpu.make_async_remote_copy(src_ref=ins[a], dst_ref=outs[a], send_sem=send_sems.at[a],
                                            recv_sem=recv_sems.at[a], device_id=(x, y, 1 - c),
                                            device_id_type=MESH) for a in range(n)]
        for cp in cps:
            cp.start()
        for cp in cps:
            cp.wait()

    any_spec = pl.BlockSpec(memory_space=pl.ANY)
    return pl.pallas_call(
        body, name=name, out_shape=tuple(jax.ShapeDtypeStruct(a.shape, a.dtype) for a in arrs),
        in_specs=[any_spec] * n, out_specs=(any_spec,) * n,
        scratch_shapes=[pltpu.SemaphoreType.DMA((n,)), pltpu.SemaphoreType.DMA((n,))],
    )(*arrs)


def all_reduce_small(pack, name):
    R = pack.shape[0]

    def body(p_ref, o_ref, slots, send_sems, recv_sems):
        x, y, c = _coords()
        me = 4 * x + 2 * y + c
        slots[me] = p_ref[...]
        cps = []
        for k in range(1, 8):
            kx, ky, kc = (k >> 2) & 1, (k >> 1) & 1, k & 1
            peer = ((1 - x) if kx else x, (1 - y) if ky else y, (1 - c) if kc else c)
            cp = pltpu.make_async_remote_copy(src_ref=p_ref, dst_ref=slots.at[me], send_sem=send_sems.at[k],
                                              recv_sem=recv_sems.at[k], device_id=peer, device_id_type=MESH)
            cp.start()
            cps.append((cp, peer))
        for k, (cp, peer) in enumerate(cps, start=1):
            frm = 4 * peer[0] + 2 * peer[1] + peer[2]
            pltpu.make_async_remote_copy(src_ref=p_ref, dst_ref=slots.at[frm], send_sem=send_sems.at[k],
                                         recv_sem=recv_sems.at[k], device_id=peer, device_id_type=MESH).wait_recv()
        for cp, _ in cps:
            cp.wait_send()
        acc = slots[0]
        for s in range(1, 8):
            acc = acc + slots[s]
        o_ref[...] = acc

    vm = pl.BlockSpec(memory_space=pltpu.VMEM)
    return pl.pallas_call(
        body, name=name, out_shape=jax.ShapeDtypeStruct((R, LANES), F32), in_specs=[vm], out_specs=vm,
        scratch_shapes=[pltpu.VMEM((8, R, LANES), F32), pltpu.SemaphoreType.DMA((8,)),
                        pltpu.SemaphoreType.DMA((8,))],
        compiler_params=pltpu.CompilerParams(vmem_limit_bytes=VMEM_LIMIT),
    )(pack)


def _gate_up(w):
    r, c = w.shape
    return w.reshape(r, 2, c // 2).transpose(1, 0, 2)


def _gate_up_inv(w):
    return w.transpose(1, 0, 2).reshape(w.shape[1], -1)


def _pack(arrs):
    parts = []
    for a in arrs:
        f = a.reshape(-1).astype(F32)
        parts.append(jnp.pad(f, (0, -f.shape[0] % 1024)).reshape(-1, LANES))
    return jnp.concatenate(parts, axis=0)


def _unpack(p, shapes):
    out, r = [], 0
    for s in shapes:
        n = math.prod(s)
        nr = (n + 1023) // 1024 * 8
        out.append(p[r:r + nr].reshape(-1)[:n].reshape(s))
        r += nr
    return out


def _vec(g):
    return g.reshape(1, -1)


class _LocalWeights:
    def __init__(self, cfg, wf):
        self.cfg, self.wf, self.grads = cfg, wf, {}

    def w_in(self, l):
        return self.wf['w_in'][l]

    def fwd_exchanges(self, l):
        return None, None

    def rest(self, l, got_sb, got_fox):
        w_up = self.wf['w_up'][l]
        return dict(w_out=self.wf['w_out'][l], w_down=self.wf['w_down'][l],
                    w_up=w_up.reshape(w_up.shape[0], NCHIP, -1).transpose(1, 0, 2))

    def bwd_exchanges(self, l, g):
        return None, None

    def in_dx_exchange(self, l, g):
        return None

    def bwd_done(self, l, g, got_sb, got_fox, got_in):
        self.grads[l] = g


def _w_in_parts(cfg, w_in):
    w_f = jnp.pad(w_in[:, cfg.NQKV:], ((0, 0), (0, LANES - cfg.NH)))
    return dict(w_qkv=w_in[:, :cfg.NQKV], w_f=w_f, w_in_ext=jnp.concatenate([w_in[:, :cfg.NQKV], w_f], axis=1))


def _layer_fwd(cfg, l, h, u, wl, io):
    tag = f"l{l}"
    qkv = matmul(u, wl['w_qkv'], 'nn', MXU, f"{tag}_qkv", tn_cap=1024)
    f_logit = matmul(u, wl['w_f'], 'nn', F32, f"{tag}_fproj")
    cpre = fox_gate_fwd(cfg, f_logit, wl['b_pad'], f"{tag}_gate_fwd")
    c_heads = cpre[:, :cfg.NH].T
    c_col = c_heads[:, :, None]
    c_row = c_heads.reshape(cfg.NH, cfg.LP // HD, HD)
    x_sb, x_fox = io.fwd_exchanges(l)
    (o_sb, on_sb, rtot), got_sb = sb_fwd(cfg, qkv, wl['g_sb'], f"{tag}_sb_fwd", x_sb)
    (o_fx, on_fx, lse), got_fox = fox_fwd(cfg, qkv, wl['g_fox'], c_col, c_row, f"{tag}_fox_fwd", x_fox)
    wl.update(io.rest(l, got_sb, got_fox))
    mixin = jnp.concatenate([on_sb, on_fx], axis=1)
    mix = matmul(mixin, wl['w_out'], 'nn', F32, f"{tag}_out")
    h1, u2 = resid_norm(cfg, mix, h, wl['g_mix_post'], wl['g_ffn_pre'], f"{tag}_mixres")
    a = matmul(u2, wl['w_up'], 'nn', F32, f"{tag}_up", tn_cap=1408, out_split=2)
    act = conv_act_fwd(cfg, a, wl['conv_w'], wl['conv_b'], f"{tag}_conv_fwd")
    ff = matmul(act, wl['w_down'], 'nn', F32, f"{tag}_down", tm_cap=704)
    h2, u_next = resid_norm(cfg, ff, h1, wl['g_ffn_post'], wl['g_next'], f"{tag}_ffnres")
    saved = dict(h=h, u=u, qkv=qkv, f_logit=f_logit, c_col=c_col, c_row=c_row, o_sb=o_sb, o_fx=o_fx, rtot=rtot,
                 lse=lse, mixin=mixin, mix=mix, h1=h1, u2=u2, a=a, act=act, ff=ff)
    return h2, u_next, saved


def _layer_bwd(cfg, l, dh2, wl, sv, io):
    tag = f"l{l}"
    g = {}
    d_ff, g['g_ffn_post'] = norm_bwd(cfg, sv['ff'], wl['g_ffn_post'], dh2, None, MXU, f"{tag}_ffnpost_bwd")
    d_act = matmul(d_ff, wl['w_down'], 'nt', F32, f"{tag}_down_dx")
    g['w_down'] = matmul(sv['act'], d_ff, 'tn', MXU, f"{tag}_down_dw", tm_cap=704, tk_cap=4224)
    d_a, d_conv = conv_act_bwd(cfg, sv['a'], d_act, wl['conv_w'], wl['conv_b'], f"{tag}_conv_bwd")
    g['conv'] = d_conv
    g['w_up'] = matmul(sv['u2'], d_a, 'tn', MXU, f"{tag}_up_dw", tm_cap=512, tn_cap=1408, tk_cap=4224,
                       out_split=NCHIP)
    du2 = matmul(d_a, wl['w_up'], 'nt', F32, f"{tag}_up_dx", tm_cap=704, tk_cap=5632)
    dh1, g['g_ffn_pre'] = norm_bwd(cfg, sv['h1'], wl['g_ffn_pre'], du2, dh2, F32, f"{tag}_ffnpre_bwd")
    d_mix, g['g_mix_post'] = norm_bwd(cfg, sv['mix'], wl['g_mix_post'], dh1, None, MXU, f"{tag}_mixpost_bwd")
    d_mixin = matmul(d_mix, wl['w_out'], 'nt', F32, f"{tag}_out_dx")
    g['w_out'] = matmul(sv['mixin'], d_mix, 'tn', MXU, f"{tag}_out_dw", tm_cap=1024, tk_cap=4224)
    WG = cfg.WG
    x_sb, x_fox = io.bwd_exchanges(l, g)
    (dq_s, dk_s, dv_s, g['g_sb']), got_sb = sb_bwd(cfg, sv['qkv'], wl['g_sb'], sv['o_sb'], d_mixin[:, :WG],
                                                   sv['rtot'], f"{tag}_sb_bwd", x_sb)
    (dq_f, dk_f, dv_f, g['g_fox'], dc_row), got_fox = fox_bwd(cfg, sv['qkv'], wl['g_fox'], sv['c_col'], sv['c_row'],
                                                              sv['o_fx'], d_mixin[:, WG:], sv['lse'],
                                                              f"{tag}_fox_bwd", x_fox)
    dc = jnp.pad(dc_row.reshape(cfg.NH, cfg.LP).T, ((0, 0), (0, LANES - cfg.NH)))
    d_f, g['b_f'] = fox_gate_bwd(cfg, dc, sv['f_logit'], wl['b_pad'], f"{tag}_gate_bwd")
    d_proj = jnp.concatenate([dq_s, dk_s, dv_s, dq_f, dk_f, dv_f, d_f], axis=1)
    g['w_in_ext'] = matmul(sv['u'], d_proj, 'tn', MXU, f"{tag}_in_dw", tm_cap=1024, tn_cap=896, tk_cap=4224)
    x_in = io.in_dx_exchange(l, g)
    du = matmul(d_proj, wl['w_in_ext'], 'nt', F32, f"{tag}_in_dx", tm_cap=704, xfer=x_in)
    du, got_in = du if x_in is not None else (du, None)
    dh, g['g_mix_pre'] = norm_bwd(cfg, sv['h'], wl['g_mix_pre'], du, dh1, F32, f"{tag}_mixpre_bwd")
    io.bwd_done(l, g, got_sb, got_fox, got_in)
    return dh


def _local_step(cfg, h0, target_p, wf, io=None):
    io = _LocalWeights(cfg, wf) if io is None else io
    layers = []
    for l in range(cfg.DEPTH):
        layers.append(dict(
            b_pad=jnp.pad(wf['b_f'][l], (0, LANES - cfg.NH)).reshape(1, LANES),
            g_sb=wf['g_sb'][l][:, None, :], g_fox=wf['g_fox'][l][:, None, :],
            conv_w=_gate_up(wf['conv_w'][l]), conv_b=wf['conv_b'][l].reshape(2, 1, cfg.F),
            g_mix_pre=_vec(wf['g_mix_pre'][l]), g_mix_post=_vec(wf['g_mix_post'][l]),
            g_ffn_pre=_vec(wf['g_ffn_pre'][l]), g_ffn_post=_vec(wf['g_ffn_post'][l]),
            g_next=_vec(wf['g_mix_pre'][(l + 1) % cfg.DEPTH])))
    h = h0
    u = pre_norm(cfg, h0, layers[0]['g_mix_pre'], "l0_prenorm")
    saved = []
    for l in range(cfg.DEPTH):
        layers[l].update(_w_in_parts(cfg, io.w_in(l)))
        h, u, sv = _layer_fwd(cfg, l, h, u, layers[l], io)
        saved.append(sv)
    dh, loss_blk = loss_head(cfg, h, target_p, "loss_head")
    for l in reversed(range(cfg.DEPTH)):
        dh = _layer_bwd(cfg, l, dh, layers[l], saved[l], io)
    return loss_blk, dh, [io.grads[l] for l in range(cfg.DEPTH)]


def _cols(g):
    return g.transpose(1, 0, 2).reshape(g.shape[1], -1)


def _rows(g):
    return g.reshape(-1, g.shape[2])


def _send_cols(g):
    r, c = g.shape
    return g.reshape(r, NCHIP, c // NCHIP).transpose(1, 0, 2).astype(MXU)


def _send_rows(g):
    r, c = g.shape
    return g.reshape(NCHIP, r // NCHIP, c).astype(MXU)


class _StreamedWeights:
    def __init__(self, cfg, w):
        self.cfg = cfg
        self.shard = {n: [w[n][l].astype(MXU) for l in range(cfg.DEPTH)] for n in BIG}
        self.grads, self.recv = {}, {n: [None] * cfg.DEPTH for n in BIG}
        self.win = {0: _cols(chip_exchange([self.shard['w_in'][0]], False, "gather_w_in0")[0])}
        self.pending = None

    def w_in(self, l):
        return self.win[l]

    def fwd_exchanges(self, l):
        s = self.shard
        fox = [s['w_down'][l]] + ([s['w_in'][l + 1]] if l + 1 < self.cfg.DEPTH else [])
        return ([s['w_out'][l], s['w_up'][l]], False), (fox, False)

    def rest(self, l, got_sb, got_fox):
        if l + 1 < self.cfg.DEPTH:
            self.win[l + 1] = _cols(got_fox[1])
        return dict(w_out=_rows(got_sb[0]), w_up=got_sb[1], w_down=_rows(got_fox[0]))

    def bwd_exchanges(self, l, g):
        sb = [g['w_up']] + ([] if self.pending is None else [self.pending])
        self.pending = None
        return (sb, True), ([_send_rows(g['w_down']), _send_rows(g['w_out'])], True)

    def in_dx_exchange(self, l, g):
        send = _send_cols(g['w_in_ext'][:, :self.cfg.N_IN])
        if l == 0:
            return [send], True
        self.pending = send
        return None

    def bwd_done(self, l, g, got_sb, got_fox, got_in):
        self.grads[l] = g
        self.recv['w_up'][l] = got_sb[0]
        if len(got_sb) > 1:
            self.recv['w_in'][l + 1] = got_sb[1]
        self.recv['w_down'][l], self.recv['w_out'][l] = got_fox
        if got_in is not None:
            self.recv['w_in'][l] = got_in[0]


def _step(cfg, x, w, target, m, v):
    xi, yi, ci = _coords()
    chip = 2 * xi + yi
    D, LP, L, NM = cfg.D, cfg.LP, cfg.L, cfg.NMETA
    DEP = cfg.DEPTH
    io = _StreamedWeights(cfg, w)
    wf = {}
    small_in = _pack([lax.dynamic_update_slice(jnp.zeros((NM, D), F32), w['meta'], (0, chip * (D // NCHIP))),
                      lax.dynamic_update_slice(jnp.zeros((DEP, 3, cfg.F2), F32), w['conv_w'],
                                               (0, 0, chip * (cfg.F2 // NCHIP)))])
    small_in = jnp.where(ci == 0, small_in, 0.0)
    meta_full, conv_w_full = _unpack(all_reduce_small(small_in, "gather_small"), [(NM, D), (DEP, 3, cfg.F2)])
    for n in SMALL:
        wf[n] = w[n]
    wf['conv_w'] = conv_w_full

    zpad = jnp.zeros((LP - L, D), F32)
    h0 = jnp.concatenate([meta_full, x[0], zpad], axis=0)
    target_p = jnp.concatenate([jnp.zeros((NM, D), F32), target[0], zpad], axis=0)
    loss_blk, dh0, grads = _local_step(cfg, h0, target_p, wf, io)

    def stack(key, shape):
        return jnp.stack([grads[l][key].reshape(shape) for l in range(DEP)])

    conv_g = jnp.stack([_gate_up_inv(grads[l]['conv']) for l in range(DEP)])
    small_g = dict(
        loss=loss_blk[0:1, 0:1], meta=dh0[:NM], g_mix_pre=stack('g_mix_pre', (D,)),
        b_f=stack('b_f', (LANES,))[:, :cfg.NH], g_sb=stack('g_sb', (cfg.NH, HD)), g_fox=stack('g_fox', (cfg.NH, HD)),
        g_mix_post=stack('g_mix_post', (D,)), g_ffn_pre=stack('g_ffn_pre', (D,)),
        conv_w=conv_g[:, 0:3], conv_b=conv_g[:, 3], g_ffn_post=stack('g_ffn_post', (D,)))
    keys = list(small_g)
    red = dict(zip(keys, _unpack(all_reduce_small(_pack([small_g[k] for k in keys]), "reduce_small"),
                                 [small_g[k].shape for k in keys])))
    loss = red['loss'].reshape(())
    red['meta'] = lax.dynamic_slice(red['meta'], (0, chip * (D // NCHIP)), (NM, D // NCHIP))
    red['conv_w'] = lax.dynamic_slice(red['conv_w'], (0, 0, chip * (cfg.F2 // NCHIP)), (DEP, 3, cfg.F2 // NCHIP))

    recv = io.recv
    part =[sum_slots(recv[n], f"sum_{n}") for n in BIG]
    other = sibling_exchange(part, "sibling_grads")

    outs = {}
    for n, p, q in zip(BIG, part, other):
        shp = w[n].shape
        s2 = (shp[0] * shp[1], shp[2])
        res = adamw(w[n].reshape(s2), m[n].reshape(s2), v[n].reshape(s2), [p, q], f"adamw_{n}")
        outs[n] = [r.reshape(shp) for r in res]

    shapes = [w[n].shape for n in SMALL]
    res = adamw(_pack([w[n] for n in SMALL]), _pack([m[n] for n in SMALL]), _pack([v[n] for n in SMALL]),
                [_pack([red[n] for n in SMALL])], "adamw_small")
    res = [_unpack(r, shapes) for r in res]
    for i, n in enumerate(SMALL):
        outs[n] = [res[k][i] for k in range(4)]

    grad_x = dh0[NM:L][None]
    return (loss, grad_x, *[outs[n][0] for n in WEIGHTS], *[outs[n][1] for n in WEIGHTS],
            *[outs[n][2] for n in WEIGHTS], *[outs[n][3] for n in WEIGHTS])


def kernel(x, meta, g_mix_pre, w_in, b_f, g_sb, g_fox, w_out, g_mix_post, g_ffn_pre, w_up, conv_w, conv_b, w_down, g_ffn_post, loss_target, m_meta, m_g_mix_pre, m_w_in, m_b_f, m_g_sb, m_g_fox, m_w_out, m_g_mix_post, m_g_ffn_pre, m_w_up, m_conv_w, m_conv_b, m_w_down, m_g_ffn_post, v_meta, v_g_mix_pre, v_w_in, v_b_f, v_g_sb, v_g_fox, v_w_out, v_g_mix_post, v_g_ffn_pre, v_w_up, v_conv_w, v_conv_b, v_w_down, v_g_ffn_post):
    w = dict(zip(WEIGHTS, (meta, g_mix_pre, w_in, b_f, g_sb, g_fox, w_out, g_mix_post, g_ffn_pre, w_up, conv_w,
                           conv_b, w_down, g_ffn_post)))
    m = dict(zip(WEIGHTS, (m_meta, m_g_mix_pre, m_w_in, m_b_f, m_g_sb, m_g_fox, m_w_out, m_g_mix_post, m_g_ffn_pre,
                           m_w_up, m_conv_w, m_conv_b, m_w_down, m_g_ffn_post)))
    v = dict(zip(WEIGHTS, (v_meta, v_g_mix_pre, v_w_in, v_b_f, v_g_sb, v_g_fox, v_w_out, v_g_mix_post, v_g_ffn_pre,
                           v_w_up, v_conv_w, v_conv_b, v_w_down, v_g_ffn_post)))
    return _step(PROD, x, w, loss_target, m, v)
```

```python
import functools
import math
from typing import NamedTuple

import jax
import jax.numpy as jnp
from jax import lax
from jax.experimental import pallas as pl
from jax.experimental.pallas import tpu as pltpu

F32 = jnp.float32
MXU = jnp.bfloat16
HD = 128
LANES = 128
EPS = 1e-6
NEG = -1e30
LK_PIECES = 2
ADAM_LR, ADAM_B1, ADAM_B2, ADAM_EPS, ADAM_WD, ADAM_STEP = 0.001, 0.9, 0.999, 1e-08, 0.01, 10
VMEM_LIMIT = 56 * 1024 * 1024
MESH = pl.DeviceIdType.MESH
NCHIP = 4

WEIGHTS = ['meta', 'g_mix_pre', 'w_in', 'b_f', 'g_sb', 'g_fox', 'w_out', 'g_mix_post', 'g_ffn_pre',
           'w_up', 'conv_w', 'conv_b', 'w_down', 'g_ffn_post']
BIG = ['w_in', 'w_out', 'w_up', 'w_down']
SMALL = [n for n in WEIGHTS if n not in BIG]

NT = (((1,), (1,)), ((), ()))
TN = (((0,), (0,)), ((), ()))
NN = (((1,), (0,)), ((), ()))


class Cfg(NamedTuple):
    D: int
    SEQ: int
    NMETA: int
    NH: int
    F: int
    LP: int
    tq: int
    tr: int
    FB: int
    DEPTH: int = 2

    @property
    def L(self): return self.SEQ + self.NMETA
    @property
    def WG(self): return self.NH * HD
    @property
    def WMIX(self): return 2 * self.WG
    @property
    def NQKV(self): return 6 * self.WG
    @property
    def N_IN(self): return self.NQKV + self.NH
    @property
    def NEXT(self): return self.NQKV + LANES
    @property
    def F2(self): return 2 * self.F


PROD = Cfg(D=2048, SEQ=4096, NMETA=16, NH=8, F=5632, LP=4224, tq=384, tr=192, FB=512)


def _tile(n, cap, mult=LANES):
    best = None
    for t in range(mult, min(n, cap) + 1, mult):
        if n % t == 0:
            best = t
    return best if best is not None else n


def _cp(sem):
    return pltpu.CompilerParams(dimension_semantics=sem, vmem_limit_bytes=VMEM_LIMIT)


def _split_dot(x, tri, pieces):
    acc, r = None, x
    for p in range(pieces):
        xp = r.astype(MXU)
        d = jnp.dot(xp, tri, preferred_element_type=F32)
        acc = d if acc is None else acc + d
        if p + 1 < pieces:
            r = r - xp.astype(F32)
    return acc


def _split_dot_left(tri, x, pieces):
    acc, r = None, x
    for p in range(pieces):
        xp = r.astype(MXU)
        d = jnp.dot(tri, xp, preferred_element_type=F32)
        acc = d if acc is None else acc + d
        if p + 1 < pieces:
            r = r - xp.astype(F32)
    return acc


def _softplus(z):
    return jnp.maximum(z, 0.0) + jnp.log(1.0 + jnp.exp(-jnp.abs(z)))


def _shape2(x):
    return tuple(x.shape) if x.ndim == 2 else (x.shape[1], x.shape[0] * x.shape[2])


def _split_width(x):
    return x.shape[-1]


def _spec2(x, rb, cb, idx):
    if len(x.shape) == 2:
        return pl.BlockSpec((rb, cb), idx)
    per = x.shape[2] // cb

    def im(i, j, k):
        ri, ci = idx(i, j, k)
        return ci // per, ri, ci % per

    return pl.BlockSpec((None, rb, cb), im)


def matmul(a, b, mode, out_dtype, name, tm_cap=1408, tn_cap=1024, tk_cap=8192, out_split=None, xfer=None):
    (K, M) = _shape2(a) if mode == 'tn' else _shape2(a)[::-1]
    N = _shape2(b)[0] if mode == 'nt' else _shape2(b)[1]
    n_unit = math.gcd(N // (out_split or 1), _split_width(b) if mode != 'nt' else N)
    k_unit = math.gcd(_split_width(a) if mode != 'tn' else K, _split_width(b) if mode == 'nt' else K)
    tm = _tile(M, tm_cap, LANES if mode == 'tn' else 8)
    tn, tk = _tile(n_unit, tn_cap), _tile(k_unit, tk_cap)
    nk = K // tk
    dn = {'nn': NN, 'nt': NT, 'tn': TN}[mode]

    def body(a_ref, b_ref, o_ref, *scratch):
        d = lax.dot_general(a_ref[...], b_ref[...], dn, preferred_element_type=F32)
        if nk == 1:
            o_ref[...] = d.astype(out_dtype)
        else:
            acc_ref, = scratch
            k = pl.program_id(2)

            @pl.when(k == 0)
            def _():
                acc_ref[...] = d

            @pl.when(k > 0)
            def _():
                acc_ref[...] += d

            @pl.when(k == nk - 1)
            def _():
                o_ref[...] = acc_ref[...].astype(out_dtype)

    a_spec = (_spec2(a, tk, tm, lambda i, j, k: (k, i)) if mode == 'tn'
              else _spec2(a, tm, tk, lambda i, j, k: (i, k)))
    b_spec = (_spec2(b, tn, tk, lambda i, j, k: (j, k)) if mode == 'nt'
              else _spec2(b, tk, tn, lambda i, j, k: (k, j)))
    out = (jax.ShapeDtypeStruct((M, N), out_dtype) if out_split is None
           else jax.ShapeDtypeStruct((out_split, M, N // out_split), out_dtype))
    (res,), got = call_with_exchange(
        body, name, (M // tm, N // tn, nk), [a_spec, b_spec], (_spec2(out, tm, tn, lambda i, j, k: (i, j)),),
        (out,), [] if nk == 1 else [pltpu.VMEM((tm, tn), F32)], (a, b), xfer)
    return res if xfer is None else (res, got)


def _rstd(x):
    return lax.rsqrt(jnp.mean(x * x, axis=-1, keepdims=True) + EPS)


def pre_norm(cfg, h, g, name):
    LP, D, tr = cfg.LP, cfg.D, cfg.tr

    def body(h_ref, g_ref, u_ref):
        x = h_ref[...]
        u_ref[...] = ((x * _rstd(x)) * g_ref[...]).astype(MXU)

    row = pl.BlockSpec((tr, D), lambda i: (i, 0))
    vec = pl.BlockSpec((1, D), lambda i: (0, 0))
    return pl.pallas_call(body, name=name, out_shape=jax.ShapeDtypeStruct((LP, D), MXU), grid=(LP // tr,),
                          in_specs=[row, vec], out_specs=row, compiler_params=_cp(("parallel",)))(h, g)


def resid_norm(cfg, y, h, g_post, g_next, name):
    LP, D, tr = cfg.LP, cfg.D, cfg.tr

    def body(y_ref, h_ref, gp_ref, gn_ref, hn_ref, u_ref):
        yv = y_ref[...]
        hn = h_ref[...] + (yv * _rstd(yv)) * gp_ref[...]
        hn_ref[...] = hn
        u_ref[...] = ((hn * _rstd(hn)) * gn_ref[...]).astype(MXU)

    row = pl.BlockSpec((tr, D), lambda i: (i, 0))
    vec = pl.BlockSpec((1, D), lambda i: (0, 0))
    return pl.pallas_call(
        body, name=name,
        out_shape=(jax.ShapeDtypeStruct((LP, D), F32), jax.ShapeDtypeStruct((LP, D), MXU)),
        grid=(LP // tr,), in_specs=[row, row, vec, vec], out_specs=(row, row),
        compiler_params=_cp(("parallel",)))(y, h, g_post, g_next)


def loss_head(cfg, h, target, name):
    LP, D, tr = cfg.LP, cfg.D, cfg.tr
    lo, hi = cfg.NMETA, cfg.L

    def body(h_ref, t_ref, dh_ref, loss_ref):
        i = pl.program_id(0)
        rows = lax.broadcasted_iota(jnp.int32, (tr, D), 0) + i * tr
        diff = jnp.where((rows >= lo) & (rows < hi), h_ref[...] - t_ref[...], 0.0)
        dh_ref[...] = diff * (1.0 / D)
        part = 0.5 * jnp.sum(jnp.sum(diff * diff, axis=1, keepdims=True), axis=0, keepdims=True) * (1.0 / D)

        @pl.when(i == 0)
        def _():
            loss_ref[...] = jnp.zeros_like(loss_ref)

        loss_ref[...] += jnp.broadcast_to(part, loss_ref.shape)

    row = pl.BlockSpec((tr, D), lambda i: (i, 0))
    return pl.pallas_call(
        body, name=name,
        out_shape=(jax.ShapeDtypeStruct((LP, D), F32), jax.ShapeDtypeStruct((8, LANES), F32)),
        grid=(LP // tr,), in_specs=[row, row],
        out_specs=(row, pl.BlockSpec((8, LANES), lambda i: (0, 0))),
        compiler_params=_cp(("arbitrary",)))(h, target)


def norm_bwd(cfg, x, g, dy, dres, out_dtype, name):
    LP, D, tr = cfg.LP, cfg.D, cfg.tr
    has_res = dres is not None

    def body(*refs):
        if has_res:
            x_ref, g_ref, dy_ref, dres_ref, dx_ref, dg_ref = refs
        else:
            x_ref, g_ref, dy_ref, dx_ref, dg_ref = refs
        xv, dyv = x_ref[...], dy_ref[...]
        r = _rstd(xv)
        xhat = xv * r
        gdy = dyv * g_ref[...]
        dx = r * (gdy - xhat * jnp.mean(gdy * xhat, axis=-1, keepdims=True))
        if has_res:
            dx = dx + dres_ref[...]
        dx_ref[...] = dx.astype(out_dtype)

        @pl.when(pl.program_id(0) == 0)
        def _():
            dg_ref[...] = jnp.zeros_like(dg_ref)

        dg_ref[...] += jnp.sum(dyv * xhat, axis=0, keepdims=True)

    row = pl.BlockSpec((tr, D), lambda i: (i, 0))
    vec = pl.BlockSpec((1, D), lambda i: (0, 0))
    ins = [x, g, dy] + ([dres] if has_res else [])
    return pl.pallas_call(
        body, name=name,
        out_shape=(jax.ShapeDtypeStruct((LP, D), out_dtype), jax.ShapeDtypeStruct((1, D), F32)),
        grid=(LP // tr,), in_specs=[row, vec, row] + ([row] if has_res else []), out_specs=(row, vec),
        compiler_params=_cp(("arbitrary",)))(*ins)


HALO = 8


def _conv3(ext, w, b):
    s1, s2 = pltpu.roll(ext, 1, 0), pltpu.roll(ext, 2, 0)
    return w[0:1, :] * s2 + w[1:2, :] * s1 + w[2:3, :] * ext + b, s1, s2


def conv_act_fwd(cfg, a, cw, cb, name):
    LP, F, FB, tm = cfg.LP, cfg.F, cfg.FB, cfg.tr
    nb = tm // HALO

    def body(a_ref, prev_ref, w_ref, b_ref, act_ref):
        i = pl.program_id(1)
        c = []
        for hh in range(2):
            prev = jnp.where(i > 0, prev_ref[hh], 0.0)
            ext = jnp.concatenate([prev, a_ref[hh]], axis=0)
            c.append(_conv3(ext, w_ref[hh], b_ref[hh])[0][HALO:, :])
        act_ref[...] = (c[0] * jax.nn.sigmoid(c[0]) * c[1]).astype(MXU)

    return pl.pallas_call(
        body, name=name, out_shape=jax.ShapeDtypeStruct((LP, F), MXU), grid=(F // FB, LP // tm),
        in_specs=[pl.BlockSpec((2, tm, FB), lambda j, i: (0, i, j)),
                  pl.BlockSpec((2, HALO, FB), lambda j, i: (0, jnp.maximum(i * nb - 1, 0), j)),
                  pl.BlockSpec((2, 3, FB), lambda j, i: (0, 0, j)),
                  pl.BlockSpec((2, 1, FB), lambda j, i: (0, 0, j))],
        out_specs=pl.BlockSpec((tm, FB), lambda j, i: (i, j)),
        compiler_params=_cp(("parallel", "parallel")))(a, a, cw, cb)


def conv_act_bwd(cfg, a, d_act, cw, cb, name):
    LP, F, FB, tm = cfg.LP, cfg.F, cfg.FB, cfg.tr
    nb, last = tm // HALO, LP // tm - 1
    nrow = LP // HALO
    n = tm + 2 * HALO

    def body(a_ref, aprev_ref, anext_ref, d_ref, dnext_ref, w_ref, b_ref, da_ref, dcv_ref):
        i = pl.program_id(1)

        @pl.when(i == 0)
        def _():
            dcv_ref[...] = jnp.zeros_like(dcv_ref)

        c, exts = [], []
        for hh in range(2):
            prev = jnp.where(i > 0, aprev_ref[hh], 0.0)
            ext = jnp.concatenate([prev, a_ref[hh], anext_ref[hh]], axis=0)
            chh, s1, s2 = _conv3(ext, w_ref[hh], b_ref[hh])
            c.append(chh)
            exts.append((ext, s1, s2))
        dnext = jnp.where(i < last, dnext_ref[...], 0.0)
        dact = jnp.concatenate([jnp.zeros((HALO, FB), F32), d_ref[...], dnext], axis=0)
        sg = jax.nn.sigmoid(c[0])
        d_c = [dact * c[1] * (sg * (1.0 + c[0] * (1.0 - sg))), dact * (c[0] * sg)]
        for hh in range(2):
            dc, w = d_c[hh], w_ref[hh]
            da = w[2:3, :] * dc + w[1:2, :] * pltpu.roll(dc, n - 1, 0) + w[0:1, :] * pltpu.roll(dc, n - 2, 0)
            da_ref[hh] = da[HALO:HALO + tm, :].astype(MXU)
            dcb = dc[HALO:HALO + tm, :]
            ext, s1, s2 = exts[hh]
            dcv_ref[hh, 0:1, :] += jnp.sum(dcb * s2[HALO:HALO + tm, :], axis=0, keepdims=True)
            dcv_ref[hh, 1:2, :] += jnp.sum(dcb * s1[HALO:HALO + tm, :], axis=0, keepdims=True)
            dcv_ref[hh, 2:3, :] += jnp.sum(dcb * ext[HALO:HALO + tm, :], axis=0, keepdims=True)
            dcv_ref[hh, 3:4, :] += jnp.sum(dcb, axis=0, keepdims=True)

    return pl.pallas_call(
        body, name=name,
        out_shape=(jax.ShapeDtypeStruct((2, LP, F), MXU), jax.ShapeDtypeStruct((2, 8, F), F32)),
        grid=(F // FB, LP // tm),
        in_specs=[pl.BlockSpec((2, tm, FB), lambda j, i: (0, i, j)),
                  pl.BlockSpec((2, HALO, FB), lambda j, i: (0, jnp.maximum(i * nb - 1, 0), j)),
                  pl.BlockSpec((2, HALO, FB), lambda j, i: (0, jnp.minimum((i + 1) * nb, nrow - 1), j)),
                  pl.BlockSpec((tm, FB), lambda j, i: (i, j)),
                  pl.BlockSpec((HALO, FB), lambda j, i: (jnp.minimum((i + 1) * nb, nrow - 1), j)),
                  pl.BlockSpec((2, 3, FB), lambda j, i: (0, 0, j)),
                  pl.BlockSpec((2, 1, FB), lambda j, i: (0, 0, j))],
        out_specs=(pl.BlockSpec((2, tm, FB), lambda j, i: (0, i, j)),
                   pl.BlockSpec((2, 8, FB), lambda j, i: (0, 0, j))),
        compiler_params=_cp(("parallel", "arbitrary")))(a, a, a, d_act, d_act, cw, cb)


def fox_gate_fwd(cfg, f_logit, b_pad, name):
    LP, tb = cfg.LP, cfg.tq

    def body(f_ref, b_ref, c_ref, carry_ref):
        @pl.when(pl.program_id(0) == 0)
        def _():
            carry_ref[...] = jnp.zeros_like(carry_ref)

        xv = f_ref[...] + b_ref[...]
        lf = -_softplus(-xv)
        r = lax.broadcasted_iota(jnp.int32, (tb, tb), 0)
        s = lax.broadcasted_iota(jnp.int32, (tb, tb), 1)
        c = _split_dot_left((s <= r).astype(MXU), lf, 3) + carry_ref[0:1, :]
        c_ref[...] = c
        carry_ref[0:1, :] = c[tb - 1:tb, :]

    blk = pl.BlockSpec((tb, LANES), lambda i: (i, 0))
    return pl.pallas_call(
        body, name=name, out_shape=jax.ShapeDtypeStruct((LP, LANES), F32), grid=(LP // tb,),
        in_specs=[blk, pl.BlockSpec((1, LANES), lambda i: (0, 0))], out_specs=blk,
        scratch_shapes=[pltpu.VMEM((8, LANES), F32)], compiler_params=_cp(("arbitrary",)))(f_logit, b_pad)


def fox_gate_bwd(cfg, dc, f_logit, b_pad, name):
    LP, tb = cfg.LP, cfg.tq
    nblk = LP // tb

    def body(dc_ref, f_ref, b_ref, df_ref, db_ref, carry_ref):
        @pl.when(pl.program_id(0) == 0)
        def _():
            carry_ref[...] = jnp.zeros_like(carry_ref)
            db_ref[...] = jnp.zeros_like(db_ref)

        r = lax.broadcasted_iota(jnp.int32, (tb, tb), 0)
        s = lax.broadcasted_iota(jnp.int32, (tb, tb), 1)
        dlf = _split_dot_left((s >= r).astype(MXU), dc_ref[...], 3) + carry_ref[0:1, :]
        carry_ref[0:1, :] = dlf[0:1, :]
        df = dlf * jax.nn.sigmoid(-(f_ref[...] + b_ref[...]))
        df_ref[...] = df.astype(MXU)
        db_ref[...] += jnp.sum(df, axis=0, keepdims=True)

    blk = pl.BlockSpec((tb, LANES), lambda i: (nblk - 1 - i, 0))
    vec = pl.BlockSpec((1, LANES), lambda i: (0, 0))
    return pl.pallas_call(
        body, name=name,
        out_shape=(jax.ShapeDtypeStruct((LP, LANES), MXU), jax.ShapeDtypeStruct((1, LANES), F32)),
        grid=(nblk,), in_specs=[blk, blk, vec], out_specs=(blk, vec),
        scratch_shapes=[pltpu.VMEM((8, LANES), F32)], compiler_params=_cp(("arbitrary",)))(dc, f_logit, b_pad)


def _head_norm_fwd(o, g):
    return (o * lax.rsqrt(jnp.mean(o * o, axis=-1, keepdims=True) + EPS)) * g


def _head_norm_bwd(o, g, d_on):
    r = lax.rsqrt(jnp.mean(o * o, axis=-1, keepdims=True) + EPS)
    ohat = o * r
    gdy = d_on * g
    d_o = r * (gdy - ohat * jnp.mean(gdy * ohat, axis=-1, keepdims=True))
    return d_o, jnp.sum(d_on * ohat, axis=0, keepdims=True)


def _diag_masks(tq, strict):
    rows = lax.broadcasted_iota(jnp.int32, (tq, HD), 0)
    cols = lax.broadcasted_iota(jnp.int32, (tq, HD), 1)
    return [(cols + kk * HD < rows) if strict else (cols + kk * HD <= rows) for kk in range(tq // HD)]


def _walk_groups(i, R, group, carry, masks, descending):
    big = 2 * R
    if descending:
        carry = group(i * R, R, carry, masks)
        carry = lax.fori_loop(0, i % 2, lambda t, c: group((i - 1) * R, R, c, None), carry)
        return lax.fori_loop(0, i // 2, lambda t, c: group((i // 2 - 1 - t) * big, big, c, None), carry)
    carry = lax.fori_loop(0, i // 2, lambda t, c: group(t * big, big, c, None), carry)
    carry = lax.fori_loop(0, i % 2, lambda t, c: group((i // 2) * big, R, c, None), carry)
    return group(i * R, R, carry, masks)


def _tri(pred):
    a = lax.broadcasted_iota(jnp.int32, (HD, HD), 0)
    b = lax.broadcasted_iota(jnp.int32, (HD, HD), 1)
    return pred(a, b).astype(MXU)


def _attn_specs(cfg, group):
    base = 3 * cfg.NH * group
    return base, base + cfg.NH, base + 2 * cfg.NH


def sb_fwd(cfg, qkv, g3, name, xfer=None):
    LP, NH, tq = cfg.LP, cfg.NH, cfg.tq
    nq, R = LP // tq, tq // HD
    scale = HD ** -0.5
    cq, ck, cv = _attn_specs(cfg, 0)

    def body(q_ref, k_ref, v_ref, g_ref, opre_ref, on_ref, rtot_ref):
        i = pl.program_id(1)
        q = q_ref[...]
        masks = _diag_masks(tq, True)
        m_after = _tri(lambda a, b: a > b)

        def group(j0, n, carry, mk):
            acc, rc = carry
            masked = mk is not None
            order = range(n - 1, -1, -1)
            offs = [pl.multiple_of((j0 + kk) * HD, HD) for kk in range(n)]
            zs = [lax.dot_general(q, k_ref[pl.ds(offs[kk], HD), :], NT, preferred_element_type=F32) * scale
                  for kk in range(n)]
            ls, lks, tris = [None] * n, [None] * n, [None] * n
            for kk in order:
                sp = _softplus(zs[kk])
                lks[kk] = jnp.where(mk[kk], -sp, 0.0) if masked else -sp
                tris[kk] = _split_dot(lks[kk], m_after, LK_PIECES)
                ls[kk] = zs[kk] - sp
            for kk in order:
                a = jnp.exp(ls[kk] + (tris[kk] + rc))
                if masked:
                    a = jnp.where(mk[kk], a, 0.0)
                acc = acc + jnp.dot(a.astype(MXU), v_ref[pl.ds(offs[kk], HD), :], preferred_element_type=F32)
                rc = rc + jnp.sum(lks[kk], axis=1, keepdims=True)
            return acc, rc

        acc, rc = _walk_groups(i, R, group, (jnp.zeros((tq, HD), F32), jnp.zeros((tq, 1), F32)), masks, True)
        opre_ref[...] = acc
        on_ref[...] = _head_norm_fwd(acc, g_ref[0]).astype(MXU)
        rtot_ref[0] = rc

    return call_with_exchange(
        body, name, (NH, nq),
        [pl.BlockSpec((tq, HD), lambda h, i: (i, cq + h)),
         pl.BlockSpec((LP, HD), lambda h, i: (0, ck + h)),
         pl.BlockSpec((LP, HD), lambda h, i: (0, cv + h)),
         pl.BlockSpec((1, 1, HD), lambda h, i: (h, 0, 0))],
        (pl.BlockSpec((tq, HD), lambda h, i: (i, h)),
         pl.BlockSpec((tq, HD), lambda h, i: (i, h)),
         pl.BlockSpec((1, tq, 1), lambda h, i: (h, i, 0))),
        (jax.ShapeDtypeStruct((LP, cfg.WG), F32), jax.ShapeDtypeStruct((LP, cfg.WG), MXU),
         jax.ShapeDtypeStruct((NH, LP, 1), F32)),
        [], (qkv, qkv, qkv, g3), xfer)


def sb_bwd(cfg, qkv, g3, o_pre, d_on, rtot, name, xfer=None):
    LP, NH, tq = cfg.LP, cfg.NH, cfg.tq
    nq, R = LP // tq, tq // HD
    scale = HD ** -0.5
    cq, ck, cv = _attn_specs(cfg, 0)

    def body(q_ref, k_ref, v_ref, g_ref, o_ref, don_ref, rtot_ref, dq_ref, dk_ref, dv_ref, dg_ref,
             dk_acc, dv_acc):
        i = pl.program_id(1)

        @pl.when(i == 0)
        def _():
            dk_acc[...] = jnp.zeros_like(dk_acc)
            dv_acc[...] = jnp.zeros_like(dv_acc)
            dg_ref[...] = jnp.zeros_like(dg_ref)

        q = q_ref[...]
        d_o, dg = _head_norm_bwd(o_ref[...], g_ref[0], don_ref[...])
        dg_ref[0] += dg
        do_b = d_o.astype(MXU)
        rt = rtot_ref[0]
        masks = _diag_masks(tq, True)
        m_le = _tri(lambda a, b: a <= b)
        m_lt = _tri(lambda a, b: a < b)

        def group(j0, n, carry, mk):
            dq, lc, pc = carry
            masked = mk is not None
            offs = [pl.multiple_of((j0 + kk) * HD, HD) for kk in range(n)]
            zs = [lax.dot_general(q, k_ref[pl.ds(offs[kk], HD), :], NT, preferred_element_type=F32) * scale
                  for kk in range(n)]
            das = [lax.dot_general(do_b, v_ref[pl.ds(offs[kk], HD), :], NT, preferred_element_type=F32)
                   for kk in range(n)]
            ls, lks, tri1 = [], [], []
            for kk in range(n):
                sp = _softplus(zs[kk])
                lk = jnp.where(mk[kk], -sp, 0.0) if masked else -sp
                tri1.append(_split_dot(lk, m_le, LK_PIECES))
                lks.append(lk)
                ls.append(zs[kk] - sp)
            ggs, a_bs, tri2 = [], [], []
            for kk in range(n):
                a = jnp.exp(ls[kk] + ((rt - lc) - tri1[kk]))
                if masked:
                    a = jnp.where(mk[kk], a, 0.0)
                gg = a * das[kk]
                tri2.append(_split_dot(gg, m_lt, 2))
                ggs.append(gg)
                a_bs.append(a.astype(MXU))
                lc = lc + jnp.sum(lks[kk], axis=1, keepdims=True)
            for kk in range(n):
                sig = jnp.exp(ls[kk])
                dz = (ggs[kk] * (1.0 - sig) - sig * (pc + tri2[kk])) * scale
                if masked:
                    dz = jnp.where(mk[kk], dz, 0.0)
                dz_b = dz.astype(MXU)
                dq = dq + jnp.dot(dz_b, k_ref[pl.ds(offs[kk], HD), :], preferred_element_type=F32)
                dk_acc[pl.ds(offs[kk], HD), :] += lax.dot_general(dz_b, q, TN, preferred_element_type=F32)
                dv_acc[pl.ds(offs[kk], HD), :] += lax.dot_general(a_bs[kk], do_b, TN, preferred_element_type=F32)
                pc = pc + jnp.sum(ggs[kk], axis=1, keepdims=True)
            return dq, lc, pc

        zc = jnp.zeros((tq, 1), F32)
        dq, _, _ = _walk_groups(i, R, group, (jnp.zeros((tq, HD), F32), zc, zc), masks, False)
        dq_ref[...] = dq.astype(MXU)

        @pl.when(i == nq - 1)
        def _():
            dk_ref[...] = dk_acc[...].astype(MXU)
            dv_ref[...] = dv_acc[...].astype(MXU)

    blk = pl.BlockSpec((tq, HD), lambda h, i: (i, h))
    full = pl.BlockSpec((LP, HD), lambda h, i: (0, h))
    gspec = pl.BlockSpec((1, 1, HD), lambda h, i: (h, 0, 0))
    return call_with_exchange(
        body, name, (NH, nq),
        [pl.BlockSpec((tq, HD), lambda h, i: (i, cq + h)),
         pl.BlockSpec((LP, HD), lambda h, i: (0, ck + h)),
         pl.BlockSpec((LP, HD), lambda h, i: (0, cv + h)),
         gspec, blk, blk, pl.BlockSpec((1, tq, 1), lambda h, i: (h, i, 0))],
        (blk, full, full, gspec),
        (jax.ShapeDtypeStruct((LP, cfg.WG), MXU),) * 3 + (jax.ShapeDtypeStruct((NH, 1, HD), F32),),
        [pltpu.VMEM((LP, HD), F32), pltpu.VMEM((LP, HD), F32)],
        (qkv, qkv, qkv, g3, o_pre, d_on, rtot), xfer)


def fox_fwd(cfg, qkv, g3, c_col, c_row, name, xfer=None):
    LP, NH, tq = cfg.LP, cfg.NH, cfg.tq
    nq, R = LP // tq, tq // HD
    scale = HD ** -0.5
    cq, ck, cv = _attn_specs(cfg, 1)

    def body(q_ref, k_ref, v_ref, g_ref, ccol_ref, crow_ref, opre_ref, on_ref, lse_ref):
        i = pl.program_id(1)
        q = q_ref[...]
        cq_b = jnp.broadcast_to(ccol_ref[0], (tq, HD))
        masks = _diag_masks(tq, False)

        def group(j0, n, carry, mk):
            acc, m, l = carry
            ss, offs = [], []
            for kk in range(n):
                j = j0 + kk
                off = pl.multiple_of(j * HD, HD)
                s = (lax.dot_general(q, k_ref[pl.ds(off, HD), :], NT, preferred_element_type=F32) * scale
                     + (cq_b - crow_ref[0, pl.ds(j, 1), :]))
                ss.append(s if mk is None else jnp.where(mk[kk], s, NEG))
                offs.append(off)
            mx = jnp.max(ss[0], axis=1, keepdims=True)
            for s in ss[1:]:
                mx = jnp.maximum(mx, jnp.max(s, axis=1, keepdims=True))
            m_new = jnp.maximum(m, mx)
            alpha = jnp.exp(m - m_new)
            acc, l = alpha * acc, alpha * l
            m_b = jnp.broadcast_to(m_new, (tq, HD))
            for s, off in zip(ss, offs):
                p = jnp.exp(s - m_b)
                l = l + jnp.sum(p, axis=1, keepdims=True)
                acc = acc + _split_dot(p, v_ref[pl.ds(off, HD), :], 2)
            return acc, m_new, l

        carry = (jnp.zeros((tq, HD), F32), jnp.full((tq, 1), NEG, F32), jnp.zeros((tq, 1), F32))
        acc, m, l = _walk_groups(i, R, group, carry, masks, False)
        o = acc / l
        opre_ref[...] = o
        on_ref[...] = _head_norm_fwd(o, g_ref[0]).astype(MXU)
        lse_ref[0] = m + jnp.log(l)

    return call_with_exchange(
        body, name, (NH, nq),
        [pl.BlockSpec((tq, HD), lambda h, i: (i, cq + h)),
         pl.BlockSpec((LP, HD), lambda h, i: (0, ck + h)),
         pl.BlockSpec((LP, HD), lambda h, i: (0, cv + h)),
         pl.BlockSpec((1, 1, HD), lambda h, i: (h, 0, 0)),
         pl.BlockSpec((1, tq, 1), lambda h, i: (h, i, 0)),
         pl.BlockSpec((1, LP // HD, HD), lambda h, i: (h, 0, 0))],
        (pl.BlockSpec((tq, HD), lambda h, i: (i, h)),
         pl.BlockSpec((tq, HD), lambda h, i: (i, h)),
         pl.BlockSpec((1, tq, 1), lambda h, i: (h, i, 0))),
        (jax.ShapeDtypeStruct((LP, cfg.WG), F32), jax.ShapeDtypeStruct((LP, cfg.WG), MXU),
         jax.ShapeDtypeStruct((NH, LP, 1), F32)),
        [], (qkv, qkv, qkv, g3, c_col, c_row), xfer)


def fox_bwd(cfg, qkv, g3, c_col, c_row, o_pre, d_on, lse, name, xfer=None):
    LP, NH, tq = cfg.LP, cfg.NH, cfg.tq
    nq, R = LP // tq, tq // HD
    scale = HD ** -0.5
    cq, ck, cv = _attn_specs(cfg, 1)

    def body(q_ref, k_ref, v_ref, g_ref, ccol_ref, crow_ref, o_ref, don_ref, lse_ref,
             dq_ref, dk_ref, dv_ref, dg_ref, dc_ref, dk_acc, dv_acc):
        i = pl.program_id(1)

        @pl.when(i == 0)
        def _():
            dk_acc[...] = jnp.zeros_like(dk_acc)
            dv_acc[...] = jnp.zeros_like(dv_acc)
            dg_ref[...] = jnp.zeros_like(dg_ref)
            dc_ref[...] = jnp.zeros_like(dc_ref)

        q = q_ref[...]
        ov = o_ref[...]
        d_o, dg = _head_norm_bwd(ov, g_ref[0], don_ref[...])
        dg_ref[0] += dg
        do_b = d_o.astype(MXU)
        delta = jnp.sum(do_b.astype(F32) * ov, axis=1, keepdims=True)
        cqv, lsev = ccol_ref[0], lse_ref[0]
        cl = cqv - lsev
        masks = _diag_masks(tq, False)

        def group(j0, n, dq, mk):
            offs = [pl.multiple_of((j0 + kk) * HD, HD) for kk in range(n)]
            ss = [lax.dot_general(q, k_ref[pl.ds(offs[kk], HD), :], NT, preferred_element_type=F32) * scale
                  + (cl - crow_ref[0, pl.ds(j0 + kk, 1), :]) for kk in range(n)]
            dps = [lax.dot_general(do_b, v_ref[pl.ds(offs[kk], HD), :], NT, preferred_element_type=F32)
                   for kk in range(n)]
            for kk in range(n):
                p = jnp.exp(ss[kk])
                if mk is not None:
                    p = jnp.where(mk[kk], p, 0.0)
                ds = p * (dps[kk] - delta)
                dc_ref[0, pl.ds(j0 + kk, 1), :] -= jnp.sum(ds, axis=0, keepdims=True)
                ds_b = (ds * scale).astype(MXU)
                dk_acc[pl.ds(offs[kk], HD), :] += lax.dot_general(ds_b, q, TN, preferred_element_type=F32)
                dv_acc[pl.ds(offs[kk], HD), :] += lax.dot_general(p.astype(MXU), do_b, TN,
                                                                  preferred_element_type=F32)
                dq = dq + jnp.dot(ds_b, k_ref[pl.ds(offs[kk], HD), :], preferred_element_type=F32)
            return dq

        dq = _walk_groups(i, R, group, jnp.zeros((tq, HD), F32), masks, False)
        dq_ref[...] = dq.astype(MXU)

        @pl.when(i == nq - 1)
        def _():
            dk_ref[...] = dk_acc[...].astype(MXU)
            dv_ref[...] = dv_acc[...].astype(MXU)

    blk = pl.BlockSpec((tq, HD), lambda h, i: (i, h))
    full = pl.BlockSpec((LP, HD), lambda h, i: (0, h))
    gspec = pl.BlockSpec((1, 1, HD), lambda h, i: (h, 0, 0))
    col = pl.BlockSpec((1, tq, 1), lambda h, i: (h, i, 0))
    rowv = pl.BlockSpec((1, LP // HD, HD), lambda h, i: (h, 0, 0))
    return call_with_exchange(
        body, name, (NH, nq),
        [pl.BlockSpec((tq, HD), lambda h, i: (i, cq + h)),
         pl.BlockSpec((LP, HD), lambda h, i: (0, ck + h)),
         pl.BlockSpec((LP, HD), lambda h, i: (0, cv + h)),
         gspec, col, rowv, blk, blk, col],
        (blk, full, full, gspec, rowv),
        (jax.ShapeDtypeStruct((LP, cfg.WG), MXU),) * 3
        + (jax.ShapeDtypeStruct((NH, 1, HD), F32), jax.ShapeDtypeStruct((NH, LP // HD, HD), F32)),
        [pltpu.VMEM((LP, HD), F32), pltpu.VMEM((LP, HD), F32)],
        (qkv, qkv, qkv, g3, c_col, c_row, o_pre, d_on, lse), xfer)


def _row_tile(rows, cols, n_arrays):
    budget = 24 * 1024 * 1024 // (2 * n_arrays * cols * 4)
    cap = max(8, min(rows, budget // 8 * 8))
    div = _tile(rows, cap, 8)
    return div if div <= cap and div * 4 >= cap else cap


def sum_slots(recvs, name, swap=None):
    S, rows, cols = recvs[0].shape
    nl = len(recvs)
    tr = _tile(rows, _row_tile(rows, cols, nl * S + 1), 8)
    nb = rows // tr

    def body(*refs):
        o_ref = refs[nl + (swap is not None)]
        layer, blk = pl.program_id(0), pl.program_id(1)
        if swap is not None:
            x, y, c = _coords()
            cp = pltpu.make_async_remote_copy(src_ref=refs[nl], dst_ref=refs[nl + 2], send_sem=refs[nl + 3],
                                              recv_sem=refs[nl + 4], device_id=(x, y, 1 - c), device_id_type=MESH)

            @pl.when((layer == 0) & (blk == 0))
            def _():
                cp.start()

        for ll in range(nl):
            @pl.when(layer == ll)
            def _(r_ref=refs[ll]):
                acc = r_ref[0].astype(F32)
                for s in range(1, S):
                    acc = acc + r_ref[s].astype(F32)
                o_ref[...] = acc

        if swap is not None:
            @pl.when((layer == nl - 1) & (blk == nb - 1))
            def _():
                cp.wait()

    def spec(ll):
        return pl.BlockSpec((S, tr, cols), lambda l, i: (0, jnp.where(l == ll, i, jnp.where(l < ll, 0, nb - 1)), 0))

    any_spec = pl.BlockSpec(memory_space=pl.ANY)
    out = jax.ShapeDtypeStruct((nl * rows, cols), F32)
    out_spec = pl.BlockSpec((tr, cols), lambda l, i: (l * nb + i, 0))
    extra = swap is not None
    return pl.pallas_call(
        body, name=name, grid=(nl, nb),
        out_shape=(out, jax.ShapeDtypeStruct(swap.shape, swap.dtype)) if extra else out,
        in_specs=[spec(ll) for ll in range(nl)] + ([any_spec] if extra else []),
        out_specs=(out_spec, any_spec) if extra else out_spec,
        scratch_shapes=[pltpu.SemaphoreType.DMA, pltpu.SemaphoreType.DMA] if extra else [],
        compiler_params=_cp(("arbitrary", "arbitrary")))(*recvs, *([swap] if extra else []))


def adamw(w, m, v, g_parts, name):
    rows, cols = w.shape
    npart = len(g_parts)
    tr = _row_tile(rows, cols, 7 + npart)
    c1 = 1.0 - ADAM_B1 ** ADAM_STEP
    c2 = 1.0 - ADAM_B2 ** ADAM_STEP

    def body(*refs):
        w_ref, m_ref, v_ref = refs[:3]
        g_refs = refs[3:3 + npart]
        g_out, d_out, m_out, v_out = refs[3 + npart:]
        g = g_refs[0][...]
        for r in g_refs[1:]:
            g = g + r[...]
        g_out[...] = g
        mn = ADAM_B1 * m_ref[...] + (1.0 - ADAM_B1) * g
        vn = ADAM_B2 * v_ref[...] + (1.0 - ADAM_B2) * (g * g)
        m_out[...] = mn
        v_out[...] = vn
        d_out[...] = -ADAM_LR * ((mn / c1) / (jnp.sqrt(vn / c2) + ADAM_EPS) + ADAM_WD * w_ref[...])

    blk = pl.BlockSpec((tr, cols), lambda i: (i, 0))
    return pl.pallas_call(
        body, name=name, out_shape=(jax.ShapeDtypeStruct((rows, cols), F32),) * 4,
        grid=(pl.cdiv(rows, tr),), in_specs=[blk] * (3 + npart), out_specs=(blk,) * 4,
        compiler_params=_cp(("parallel",)))(w, m, v, *g_parts)


def _coords():
    return lax.axis_index("x"), lax.axis_index("y"), lax.axis_index("c")


def chip_exchange(arrs, scatter, name):
    n = len(arrs)

    def body(*refs):
        ins, outs, sems = refs[:n], refs[n:2 * n], refs[2 * n:]
        _xfer_start(ins, outs, sems, scatter)
        _xfer_forward(ins, outs, sems, scatter)
        _xfer_finish(ins, outs, sems, scatter)

    out_shape, scratch = _xfer_shapes(arrs, scatter)
    any_spec = pl.BlockSpec(memory_space=pl.ANY)
    return pl.pallas_call(body, name=name, out_shape=out_shape, in_specs=[any_spec] * n, out_specs=(any_spec,) * n,
                          scratch_shapes=scratch)(*arrs)


def _xfer_shapes(arrs, scatter):
    n = len(arrs)
    shapes = [a.shape[1:] if scatter else a.shape for a in arrs]
    out_shape = tuple(jax.ShapeDtypeStruct((NCHIP,) + tuple(s), a.dtype) for s, a in zip(shapes, arrs))
    scratch = [pltpu.SemaphoreType.DMA((n, 6)), pltpu.SemaphoreType.DMA((n, 6)), pltpu.SemaphoreType.DMA((n,))]
    return out_shape, scratch


def _xfer_copies(ins, outs, sems, scatter):
    send_sems, recv_sems, local_sems = sems
    n = len(ins)
    x, y, c = _coords()
    me = 2 * x + y
    peers = [(1 - x, y), (x, 1 - y), (1 - x, 1 - y)]

    def rdma(src, dst, a, j, dev):
        return pltpu.make_async_remote_copy(src_ref=src, dst_ref=dst, send_sem=send_sems.at[a, j],
                                            recv_sem=recv_sems.at[a, j], device_id=dev, device_id_type=MESH)

    def half(ref, hc):
        hr = ref.shape[0] // 2
        return ref.at[pl.ds(pl.multiple_of(hc * hr, 8), hr)]

    local, sends, recvs, fwds, fwd_recvs = [], [], [], [], []
    for a in range(n):
        local.append(pltpu.make_async_copy(ins[a].at[me] if scatter else ins[a], outs[a].at[me], local_sems.at[a]))
        for k, (px, py) in enumerate(peers):
            there = 2 * px + py
            if scatter:
                sends.append(rdma(ins[a].at[there], outs[a].at[me], a, k, (px, py, c)))
                recvs.append(rdma(ins[a].at[me], outs[a].at[there], a, k, (px, py, c)))
            else:
                mine, landed = half(outs[a].at[me], c), half(outs[a].at[there], c)
                sends.append(rdma(half(ins[a], c), mine, a, k, (px, py, c)))
                recvs.append(rdma(half(ins[a], c), landed, a, k, (px, py, c)))
                fwds.append(rdma(landed, landed, a, 3 + k, (x, y, 1 - c)))
                other = half(outs[a].at[there], 1 - c)
                fwd_recvs.append(rdma(other, other, a, 3 + k, (x, y, 1 - c)))
    return local, sends, recvs, fwds, fwd_recvs


def _xfer_start(ins, outs, sems, scatter):
    local, sends, _, _, _ = _xfer_copies(ins, outs, sems, scatter)
    for cp in local + sends:
        cp.start()


def _xfer_forward(ins, outs, sems, scatter):
    _, _, recvs, fwds, _ = _xfer_copies(ins, outs, sems, scatter)
    for r, f in zip(recvs, fwds):
        r.wait_recv()
        f.start()


def _xfer_finish(ins, outs, sems, scatter):
    local, sends, recvs, fwds, fwd_recvs = _xfer_copies(ins, outs, sems, scatter)
    for cp in (recvs if scatter else fwd_recvs):
        cp.wait_recv()
    for cp in sends + fwds:
        cp.wait_send()
    for cp in local:
        cp.wait()


def call_with_exchange(core, name, grid, in_specs, out_specs, out_shape, scratch, args, xfer):
    n_in, n_out, n_scr = len(in_specs), len(out_specs), len(scratch)
    if xfer is None:
        res = pl.pallas_call(core, name=name, out_shape=out_shape, grid=grid, in_specs=in_specs,
                             out_specs=out_specs, scratch_shapes=scratch,
                             compiler_params=_cp(("arbitrary",) * len(grid)))(*args)
        return res, ()
    arrs, scatter = xfer
    nx = len(arrs)
    x_shape, x_scratch = _xfer_shapes(arrs, scatter)

    def body(*refs):
        a, xi = refs[:n_in], refs[n_in:n_in + nx]
        o, xo = refs[n_in + nx:n_in + nx + n_out], refs[n_in + nx + n_out:n_in + 2 * nx + n_out]
        rest = refs[n_in + 2 * nx + n_out:]
        scr, sems = rest[:n_scr], rest[n_scr:]
        first, late, last = None, None, None
        for d, g in enumerate(grid):
            pid = pl.program_id(d)
            f, m, l = pid == 0, pid == ((3 * g) // 4 if d == 0 else 0), pid == g - 1
            first = f if first is None else first & f
            late = m if late is None else late & m
            last = l if last is None else last & l

        @pl.when(first)
        def _():
            _xfer_start(xi, xo, sems, scatter)

        core(*a, *o, *scr)

        if not scatter:
            @pl.when(late)
            def _():
                _xfer_forward(xi, xo, sems, scatter)

        @pl.when(last)
        def _():
            _xfer_finish(xi, xo, sems, scatter)

    any_spec = pl.BlockSpec(memory_space=pl.ANY)
    res = pl.pallas_call(
        body, name=name, out_shape=tuple(out_shape) + tuple(x_shape), grid=grid,
        in_specs=list(in_specs) + [any_spec] * nx, out_specs=tuple(out_specs) + (any_spec,) * nx,
        scratch_shapes=list(scratch) + x_scratch,
        compiler_params=_cp(("arbitrary",) * len(grid)))(*args, *arrs)
    return res[:n_out], res[n_out:]


def sibling_exchange(arrs, name):
    n = len(arrs)

    def body(*refs):
        ins, outs = refs[:n], refs[n:2 * n]
        send_sems, recv_sems = refs[2 * n:]
        x, y, c = _coords()
        cps = [pltpu.make_async_remote_copy(src_ref=ins[a], dst_ref=outs[a], send_sem=send_sems.at[a],
                                            recv_sem=recv_sems.at[a], device_id=(x, y, 1 - c),
                                            device_id_type=MESH) for a in range(n)]
        for cp in cps:
            cp.start()
        for cp in cps:
            cp.wait()

    any_spec = pl.BlockSpec(memory_space=pl.ANY)
    return pl.pallas_call(
        body, name=name, out_shape=tuple(jax.ShapeDtypeStruct(a.shape, a.dtype) for a in arrs),
        in_specs=[any_spec] * n, out_specs=(any_spec,) * n,
        scratch_shapes=[pltpu.SemaphoreType.DMA((n,)), pltpu.SemaphoreType.DMA((n,))],
    )(*arrs)


def all_reduce_small(pack, name):
    R = pack.shape[0]

    def body(p_ref, o_ref, slots, send_sems, recv_sems):
        x, y, c = _coords()
        me = 4 * x + 2 * y + c
        slots[me] = p_ref[...]
        cps = []
        for k in range(1, 8):
            kx, ky, kc = (k >> 2) & 1, (k >> 1) & 1, k & 1
            peer = ((1 - x) if kx else x, (1 - y) if ky else y, (1 - c) if kc else c)
            cp = pltpu.make_async_remote_copy(src_ref=p_ref, dst_ref=slots.at[me], send_sem=send_sems.at[k],
                                              recv_sem=recv_sems.at[k], device_id=peer, device_id_type=MESH)
            cp.start()
            cps.append((cp, peer))
        for k, (cp, peer) in enumerate(cps, start=1):
            frm = 4 * peer[0] + 2 * peer[1] + peer[2]
            pltpu.make_async_remote_copy(src_ref=p_ref, dst_ref=slots.at[frm], send_sem=send_sems.at[k],
                                         recv_sem=recv_sems.at[k], device_id=peer, device_id_type=MESH).wait_recv()
        for cp, _ in cps:
            cp.wait_send()
        acc = slots[0]
        for s in range(1, 8):
            acc = acc + slots[s]
        o_ref[...] = acc

    vm = pl.BlockSpec(memory_space=pltpu.VMEM)
    return pl.pallas_call(
        body, name=name, out_shape=jax.ShapeDtypeStruct((R, LANES), F32), in_specs=[vm], out_specs=vm,
        scratch_shapes=[pltpu.VMEM((8, R, LANES), F32), pltpu.SemaphoreType.DMA((8,)),
                        pltpu.SemaphoreType.DMA((8,))],
        compiler_params=pltpu.CompilerParams(vmem_limit_bytes=VMEM_LIMIT),
    )(pack)


def _gate_up(w):
    r, c = w.shape
    return w.reshape(r, 2, c // 2).transpose(1, 0, 2)


def _gate_up_inv(w):
    return w.transpose(1, 0, 2).reshape(w.shape[1], -1)


def _pack(arrs):
    parts = []
    for a in arrs:
        f = a.reshape(-1).astype(F32)
        parts.append(jnp.pad(f, (0, -f.shape[0] % 1024)).reshape(-1, LANES))
    return jnp.concatenate(parts, axis=0)


def _unpack(p, shapes):
    out, r = [], 0
    for s in shapes:
        n = math.prod(s)
        nr = (n + 1023) // 1024 * 8
        out.append(p[r:r + nr].reshape(-1)[:n].reshape(s))
        r += nr
    return out


def _vec(g):
    return g.reshape(1, -1)


class _LocalWeights:
    def __init__(self, cfg, wf):
        self.cfg, self.wf, self.grads = cfg, wf, {}

    def w_in(self, l):
        return self.wf['w_in'][l]

    def fwd_exchanges(self, l):
        return None, None

    def rest(self, l, got_sb, got_fox):
        w_up = self.wf['w_up'][l]
        return dict(w_out=self.wf['w_out'][l], w_down=self.wf['w_down'][l],
                    w_up=w_up.reshape(w_up.shape[0], NCHIP, -1).transpose(1, 0, 2))

    def bwd_exchanges(self, l, g):
        return None, None

    def in_dx_exchange(self, l, g):
        return None

    def bwd_done(self, l, g, got_sb, got_fox, got_in):
        self.grads[l] = g


def _w_in_parts(cfg, w_in):
    w_f = jnp.pad(w_in[:, cfg.NQKV:], ((0, 0), (0, LANES - cfg.NH)))
    return dict(w_qkv=w_in[:, :cfg.NQKV], w_f=w_f, w_in_ext=jnp.concatenate([w_in[:, :cfg.NQKV], w_f], axis=1))


def _layer_fwd(cfg, l, h, u, wl, io):
    tag = f"l{l}"
    qkv = matmul(u, wl['w_qkv'], 'nn', MXU, f"{tag}_qkv", tn_cap=1024)
    f_logit = matmul(u, wl['w_f'], 'nn', F32, f"{tag}_fproj")
    cpre = fox_gate_fwd(cfg, f_logit, wl['b_pad'], f"{tag}_gate_fwd")
    c_heads = cpre[:, :cfg.NH].T
    c_col = c_heads[:, :, None]
    c_row = c_heads.reshape(cfg.NH, cfg.LP // HD, HD)
    x_sb, x_fox = io.fwd_exchanges(l)
    (o_sb, on_sb, rtot), got_sb = sb_fwd(cfg, qkv, wl['g_sb'], f"{tag}_sb_fwd", x_sb)
    (o_fx, on_fx, lse), got_fox = fox_fwd(cfg, qkv, wl['g_fox'], c_col, c_row, f"{tag}_fox_fwd", x_fox)
    wl.update(io.rest(l, got_sb, got_fox))
    mixin = jnp.concatenate([on_sb, on_fx], axis=1)
    mix = matmul(mixin, wl['w_out'], 'nn', F32, f"{tag}_out")
    h1, u2 = resid_norm(cfg, mix, h, wl['g_mix_post'], wl['g_ffn_pre'], f"{tag}_mixres")
    a = matmul(u2, wl['w_up'], 'nn', F32, f"{tag}_up", tn_cap=1408, out_split=2)
    act = conv_act_fwd(cfg, a, wl['conv_w'], wl['conv_b'], f"{tag}_conv_fwd")
    ff = matmul(act, wl['w_down'], 'nn', F32, f"{tag}_down", tm_cap=704)
    h2, u_next = resid_norm(cfg, ff, h1, wl['g_ffn_post'], wl['g_next'], f"{tag}_ffnres")
    saved = dict(h=h, u=u, qkv=qkv, f_logit=f_logit, c_col=c_col, c_row=c_row, o_sb=o_sb, o_fx=o_fx, rtot=rtot,
                 lse=lse, mixin=mixin, mix=mix, h1=h1, u2=u2, a=a, act=act, ff=ff)
    return h2, u_next, saved


def _layer_bwd(cfg, l, dh2, wl, sv, io):
    tag = f"l{l}"
    g = {}
    d_ff, g['g_ffn_post'] = norm_bwd(cfg, sv['ff'], wl['g_ffn_post'], dh2, None, MXU, f"{tag}_ffnpost_bwd")
    d_act = matmul(d_ff, wl['w_down'], 'nt', F32, f"{tag}_down_dx")
    g['w_down'] = matmul(sv['act'], d_ff, 'tn', MXU, f"{tag}_down_dw", tm_cap=704, tk_cap=4224)
    d_a, d_conv = conv_act_bwd(cfg, sv['a'], d_act, wl['conv_w'], wl['conv_b'], f"{tag}_conv_bwd")
    g['conv'] = d_conv
    g['w_up'] = matmul(sv['u2'], d_a, 'tn', MXU, f"{tag}_up_dw", tm_cap=512, tn_cap=1408, tk_cap=4224,
                       out_split=NCHIP)
    du2 = matmul(d_a, wl['w_up'], 'nt', F32, f"{tag}_up_dx", tm_cap=704, tk_cap=5632)
    dh1, g['g_ffn_pre'] = norm_bwd(cfg, sv['h1'], wl['g_ffn_pre'], du2, dh2, F32, f"{tag}_ffnpre_bwd")
    d_mix, g['g_mix_post'] = norm_bwd(cfg, sv['mix'], wl['g_mix_post'], dh1, None, MXU, f"{tag}_mixpost_bwd")
    d_mixin = matmul(d_mix, wl['w_out'], 'nt', F32, f"{tag}_out_dx")
    g['w_out'] = matmul(sv['mixin'], d_mix, 'tn', MXU, f"{tag}_out_dw", tm_cap=1024, tk_cap=4224)
    WG = cfg.WG
    x_sb, x_fox = io.bwd_exchanges(l, g)
    (dq_s, dk_s, dv_s, g['g_sb']), got_sb = sb_bwd(cfg, sv['qkv'], wl['g_sb'], sv['o_sb'], d_mixin[:, :WG],
                                                   sv['rtot'], f"{tag}_sb_bwd", x_sb)
    (dq_f, dk_f, dv_f, g['g_fox'], dc_row), got_fox = fox_bwd(cfg, sv['qkv'], wl['g_fox'], sv['c_col'], sv['c_row'],
                                                              sv['o_fx'], d_mixin[:, WG:], sv['lse'],
                                                              f"{tag}_fox_bwd", x_fox)
    dc = jnp.pad(dc_row.reshape(cfg.NH, cfg.LP).T, ((0, 0), (0, LANES - cfg.NH)))
    d_f, g['b_f'] = fox_gate_bwd(cfg, dc, sv['f_logit'], wl['b_pad'], f"{tag}_gate_bwd")
    d_proj = jnp.concatenate([dq_s, dk_s, dv_s, dq_f, dk_f, dv_f, d_f], axis=1)
    g['w_in_ext'] = matmul(sv['u'], d_proj, 'tn', MXU, f"{tag}_in_dw", tm_cap=1024, tn_cap=896, tk_cap=4224)
    x_in = io.in_dx_exchange(l, g)
    du = matmul(d_proj, wl['w_in_ext'], 'nt', F32, f"{tag}_in_dx", tm_cap=704, xfer=x_in)
    du, got_in = du if x_in is not None else (du, None)
    dh, g['g_mix_pre'] = norm_bwd(cfg, sv['h'], wl['g_mix_pre'], du, dh1, F32, f"{tag}_mixpre_bwd")
    io.bwd_done(l, g, got_sb, got_fox, got_in)
    return dh


def _local_step(cfg, h0, target_p, wf, io=None):
    io = _LocalWeights(cfg, wf) if io is None else io
    layers = []
    for l in range(cfg.DEPTH):
        layers.append(dict(
            b_pad=jnp.pad(wf['b_f'][l], (0, LANES - cfg.NH)).reshape(1, LANES),
            g_sb=wf['g_sb'][l][:, None, :], g_fox=wf['g_fox'][l][:, None, :],
            conv_w=_gate_up(wf['conv_w'][l]), conv_b=wf['conv_b'][l].reshape(2, 1, cfg.F),
            g_mix_pre=_vec(wf['g_mix_pre'][l]), g_mix_post=_vec(wf['g_mix_post'][l]),
            g_ffn_pre=_vec(wf['g_ffn_pre'][l]), g_ffn_post=_vec(wf['g_ffn_post'][l]),
            g_next=_vec(wf['g_mix_pre'][(l + 1) % cfg.DEPTH])))
    h = h0
    u = pre_norm(cfg, h0, layers[0]['g_mix_pre'], "l0_prenorm")
    saved = []
    for l in range(cfg.DEPTH):
        layers[l].update(_w_in_parts(cfg, io.w_in(l)))
        h, u, sv = _layer_fwd(cfg, l, h, u, layers[l], io)
        saved.append(sv)
    dh, loss_blk = loss_head(cfg, h, target_p, "loss_head")
    for l in reversed(range(cfg.DEPTH)):
        dh = _layer_bwd(cfg, l, dh, layers[l], saved[l], io)
    return loss_blk, dh, [io.grads[l] for l in range(cfg.DEPTH)]


def _cols(g):
    return g.transpose(1, 0, 2).reshape(g.shape[1], -1)


def _rows(g):
    return g.reshape(-1, g.shape[2])


def _send_cols(g):
    r, c = g.shape
    return g.reshape(r, NCHIP, c // NCHIP).transpose(1, 0, 2).astype(MXU)


def _send_rows(g):
    r, c = g.shape
    return g.reshape(NCHIP, r // NCHIP, c).astype(MXU)


class _StreamedWeights:
    def __init__(self, cfg, w):
        self.cfg = cfg
        self.shard = {n: [w[n][l].astype(MXU) for l in range(cfg.DEPTH)] for n in BIG}
        self.grads, self.recv = {}, {n: [None] * cfg.DEPTH for n in BIG}
        self.win = {0: _cols(chip_exchange([self.shard['w_in'][0]], False, "gather_w_in0")[0])}
        self.pending = None

    def w_in(self, l):
        return self.win[l]

    def fwd_exchanges(self, l):
        s = self.shard
        fox = [s['w_down'][l]] + ([s['w_in'][l + 1]] if l + 1 < self.cfg.DEPTH else [])
        return ([s['w_out'][l], s['w_up'][l]], False), (fox, False)

    def rest(self, l, got_sb, got_fox):
        if l + 1 < self.cfg.DEPTH:
            self.win[l + 1] = _cols(got_fox[1])
        return dict(w_out=_rows(got_sb[0]), w_up=got_sb[1], w_down=_rows(got_fox[0]))

    def bwd_exchanges(self, l, g):
        sb = [g['w_up']] + ([] if self.pending is None else [self.pending])
        self.pending = None
        return (sb, True), ([_send_rows(g['w_down']), _send_rows(g['w_out'])], True)

    def in_dx_exchange(self, l, g):
        send = _send_cols(g['w_in_ext'][:, :self.cfg.N_IN])
        if l == 0:
            return [send], True
        self.pending = send
        return None

    def bwd_done(self, l, g, got_sb, got_fox, got_in):
        self.grads[l] = g
        self.recv['w_up'][l] = got_sb[0]
        if len(got_sb) > 1:
            self.recv['w_in'][l + 1] = got_sb[1]
        self.recv['w_down'][l], self.recv['w_out'][l] = got_fox
        if got_in is not None:
            self.recv['w_in'][l] = got_in[0]


def _step(cfg, x, w, target, m, v):
    xi, yi, ci = _coords()
    chip = 2 * xi + yi
    D, LP, L, NM = cfg.D, cfg.LP, cfg.L, cfg.NMETA
    DEP = cfg.DEPTH
    io = _StreamedWeights(cfg, w)
    wf = {}
    small_in = _pack([lax.dynamic_update_slice(jnp.zeros((NM, D), F32), w['meta'], (0, chip * (D // NCHIP))),
                      lax.dynamic_update_slice(jnp.zeros((DEP, 3, cfg.F2), F32), w['conv_w'],
                                               (0, 0, chip * (cfg.F2 // NCHIP)))])
    small_in = jnp.where(ci == 0, small_in, 0.0)
    meta_full, conv_w_full = _unpack(all_reduce_small(small_in, "gather_small"), [(NM, D), (DEP, 3, cfg.F2)])
    for n in SMALL:
        wf[n] = w[n]
    wf['conv_w'] = conv_w_full

    zpad = jnp.zeros((LP - L, D), F32)
    h0 = jnp.concatenate([meta_full, x[0], zpad], axis=0)
    target_p = jnp.concatenate([jnp.zeros((NM, D), F32), target[0], zpad], axis=0)
    loss_blk, dh0, grads = _local_step(cfg, h0, target_p, wf, io)

    def stack(key, shape):
        return jnp.stack([grads[l][key].reshape(shape) for l in range(DEP)])

    conv_g = jnp.stack([_gate_up_inv(grads[l]['conv']) for l in range(DEP)])
    small_g = dict(
        loss=loss_blk[0:1, 0:1], meta=dh0[:NM], g_mix_pre=stack('g_mix_pre', (D,)),
        b_f=stack('b_f', (LANES,))[:, :cfg.NH], g_sb=stack('g_sb', (cfg.NH, HD)), g_fox=stack('g_fox', (cfg.NH, HD)),
        g_mix_post=stack('g_mix_post', (D,)), g_ffn_pre=stack('g_ffn_pre', (D,)),
        conv_w=conv_g[:, 0:3], conv_b=conv_g[:, 3], g_ffn_post=stack('g_ffn_post', (D,)))
    keys = list(small_g)
    red = dict(zip(keys, _unpack(all_reduce_small(_pack([small_g[k] for k in keys]), "reduce_small"),
                                 [small_g[k].shape for k in keys])))
    loss = red['loss'].reshape(())
    red['meta'] = lax.dynamic_slice(red['meta'], (0, chip * (D // NCHIP)), (NM, D // NCHIP))
    red['conv_w'] = lax.dynamic_slice(red['conv_w'], (0, 0, chip * (cfg.F2 // NCHIP)), (DEP, 3, cfg.F2 // NCHIP))

    recv = io.recv
    order = ['w_up', 'w_in', 'w_down', 'w_out']
    part, other = {}, {}
    for k, n in enumerate(order):
        if k == 0:
            part[n] = sum_slots(recv[n], f"sum_{n}")
        else:
            part[n], other[order[k - 1]] = sum_slots(recv[n], f"sum_{n}", swap=part[order[k - 1]])
    other[order[-1]], = sibling_exchange([part[order[-1]]], "sibling_grads")

    outs = {}
    for n in BIG:
        shp = w[n].shape
        s2 = (shp[0] * shp[1], shp[2])
        res = adamw(w[n].reshape(s2), m[n].reshape(s2), v[n].reshape(s2), [part[n], other[n]], f"adamw_{n}")
        outs[n] = [r.reshape(shp) for r in res]

    for n in SMALL:
        shp = w[n].shape
        s2 = (shp[0], math.prod(shp[1:]))
        res = adamw(w[n].reshape(s2), m[n].reshape(s2), v[n].reshape(s2), [red[n].reshape(s2)], f"adamw_{n}")
        outs[n] = [r.reshape(shp) for r in res]

    grad_x = dh0[NM:L][None]
    return (loss, grad_x, *[outs[n][0] for n in WEIGHTS], *[outs[n][1] for n in WEIGHTS],
            *[outs[n][2] for n in WEIGHTS], *[outs[n][3] for n in WEIGHTS])


def kernel(x, meta, g_mix_pre, w_in, b_f, g_sb, g_fox, w_out, g_mix_post, g_ffn_pre, w_up, conv_w, conv_b, w_down, g_ffn_post, loss_target, m_meta, m_g_mix_pre, m_w_in, m_b_f, m_g_sb, m_g_fox, m_w_out, m_g_mix_post, m_g_ffn_pre, m_w_up, m_conv_w, m_conv_b, m_w_down, m_g_ffn_post, v_meta, v_g_mix_pre, v_w_in, v_b_f, v_g_sb, v_g_fox, v_w_out, v_g_mix_post, v_g_ffn_pre, v_w_up, v_conv_w, v_conv_b, v_w_down, v_g_ffn_post):
    w = dict(zip(WEIGHTS, (meta, g_mix_pre, w_in, b_f, g_sb, g_fox, w_out, g_mix_post, g_ffn_pre, w_up, conv_w,
                           conv_b, w_down, g_ffn_post)))
    m = dict(zip(WEIGHTS, (m_meta, m_g_mix_pre, m_w_in, m_b_f, m_g_sb, m_g_fox, m_w_out, m_g_mix_post, m_g_ffn_pre,
                           m_w_up, m_conv_w, m_conv_b, m_w_down, m_g_ffn_post)))
    v = dict(zip(WEIGHTS, (v_meta, v_g_mix_pre, v_w_in, v_b_f, v_g_sb, v_g_fox, v_w_out, v_g_mix_post, v_g_ffn_pre,
                           v_w_up, v_conv_w, v_conv_b, v_w_down, v_g_ffn_post)))
    return _step(PROD, x, w, loss_target, m, v)
```

```python
import functools
import math
from typing import NamedTuple

import jax
import jax.numpy as jnp
from jax import lax
from jax.experimental import pallas as pl
from jax.experimental.pallas import tpu as pltpu

F32 = jnp.float32
MXU = jnp.bfloat16
HD = 128
LANES = 128
EPS = 1e-6
NEG = -1e30
LK_PIECES = 2
ADAM_LR, ADAM_B1, ADAM_B2, ADAM_EPS, ADAM_WD, ADAM_STEP = 0.001, 0.9, 0.999, 1e-08, 0.01, 10
VMEM_LIMIT = 56 * 1024 * 1024
MESH = pl.DeviceIdType.MESH
NCHIP = 4

WEIGHTS = ['meta', 'g_mix_pre', 'w_in', 'b_f', 'g_sb', 'g_fox', 'w_out', 'g_mix_post', 'g_ffn_pre',
           'w_up', 'conv_w', 'conv_b', 'w_down', 'g_ffn_post']
BIG = ['w_in', 'w_out', 'w_up', 'w_down']
SMALL = [n for n in WEIGHTS if n not in BIG]

NT = (((1,), (1,)), ((), ()))
TN = (((0,), (0,)), ((), ()))
NN = (((1,), (0,)), ((), ()))


class Cfg(NamedTuple):
    D: int
    SEQ: int
    NMETA: int
    NH: int
    F: int
    LP: int
    tq: int
    tr: int
    FB: int
    DEPTH: int = 2

    @property
    def L(self): return self.SEQ + self.NMETA
    @property
    def WG(self): return self.NH * HD
    @property
    def WMIX(self): return 2 * self.WG
    @property
    def NQKV(self): return 6 * self.WG
    @property
    def N_IN(self): return self.NQKV + self.NH
    @property
    def NEXT(self): return self.NQKV + LANES
    @property
    def F2(self): return 2 * self.F


PROD = Cfg(D=2048, SEQ=4096, NMETA=16, NH=8, F=5632, LP=4224, tq=384, tr=384, FB=512)


def _tile(n, cap, mult=LANES):
    best = None
    for t in range(mult, min(n, cap) + 1, mult):
        if n % t == 0:
            best = t
    return best if best is not None else n


def _cp(sem):
    return pltpu.CompilerParams(dimension_semantics=sem, vmem_limit_bytes=VMEM_LIMIT)


def _split_dot(x, tri, pieces):
    acc, r = None, x
    for p in range(pieces):
        xp = r.astype(MXU)
        d = jnp.dot(xp, tri, preferred_element_type=F32)
        acc = d if acc is None else acc + d
        if p + 1 < pieces:
            r = r - xp.astype(F32)
    return acc


def _split_dot_left(tri, x, pieces):
    acc, r = None, x
    for p in range(pieces):
        xp = r.astype(MXU)
        d = jnp.dot(tri, xp, preferred_element_type=F32)
        acc = d if acc is None else acc + d
        if p + 1 < pieces:
            r = r - xp.astype(F32)
    return acc


def _softplus(z):
    return jnp.maximum(z, 0.0) + jnp.log(1.0 + jnp.exp(-jnp.abs(z)))


def _shape2(x):
    return tuple(x.shape) if x.ndim == 2 else (x.shape[1], x.shape[0] * x.shape[2])


def _split_width(x):
    return x.shape[-1]


def _spec2(x, rb, cb, idx):
    if len(x.shape) == 2:
        return pl.BlockSpec((rb, cb), idx)
    per = x.shape[2] // cb

    def im(i, j, k):
        ri, ci = idx(i, j, k)
        return ci // per, ri, ci % per

    return pl.BlockSpec((None, rb, cb), im)


def matmul(a, b, mode, out_dtype, name, tm_cap=1408, tn_cap=1024, tk_cap=8192, out_split=None, xfer=None):
    (K, M) = _shape2(a) if mode == 'tn' else _shape2(a)[::-1]
    N = _shape2(b)[0] if mode == 'nt' else _shape2(b)[1]
    n_unit = math.gcd(N // (out_split or 1), _split_width(b) if mode != 'nt' else N)
    k_unit = math.gcd(_split_width(a) if mode != 'tn' else K, _split_width(b) if mode == 'nt' else K)
    tm = _tile(M, tm_cap, LANES if mode == 'tn' else 8)
    tn, tk = _tile(n_unit, tn_cap), _tile(k_unit, tk_cap)
    nk = K // tk
    dn = {'nn': NN, 'nt': NT, 'tn': TN}[mode]

    def body(a_ref, b_ref, o_ref, *scratch):
        d = lax.dot_general(a_ref[...], b_ref[...], dn, preferred_element_type=F32)
        if nk == 1:
            o_ref[...] = d.astype(out_dtype)
        else:
            acc_ref, = scratch
            k = pl.program_id(2)

            @pl.when(k == 0)
            def _():
                acc_ref[...] = d

            @pl.when(k > 0)
            def _():
                acc_ref[...] += d

            @pl.when(k == nk - 1)
            def _():
                o_ref[...] = acc_ref[...].astype(out_dtype)

    a_spec = (_spec2(a, tk, tm, lambda i, j, k: (k, i)) if mode == 'tn'
              else _spec2(a, tm, tk, lambda i, j, k: (i, k)))
    b_spec = (_spec2(b, tn, tk, lambda i, j, k: (j, k)) if mode == 'nt'
              else _spec2(b, tk, tn, lambda i, j, k: (k, j)))
    out = (jax.ShapeDtypeStruct((M, N), out_dtype) if out_split is None
           else jax.ShapeDtypeStruct((out_split, M, N // out_split), out_dtype))
    (res,), got = call_with_exchange(
        body, name, (M // tm, N // tn, nk), [a_spec, b_spec], (_spec2(out, tm, tn, lambda i, j, k: (i, j)),),
        (out,), [] if nk == 1 else [pltpu.VMEM((tm, tn), F32)], (a, b), xfer)
    return res if xfer is None else (res, got)


def _rstd(x):
    return lax.rsqrt(jnp.mean(x * x, axis=-1, keepdims=True) + EPS)


def pre_norm(cfg, h, g, name):
    LP, D, tr = cfg.LP, cfg.D, cfg.tr

    def body(h_ref, g_ref, u_ref):
        x = h_ref[...]
        u_ref[...] = ((x * _rstd(x)) * g_ref[...]).astype(MXU)

    row = pl.BlockSpec((tr, D), lambda i: (i, 0))
    vec = pl.BlockSpec((1, D), lambda i: (0, 0))
    return pl.pallas_call(body, name=name, out_shape=jax.ShapeDtypeStruct((LP, D), MXU), grid=(LP // tr,),
                          in_specs=[row, vec], out_specs=row, compiler_params=_cp(("parallel",)))(h, g)


def resid_norm(cfg, y, h, g_post, g_next, name):
    LP, D, tr = cfg.LP, cfg.D, cfg.tr

    def body(y_ref, h_ref, gp_ref, gn_ref, hn_ref, u_ref):
        yv = y_ref[...]
        hn = h_ref[...] + (yv * _rstd(yv)) * gp_ref[...]
        hn_ref[...] = hn
        u_ref[...] = ((hn * _rstd(hn)) * gn_ref[...]).astype(MXU)

    row = pl.BlockSpec((tr, D), lambda i: (i, 0))
    vec = pl.BlockSpec((1, D), lambda i: (0, 0))
    return pl.pallas_call(
        body, name=name,
        out_shape=(jax.ShapeDtypeStruct((LP, D), F32), jax.ShapeDtypeStruct((LP, D), MXU)),
        grid=(LP // tr,), in_specs=[row, row, vec, vec], out_specs=(row, row),
        compiler_params=_cp(("parallel",)))(y, h, g_post, g_next)


def loss_head(cfg, h, target, name):
    LP, D, tr = cfg.LP, cfg.D, cfg.tr
    lo, hi = cfg.NMETA, cfg.L

    def body(h_ref, t_ref, dh_ref, loss_ref):
        i = pl.program_id(0)
        rows = lax.broadcasted_iota(jnp.int32, (tr, D), 0) + i * tr
        diff = jnp.where((rows >= lo) & (rows < hi), h_ref[...] - t_ref[...], 0.0)
        dh_ref[...] = diff * (1.0 / D)
        part = 0.5 * jnp.sum(jnp.sum(diff * diff, axis=1, keepdims=True), axis=0, keepdims=True) * (1.0 / D)

        @pl.when(i == 0)
        def _():
            loss_ref[...] = jnp.zeros_like(loss_ref)

        loss_ref[...] += jnp.broadcast_to(part, loss_ref.shape)

    row = pl.BlockSpec((tr, D), lambda i: (i, 0))
    return pl.pallas_call(
        body, name=name,
        out_shape=(jax.ShapeDtypeStruct((LP, D), F32), jax.ShapeDtypeStruct((8, LANES), F32)),
        grid=(LP // tr,), in_specs=[row, row],
        out_specs=(row, pl.BlockSpec((8, LANES), lambda i: (0, 0))),
        compiler_params=_cp(("arbitrary",)))(h, target)


def norm_bwd(cfg, x, g, dy, dres, out_dtype, name):
    LP, D, tr = cfg.LP, cfg.D, cfg.tr
    has_res = dres is not None

    def body(*refs):
        if has_res:
            x_ref, g_ref, dy_ref, dres_ref, dx_ref, dg_ref = refs
        else:
            x_ref, g_ref, dy_ref, dx_ref, dg_ref = refs
        xv, dyv = x_ref[...], dy_ref[...]
        r = _rstd(xv)
        xhat = xv * r
        gdy = dyv * g_ref[...]
        dx = r * (gdy - xhat * jnp.mean(gdy * xhat, axis=-1, keepdims=True))
        if has_res:
            dx = dx + dres_ref[...]
        dx_ref[...] = dx.astype(out_dtype)

        @pl.when(pl.program_id(0) == 0)
        def _():
            dg_ref[...] = jnp.zeros_like(dg_ref)

        dg_ref[...] += jnp.sum(dyv * xhat, axis=0, keepdims=True)

    row = pl.BlockSpec((tr, D), lambda i: (i, 0))
    vec = pl.BlockSpec((1, D), lambda i: (0, 0))
    ins = [x, g, dy] + ([dres] if has_res else [])
    return pl.pallas_call(
        body, name=name,
        out_shape=(jax.ShapeDtypeStruct((LP, D), out_dtype), jax.ShapeDtypeStruct((1, D), F32)),
        grid=(LP // tr,), in_specs=[row, vec, row] + ([row] if has_res else []), out_specs=(row, vec),
        compiler_params=_cp(("arbitrary",)))(*ins)


HALO = 8


def _conv3(ext, w, b):
    s1, s2 = pltpu.roll(ext, 1, 0), pltpu.roll(ext, 2, 0)
    return w[0:1, :] * s2 + w[1:2, :] * s1 + w[2:3, :] * ext + b, s1, s2


def conv_act_fwd(cfg, a, cw, cb, name):
    LP, F, FB, tm = cfg.LP, cfg.F, cfg.FB, cfg.tr
    nb = tm // HALO

    def body(a_ref, prev_ref, w_ref, b_ref, act_ref):
        i = pl.program_id(1)
        c = []
        for hh in range(2):
            prev = jnp.where(i > 0, prev_ref[hh], 0.0)
            ext = jnp.concatenate([prev, a_ref[hh]], axis=0)
            c.append(_conv3(ext, w_ref[hh], b_ref[hh])[0][HALO:, :])
        act_ref[...] = (c[0] * jax.nn.sigmoid(c[0]) * c[1]).astype(MXU)

    return pl.pallas_call(
        body, name=name, out_shape=jax.ShapeDtypeStruct((LP, F), MXU), grid=(F // FB, LP // tm),
        in_specs=[pl.BlockSpec((2, tm, FB), lambda j, i: (0, i, j)),
                  pl.BlockSpec((2, HALO, FB), lambda j, i: (0, jnp.maximum(i * nb - 1, 0), j)),
                  pl.BlockSpec((2, 3, FB), lambda j, i: (0, 0, j)),
                  pl.BlockSpec((2, 1, FB), lambda j, i: (0, 0, j))],
        out_specs=pl.BlockSpec((tm, FB), lambda j, i: (i, j)),
        compiler_params=_cp(("parallel", "parallel")))(a, a, cw, cb)


def conv_act_bwd(cfg, a, d_act, cw, cb, name):
    LP, F, FB, tm = cfg.LP, cfg.F, cfg.FB, cfg.tr
    nb, last = tm // HALO, LP // tm - 1
    nrow = LP // HALO
    n = tm + 2 * HALO

    def body(a_ref, aprev_ref, anext_ref, d_ref, dnext_ref, w_ref, b_ref, da_ref, dcv_ref):
        i = pl.program_id(1)

        @pl.when(i == 0)
        def _():
            dcv_ref[...] = jnp.zeros_like(dcv_ref)

        c, exts = [], []
        for hh in range(2):
            prev = jnp.where(i > 0, aprev_ref[hh], 0.0)
            ext = jnp.concatenate([prev, a_ref[hh], anext_ref[hh]], axis=0)
            chh, s1, s2 = _conv3(ext, w_ref[hh], b_ref[hh])
            c.append(chh)
            exts.append((ext, s1, s2))
        dnext = jnp.where(i < last, dnext_ref[...], 0.0)
        dact = jnp.concatenate([jnp.zeros((HALO, FB), F32), d_ref[...], dnext], axis=0)
        sg = jax.nn.sigmoid(c[0])
        d_c = [dact * c[1] * (sg * (1.0 + c[0] * (1.0 - sg))), dact * (c[0] * sg)]
        for hh in range(2):
            dc, w = d_c[hh], w_ref[hh]
            da = w[2:3, :] * dc + w[1:2, :] * pltpu.roll(dc, n - 1, 0) + w[0:1, :] * pltpu.roll(dc, n - 2, 0)
            da_ref[hh] = da[HALO:HALO + tm, :].astype(MXU)
            dcb = dc[HALO:HALO + tm, :]
            ext, s1, s2 = exts[hh]
            dcv_ref[hh, 0:1, :] += jnp.sum(dcb * s2[HALO:HALO + tm, :], axis=0, keepdims=True)
            dcv_ref[hh, 1:2, :] += jnp.sum(dcb * s1[HALO:HALO + tm, :], axis=0, keepdims=True)
            dcv_ref[hh, 2:3, :] += jnp.sum(dcb * ext[HALO:HALO + tm, :], axis=0, keepdims=True)
            dcv_ref[hh, 3:4, :] += jnp.sum(dcb, axis=0, keepdims=True)

    return pl.pallas_call(
        body, name=name,
        out_shape=(jax.ShapeDtypeStruct((2, LP, F), MXU), jax.ShapeDtypeStruct((2, 8, F), F32)),
        grid=(F // FB, LP // tm),
        in_specs=[pl.BlockSpec((2, tm, FB), lambda j, i: (0, i, j)),
                  pl.BlockSpec((2, HALO, FB), lambda j, i: (0, jnp.maximum(i * nb - 1, 0), j)),
                  pl.BlockSpec((2, HALO, FB), lambda j, i: (0, jnp.minimum((i + 1) * nb, nrow - 1), j)),
                  pl.BlockSpec((tm, FB), lambda j, i: (i, j)),
                  pl.BlockSpec((HALO, FB), lambda j, i: (jnp.minimum((i + 1) * nb, nrow - 1), j)),
                  pl.BlockSpec((2, 3, FB), lambda j, i: (0, 0, j)),
                  pl.BlockSpec((2, 1, FB), lambda j, i: (0, 0, j))],
        out_specs=(pl.BlockSpec((2, tm, FB), lambda j, i: (0, i, j)),
                   pl.BlockSpec((2, 8, FB), lambda j, i: (0, 0, j))),
        compiler_params=_cp(("parallel", "arbitrary")))(a, a, a, d_act, d_act, cw, cb)


def fox_gate_fwd(cfg, f_logit, b_pad, name):
    LP, tb = cfg.LP, cfg.tq

    def body(f_ref, b_ref, c_ref, carry_ref):
        @pl.when(pl.program_id(0) == 0)
        def _():
            carry_ref[...] = jnp.zeros_like(carry_ref)

        xv = f_ref[...] + b_ref[...]
        lf = -_softplus(-xv)
        r = lax.broadcasted_iota(jnp.int32, (tb, tb), 0)
        s = lax.broadcasted_iota(jnp.int32, (tb, tb), 1)
        c = _split_dot_left((s <= r).astype(MXU), lf, 3) + carry_ref[0:1, :]
        c_ref[...] = c
        carry_ref[0:1, :] = c[tb - 1:tb, :]

    blk = pl.BlockSpec((tb, LANES), lambda i: (i, 0))
    return pl.pallas_call(
        body, name=name, out_shape=jax.ShapeDtypeStruct((LP, LANES), F32), grid=(LP // tb,),
        in_specs=[blk, pl.BlockSpec((1, LANES), lambda i: (0, 0))], out_specs=blk,
        scratch_shapes=[pltpu.VMEM((8, LANES), F32)], compiler_params=_cp(("arbitrary",)))(f_logit, b_pad)


def fox_gate_bwd(cfg, dc, f_logit, b_pad, name):
    LP, tb = cfg.LP, cfg.tq
    nblk = LP // tb

    def body(dc_ref, f_ref, b_ref, df_ref, db_ref, carry_ref):
        @pl.when(pl.program_id(0) == 0)
        def _():
            carry_ref[...] = jnp.zeros_like(carry_ref)
            db_ref[...] = jnp.zeros_like(db_ref)

        r = lax.broadcasted_iota(jnp.int32, (tb, tb), 0)
        s = lax.broadcasted_iota(jnp.int32, (tb, tb), 1)
        dlf = _split_dot_left((s >= r).astype(MXU), dc_ref[...], 3) + carry_ref[0:1, :]
        carry_ref[0:1, :] = dlf[0:1, :]
        df = dlf * jax.nn.sigmoid(-(f_ref[...] + b_ref[...]))
        df_ref[...] = df.astype(MXU)
        db_ref[...] += jnp.sum(df, axis=0, keepdims=True)

    blk = pl.BlockSpec((tb, LANES), lambda i: (nblk - 1 - i, 0))
    vec = pl.BlockSpec((1, LANES), lambda i: (0, 0))
    return pl.pallas_call(
        body, name=name,
        out_shape=(jax.ShapeDtypeStruct((LP, LANES), MXU), jax.ShapeDtypeStruct((1, LANES), F32)),
        grid=(nblk,), in_specs=[blk, blk, vec], out_specs=(blk, vec),
        scratch_shapes=[pltpu.VMEM((8, LANES), F32)], compiler_params=_cp(("arbitrary",)))(dc, f_logit, b_pad)


def _head_norm_fwd(o, g):
    return (o * lax.rsqrt(jnp.mean(o * o, axis=-1, keepdims=True) + EPS)) * g


def _head_norm_bwd(o, g, d_on):
    r = lax.rsqrt(jnp.mean(o * o, axis=-1, keepdims=True) + EPS)
    ohat = o * r
    gdy = d_on * g
    d_o = r * (gdy - ohat * jnp.mean(gdy * ohat, axis=-1, keepdims=True))
    return d_o, jnp.sum(d_on * ohat, axis=0, keepdims=True)


def _diag_masks(tq, strict):
    rows = lax.broadcasted_iota(jnp.int32, (tq, HD), 0)
    cols = lax.broadcasted_iota(jnp.int32, (tq, HD), 1)
    return [(cols + kk * HD < rows) if strict else (cols + kk * HD <= rows) for kk in range(tq // HD)]


def _walk_groups(i, R, group, carry, masks, descending):
    big = 2 * R
    if descending:
        carry = group(i * R, R, carry, masks)
        carry = lax.fori_loop(0, i % 2, lambda t, c: group((i - 1) * R, R, c, None), carry)
        return lax.fori_loop(0, i // 2, lambda t, c: group((i // 2 - 1 - t) * big, big, c, None), carry)
    carry = lax.fori_loop(0, i // 2, lambda t, c: group(t * big, big, c, None), carry)
    carry = lax.fori_loop(0, i % 2, lambda t, c: group((i // 2) * big, R, c, None), carry)
    return group(i * R, R, carry, masks)


def _tri(pred):
    a = lax.broadcasted_iota(jnp.int32, (HD, HD), 0)
    b = lax.broadcasted_iota(jnp.int32, (HD, HD), 1)
    return pred(a, b).astype(MXU)


def _attn_specs(cfg, group):
    base = 3 * cfg.NH * group
    return base, base + cfg.NH, base + 2 * cfg.NH


def sb_fwd(cfg, qkv, g3, name, xfer=None):
    LP, NH, tq = cfg.LP, cfg.NH, cfg.tq
    nq, R = LP // tq, tq // HD
    scale = HD ** -0.5
    cq, ck, cv = _attn_specs(cfg, 0)

    def body(q_ref, k_ref, v_ref, g_ref, opre_ref, on_ref, rtot_ref):
        i = pl.program_id(1)
        q = q_ref[...]
        masks = _diag_masks(tq, True)
        m_after = _tri(lambda a, b: a > b)

        def group(j0, n, carry, mk):
            acc, rc = carry
            masked = mk is not None
            order = range(n - 1, -1, -1)
            offs = [pl.multiple_of((j0 + kk) * HD, HD) for kk in range(n)]
            zs = [lax.dot_general(q, k_ref[pl.ds(offs[kk], HD), :], NT, preferred_element_type=F32) * scale
                  for kk in range(n)]
            ls, lks, tris = [None] * n, [None] * n, [None] * n
            for kk in order:
                sp = _softplus(zs[kk])
                lks[kk] = jnp.where(mk[kk], -sp, 0.0) if masked else -sp
                tris[kk] = _split_dot(lks[kk], m_after, LK_PIECES)
                ls[kk] = zs[kk] - sp
            for kk in order:
                a = jnp.exp(ls[kk] + (tris[kk] + rc))
                if masked:
                    a = jnp.where(mk[kk], a, 0.0)
                acc = acc + jnp.dot(a.astype(MXU), v_ref[pl.ds(offs[kk], HD), :], preferred_element_type=F32)
                rc = rc + jnp.sum(lks[kk], axis=1, keepdims=True)
            return acc, rc

        acc, rc = _walk_groups(i, R, group, (jnp.zeros((tq, HD), F32), jnp.zeros((tq, 1), F32)), masks, True)
        opre_ref[...] = acc
        on_ref[...] = _head_norm_fwd(acc, g_ref[0]).astype(MXU)
        rtot_ref[0] = rc

    return call_with_exchange(
        body, name, (NH, nq),
        [pl.BlockSpec((tq, HD), lambda h, i: (i, cq + h)),
         pl.BlockSpec((LP, HD), lambda h, i: (0, ck + h)),
         pl.BlockSpec((LP, HD), lambda h, i: (0, cv + h)),
         pl.BlockSpec((1, 1, HD), lambda h, i: (h, 0, 0))],
        (pl.BlockSpec((tq, HD), lambda h, i: (i, h)),
         pl.BlockSpec((tq, HD), lambda h, i: (i, h)),
         pl.BlockSpec((1, tq, 1), lambda h, i: (h, i, 0))),
        (jax.ShapeDtypeStruct((LP, cfg.WG), F32), jax.ShapeDtypeStruct((LP, cfg.WG), MXU),
         jax.ShapeDtypeStruct((NH, LP, 1), F32)),
        [], (qkv, qkv, qkv, g3), xfer)


def sb_bwd(cfg, qkv, g3, o_pre, d_on, rtot, name, xfer=None):
    LP, NH, tq = cfg.LP, cfg.NH, cfg.tq
    nq, R = LP // tq, tq // HD
    scale = HD ** -0.5
    cq, ck, cv = _attn_specs(cfg, 0)

    def body(q_ref, k_ref, v_ref, g_ref, o_ref, don_ref, rtot_ref, dq_ref, dk_ref, dv_ref, dg_ref,
             dk_acc, dv_acc):
        i = pl.program_id(1)

        @pl.when(i == 0)
        def _():
            dk_acc[...] = jnp.zeros_like(dk_acc)
            dv_acc[...] = jnp.zeros_like(dv_acc)
            dg_ref[...] = jnp.zeros_like(dg_ref)

        q = q_ref[...]
        d_o, dg = _head_norm_bwd(o_ref[...], g_ref[0], don_ref[...])
        dg_ref[0] += dg
        do_b = d_o.astype(MXU)
        rt = rtot_ref[0]
        masks = _diag_masks(tq, True)
        m_le = _tri(lambda a, b: a <= b)
        m_lt = _tri(lambda a, b: a < b)

        def group(j0, n, carry, mk):
            dq, lc, pc = carry
            masked = mk is not None
            offs = [pl.multiple_of((j0 + kk) * HD, HD) for kk in range(n)]
            zs = [lax.dot_general(q, k_ref[pl.ds(offs[kk], HD), :], NT, preferred_element_type=F32) * scale
                  for kk in range(n)]
            das = [lax.dot_general(do_b, v_ref[pl.ds(offs[kk], HD), :], NT, preferred_element_type=F32)
                   for kk in range(n)]
            ls, lks, tri1 = [], [], []
            for kk in range(n):
                sp = _softplus(zs[kk])
                lk = jnp.where(mk[kk], -sp, 0.0) if masked else -sp
                tri1.append(_split_dot(lk, m_le, LK_PIECES))
                lks.append(lk)
                ls.append(zs[kk] - sp)
            ggs, a_bs, tri2 = [], [], []
            for kk in range(n):
                a = jnp.exp(ls[kk] + ((rt - lc) - tri1[kk]))
                if masked:
                    a = jnp.where(mk[kk], a, 0.0)
                gg = a * das[kk]
                tri2.append(_split_dot(gg, m_lt, 2))
                ggs.append(gg)
                a_bs.append(a.astype(MXU))
                lc = lc + jnp.sum(lks[kk], axis=1, keepdims=True)
            for kk in range(n):
                sig = jnp.exp(ls[kk])
                dz = (ggs[kk] * (1.0 - sig) - sig * (pc + tri2[kk])) * scale
                if masked:
                    dz = jnp.where(mk[kk], dz, 0.0)
                dz_b = dz.astype(MXU)
                dq = dq + jnp.dot(dz_b, k_ref[pl.ds(offs[kk], HD), :], preferred_element_type=F32)
                dk_acc[pl.ds(offs[kk], HD), :] += lax.dot_general(dz_b, q, TN, preferred_element_type=F32)
                dv_acc[pl.ds(offs[kk], HD), :] += lax.dot_general(a_bs[kk], do_b, TN, preferred_element_type=F32)
                pc = pc + jnp.sum(ggs[kk], axis=1, keepdims=True)
            return dq, lc, pc

        zc = jnp.zeros((tq, 1), F32)
        dq, _, _ = _walk_groups(i, R, group, (jnp.zeros((tq, HD), F32), zc, zc), masks, False)
        dq_ref[...] = dq.astype(MXU)

        @pl.when(i == nq - 1)
        def _():
            dk_ref[...] = dk_acc[...].astype(MXU)
            dv_ref[...] = dv_acc[...].astype(MXU)

    blk = pl.BlockSpec((tq, HD), lambda h, i: (i, h))
    full = pl.BlockSpec((LP, HD), lambda h, i: (0, h))
    gspec = pl.BlockSpec((1, 1, HD), lambda h, i: (h, 0, 0))
    return call_with_exchange(
        body, name, (NH, nq),
        [pl.BlockSpec((tq, HD), lambda h, i: (i, cq + h)),
         pl.BlockSpec((LP, HD), lambda h, i: (0, ck + h)),
         pl.BlockSpec((LP, HD), lambda h, i: (0, cv + h)),
         gspec, blk, blk, pl.BlockSpec((1, tq, 1), lambda h, i: (h, i, 0))],
        (blk, full, full, gspec),
        (jax.ShapeDtypeStruct((LP, cfg.WG), MXU),) * 3 + (jax.ShapeDtypeStruct((NH, 1, HD), F32),),
        [pltpu.VMEM((LP, HD), F32), pltpu.VMEM((LP, HD), F32)],
        (qkv, qkv, qkv, g3, o_pre, d_on, rtot), xfer)


def fox_fwd(cfg, qkv, g3, c_col, c_row, name, xfer=None):
    LP, NH, tq = cfg.LP, cfg.NH, cfg.tq
    nq, R = LP // tq, tq // HD
    scale = HD ** -0.5
    cq, ck, cv = _attn_specs(cfg, 1)

    def body(q_ref, k_ref, v_ref, g_ref, ccol_ref, crow_ref, opre_ref, on_ref, lse_ref):
        i = pl.program_id(1)
        q = q_ref[...]
        cq_b = jnp.broadcast_to(ccol_ref[0], (tq, HD))
        masks = _diag_masks(tq, False)

        def group(j0, n, carry, mk):
            acc, m, l = carry
            ss, offs = [], []
            for kk in range(n):
                j = j0 + kk
                off = pl.multiple_of(j * HD, HD)
                s = (lax.dot_general(q, k_ref[pl.ds(off, HD), :], NT, preferred_element_type=F32) * scale
                     + (cq_b - crow_ref[0, pl.ds(j, 1), :]))
                ss.append(s if mk is None else jnp.where(mk[kk], s, NEG))
                offs.append(off)
            mx = jnp.max(ss[0], axis=1, keepdims=True)
            for s in ss[1:]:
                mx = jnp.maximum(mx, jnp.max(s, axis=1, keepdims=True))
            m_new = jnp.maximum(m, mx)
            alpha = jnp.exp(m - m_new)
            acc, l = alpha * acc, alpha * l
            m_b = jnp.broadcast_to(m_new, (tq, HD))
            for s, off in zip(ss, offs):
                p = jnp.exp(s - m_b)
                l = l + jnp.sum(p, axis=1, keepdims=True)
                acc = acc + _split_dot(p, v_ref[pl.ds(off, HD), :], 2)
            return acc, m_new, l

        carry = (jnp.zeros((tq, HD), F32), jnp.full((tq, 1), NEG, F32), jnp.zeros((tq, 1), F32))
        acc, m, l = _walk_groups(i, R, group, carry, masks, False)
        o = acc / l
        opre_ref[...] = o
        on_ref[...] = _head_norm_fwd(o, g_ref[0]).astype(MXU)
        lse_ref[0] = m + jnp.log(l)

    return call_with_exchange(
        body, name, (NH, nq),
        [pl.BlockSpec((tq, HD), lambda h, i: (i, cq + h)),
         pl.BlockSpec((LP, HD), lambda h, i: (0, ck + h)),
         pl.BlockSpec((LP, HD), lambda h, i: (0, cv + h)),
         pl.BlockSpec((1, 1, HD), lambda h, i: (h, 0, 0)),
         pl.BlockSpec((1, tq, 1), lambda h, i: (h, i, 0)),
         pl.BlockSpec((1, LP // HD, HD), lambda h, i: (h, 0, 0))],
        (pl.BlockSpec((tq, HD), lambda h, i: (i, h)),
         pl.BlockSpec((tq, HD), lambda h, i: (i, h)),
         pl.BlockSpec((1, tq, 1), lambda h, i: (h, i, 0))),
        (jax.ShapeDtypeStruct((LP, cfg.WG), F32), jax.ShapeDtypeStruct((LP, cfg.WG), MXU),
         jax.ShapeDtypeStruct((NH, LP, 1), F32)),
        [], (qkv, qkv, qkv, g3, c_col, c_row), xfer)


def fox_bwd(cfg, qkv, g3, c_col, c_row, o_pre, d_on, lse, name, xfer=None):
    LP, NH, tq = cfg.LP, cfg.NH, cfg.tq
    nq, R = LP // tq, tq // HD
    scale = HD ** -0.5
    cq, ck, cv = _attn_specs(cfg, 1)

    def body(q_ref, k_ref, v_ref, g_ref, ccol_ref, crow_ref, o_ref, don_ref, lse_ref,
             dq_ref, dk_ref, dv_ref, dg_ref, dc_ref, dk_acc, dv_acc):
        i = pl.program_id(1)

        @pl.when(i == 0)
        def _():
            dk_acc[...] = jnp.zeros_like(dk_acc)
            dv_acc[...] = jnp.zeros_like(dv_acc)
            dg_ref[...] = jnp.zeros_like(dg_ref)
            dc_ref[...] = jnp.zeros_like(dc_ref)

        q = q_ref[...]
        ov = o_ref[...]
        d_o, dg = _head_norm_bwd(ov, g_ref[0], don_ref[...])
        dg_ref[0] += dg
        do_b = d_o.astype(MXU)
        delta = jnp.sum(do_b.astype(F32) * ov, axis=1, keepdims=True)
        cqv, lsev = ccol_ref[0], lse_ref[0]
        cl = cqv - lsev
        masks = _diag_masks(tq, False)

        def group(j0, n, dq, mk):
            offs = [pl.multiple_of((j0 + kk) * HD, HD) for kk in range(n)]
            ss = [lax.dot_general(q, k_ref[pl.ds(offs[kk], HD), :], NT, preferred_element_type=F32) * scale
                  + (cl - crow_ref[0, pl.ds(j0 + kk, 1), :]) for kk in range(n)]
            dps = [lax.dot_general(do_b, v_ref[pl.ds(offs[kk], HD), :], NT, preferred_element_type=F32)
                   for kk in range(n)]
            for kk in range(n):
                p = jnp.exp(ss[kk])
                if mk is not None:
                    p = jnp.where(mk[kk], p, 0.0)
                ds = p * (dps[kk] - delta)
                dc_ref[0, pl.ds(j0 + kk, 1), :] -= jnp.sum(ds, axis=0, keepdims=True)
                ds_b = (ds * scale).astype(MXU)
                dk_acc[pl.ds(offs[kk], HD), :] += lax.dot_general(ds_b, q, TN, preferred_element_type=F32)
                dv_acc[pl.ds(offs[kk], HD), :] += lax.dot_general(p.astype(MXU), do_b, TN,
                                                                  preferred_element_type=F32)
                dq = dq + jnp.dot(ds_b, k_ref[pl.ds(offs[kk], HD), :], preferred_element_type=F32)
            return dq

        dq = _walk_groups(i, R, group, jnp.zeros((tq, HD), F32), masks, False)
        dq_ref[...] = dq.astype(MXU)

        @pl.when(i == nq - 1)
        def _():
            dk_ref[...] = dk_acc[...].astype(MXU)
            dv_ref[...] = dv_acc[...].astype(MXU)

    blk = pl.BlockSpec((tq, HD), lambda h, i: (i, h))
    full = pl.BlockSpec((LP, HD), lambda h, i: (0, h))
    gspec = pl.BlockSpec((1, 1, HD), lambda h, i: (h, 0, 0))
    col = pl.BlockSpec((1, tq, 1), lambda h, i: (h, i, 0))
    rowv = pl.BlockSpec((1, LP // HD, HD), lambda h, i: (h, 0, 0))
    return call_with_exchange(
        body, name, (NH, nq),
        [pl.BlockSpec((tq, HD), lambda h, i: (i, cq + h)),
         pl.BlockSpec((LP, HD), lambda h, i: (0, ck + h)),
         pl.BlockSpec((LP, HD), lambda h, i: (0, cv + h)),
         gspec, col, rowv, blk, blk, col],
        (blk, full, full, gspec, rowv),
        (jax.ShapeDtypeStruct((LP, cfg.WG), MXU),) * 3
        + (jax.ShapeDtypeStruct((NH, 1, HD), F32), jax.ShapeDtypeStruct((NH, LP // HD, HD), F32)),
        [pltpu.VMEM((LP, HD), F32), pltpu.VMEM((LP, HD), F32)],
        (qkv, qkv, qkv, g3, c_col, c_row, o_pre, d_on, lse), xfer)


def _row_tile(rows, cols, n_arrays):
    budget = 24 * 1024 * 1024 // (2 * n_arrays * cols * 4)
    cap = max(8, min(rows, budget // 8 * 8))
    div = _tile(rows, cap, 8)
    return div if div <= cap and div * 4 >= cap else cap


def sum_slots(recvs, name, swap=None):
    S, rows, cols = recvs[0].shape
    nl = len(recvs)
    tr = _tile(rows, _row_tile(rows, cols, nl * S + 1), 8)
    nb = rows // tr

    def body(*refs):
        o_ref = refs[nl + (swap is not None)]
        layer, blk = pl.program_id(0), pl.program_id(1)
        if swap is not None:
            x, y, c = _coords()
            cp = pltpu.make_async_remote_copy(src_ref=refs[nl], dst_ref=refs[nl + 2], send_sem=refs[nl + 3],
                                              recv_sem=refs[nl + 4], device_id=(x, y, 1 - c), device_id_type=MESH)

            @pl.when((layer == 0) & (blk == 0))
            def _():
                cp.start()

        for ll in range(nl):
            @pl.when(layer == ll)
            def _(r_ref=refs[ll]):
                acc = r_ref[0].astype(F32)
                for s in range(1, S):
                    acc = acc + r_ref[s].astype(F32)
                o_ref[...] = acc

        if swap is not None:
            @pl.when((layer == nl - 1) & (blk == nb - 1))
            def _():
                cp.wait()

    def spec(ll):
        return pl.BlockSpec((S, tr, cols), lambda l, i: (0, jnp.where(l == ll, i, jnp.where(l < ll, 0, nb - 1)), 0))

    any_spec = pl.BlockSpec(memory_space=pl.ANY)
    out = jax.ShapeDtypeStruct((nl * rows, cols), F32)
    out_spec = pl.BlockSpec((tr, cols), lambda l, i: (l * nb + i, 0))
    extra = swap is not None
    return pl.pallas_call(
        body, name=name, grid=(nl, nb),
        out_shape=(out, jax.ShapeDtypeStruct(swap.shape, swap.dtype)) if extra else out,
        in_specs=[spec(ll) for ll in range(nl)] + ([any_spec] if extra else []),
        out_specs=(out_spec, any_spec) if extra else out_spec,
        scratch_shapes=[pltpu.SemaphoreType.DMA, pltpu.SemaphoreType.DMA] if extra else [],
        compiler_params=_cp(("arbitrary", "arbitrary")))(*recvs, *([swap] if extra else []))


def adamw(w, m, v, g_parts, name):
    rows, cols = w.shape
    npart = len(g_parts)
    tr = _row_tile(rows, cols, 7 + npart)
    c1 = 1.0 - ADAM_B1 ** ADAM_STEP
    c2 = 1.0 - ADAM_B2 ** ADAM_STEP

    def body(*refs):
        w_ref, m_ref, v_ref = refs[:3]
        g_refs = refs[3:3 + npart]
        g_out, d_out, m_out, v_out = refs[3 + npart:]
        g = g_refs[0][...]
        for r in g_refs[1:]:
            g = g + r[...]
        g_out[...] = g
        mn = ADAM_B1 * m_ref[...] + (1.0 - ADAM_B1) * g
        vn = ADAM_B2 * v_ref[...] + (1.0 - ADAM_B2) * (g * g)
        m_out[...] = mn
        v_out[...] = vn
        d_out[...] = -ADAM_LR * ((mn / c1) / (jnp.sqrt(vn / c2) + ADAM_EPS) + ADAM_WD * w_ref[...])

    blk = pl.BlockSpec((tr, cols), lambda i: (i, 0))
    return pl.pallas_call(
        body, name=name, out_shape=(jax.ShapeDtypeStruct((rows, cols), F32),) * 4,
        grid=(pl.cdiv(rows, tr),), in_specs=[blk] * (3 + npart), out_specs=(blk,) * 4,
        compiler_params=_cp(("parallel",)))(w, m, v, *g_parts)


def _coords():
    return lax.axis_index("x"), lax.axis_index("y"), lax.axis_index("c")


def chip_exchange(arrs, scatter, name):
    n = len(arrs)

    def body(*refs):
        ins, outs, sems = refs[:n], refs[n:2 * n], refs[2 * n:]
        _xfer_start(ins, outs, sems, scatter)
        _xfer_forward(ins, outs, sems, scatter)
        _xfer_finish(ins, outs, sems, scatter)

    out_shape, scratch = _xfer_shapes(arrs, scatter)
    any_spec = pl.BlockSpec(memory_space=pl.ANY)
    return pl.pallas_call(body, name=name, out_shape=out_shape, in_specs=[any_spec] * n, out_specs=(any_spec,) * n,
                          scratch_shapes=scratch)(*arrs)


def _xfer_shapes(arrs, scatter):
    n = len(arrs)
    shapes = [a.shape[1:] if scatter else a.shape for a in arrs]
    out_shape = tuple(jax.ShapeDtypeStruct((NCHIP,) + tuple(s), a.dtype) for s, a in zip(shapes, arrs))
    scratch = [pltpu.SemaphoreType.DMA((n, 6)), pltpu.SemaphoreType.DMA((n, 6)), pltpu.SemaphoreType.DMA((n,))]
    return out_shape, scratch


def _xfer_copies(ins, outs, sems, scatter):
    send_sems, recv_sems, local_sems = sems
    n = len(ins)
    x, y, c = _coords()
    me = 2 * x + y
    peers = [(1 - x, y), (x, 1 - y), (1 - x, 1 - y)]

    def rdma(src, dst, a, j, dev):
        return pltpu.make_async_remote_copy(src_ref=src, dst_ref=dst, send_sem=send_sems.at[a, j],
                                            recv_sem=recv_sems.at[a, j], device_id=dev, device_id_type=MESH)

    def half(ref, hc):
        hr = ref.shape[0] // 2
        return ref.at[pl.ds(pl.multiple_of(hc * hr, 8), hr)]

    local, sends, recvs, fwds, fwd_recvs = [], [], [], [], []
    for a in range(n):
        local.append(pltpu.make_async_copy(ins[a].at[me] if scatter else ins[a], outs[a].at[me], local_sems.at[a]))
        for k, (px, py) in enumerate(peers):
            there = 2 * px + py
            if scatter:
                sends.append(rdma(ins[a].at[there], outs[a].at[me], a, k, (px, py, c)))
                recvs.append(rdma(ins[a].at[me], outs[a].at[there], a, k, (px, py, c)))
            else:
                mine, landed = half(outs[a].at[me], c), half(outs[a].at[there], c)
                sends.append(rdma(half(ins[a], c), mine, a, k, (px, py, c)))
                recvs.append(rdma(half(ins[a], c), landed, a, k, (px, py, c)))
                fwds.append(rdma(landed, landed, a, 3 + k, (x, y, 1 - c)))
                other = half(outs[a].at[there], 1 - c)
                fwd_recvs.append(rdma(other, other, a, 3 + k, (x, y, 1 - c)))
    return local, sends, recvs, fwds, fwd_recvs


def _xfer_start(ins, outs, sems, scatter):
    local, sends, _, _, _ = _xfer_copies(ins, outs, sems, scatter)
    for cp in local + sends:
        cp.start()


def _xfer_forward(ins, outs, sems, scatter):
    _, _, recvs, fwds, _ = _xfer_copies(ins, outs, sems, scatter)
    for r, f in zip(recvs, fwds):
        r.wait_recv()
        f.start()


def _xfer_finish(ins, outs, sems, scatter):
    local, sends, recvs, fwds, fwd_recvs = _xfer_copies(ins, outs, sems, scatter)
    for cp in (recvs if scatter else fwd_recvs):
        cp.wait_recv()
    for cp in sends + fwds:
        cp.wait_send()
    for cp in local:
        cp.wait()


def call_with_exchange(core, name, grid, in_specs, out_specs, out_shape, scratch, args, xfer):
    n_in, n_out, n_scr = len(in_specs), len(out_specs), len(scratch)
    if xfer is None:
        res = pl.pallas_call(core, name=name, out_shape=out_shape, grid=grid, in_specs=in_specs,
                             out_specs=out_specs, scratch_shapes=scratch,
                             compiler_params=_cp(("arbitrary",) * len(grid)))(*args)
        return res, ()
    arrs, scatter = xfer
    nx = len(arrs)
    x_shape, x_scratch = _xfer_shapes(arrs, scatter)

    def body(*refs):
        a, xi = refs[:n_in], refs[n_in:n_in + nx]
        o, xo = refs[n_in + nx:n_in + nx + n_out], refs[n_in + nx + n_out:n_in + 2 * nx + n_out]
        rest = refs[n_in + 2 * nx + n_out:]
        scr, sems = rest[:n_scr], rest[n_scr:]
        first, late, last = None, None, None
        for d, g in enumerate(grid):
            pid = pl.program_id(d)
            f, m, l = pid == 0, pid == ((3 * g) // 4 if d == 0 else 0), pid == g - 1
            first = f if first is None else first & f
            late = m if late is None else late & m
            last = l if last is None else last & l

        @pl.when(first)
        def _():
            _xfer_start(xi, xo, sems, scatter)

        core(*a, *o, *scr)

        if not scatter:
            @pl.when(late)
            def _():
                _xfer_forward(xi, xo, sems, scatter)

        @pl.when(last)
        def _():
            _xfer_finish(xi, xo, sems, scatter)

    any_spec = pl.BlockSpec(memory_space=pl.ANY)
    res = pl.pallas_call(
        body, name=name, out_shape=tuple(out_shape) + tuple(x_shape), grid=grid,
        in_specs=list(in_specs) + [any_spec] * nx, out_specs=tuple(out_specs) + (any_spec,) * nx,
        scratch_shapes=list(scratch) + x_scratch,
        compiler_params=_cp(("arbitrary",) * len(grid)))(*args, *arrs)
    return res[:n_out], res[n_out:]


def sibling_exchange(arrs, name):
    n = len(arrs)

    def body(*refs):
        ins, outs = refs[:n], refs[n:2 * n]
        send_sems, recv_sems = refs[2 * n:]
        x, y, c = _coords()
        cps = [pltpu.make_async_remote_copy(src_ref=ins[a], dst_ref=outs[a], send_sem=send_sems.at[a],
                                            recv_sem=recv_sems.at[a], device_id=(x, y, 1 - c),
                                            device_id_type=MESH) for a in range(n)]
        for cp in cps:
            cp.start()
        for cp in cps:
            cp.wait()

    any_spec = pl.BlockSpec(memory_space=pl.ANY)
    return pl.pallas_call(
        body, name=name, out_shape=tuple(jax.ShapeDtypeStruct(a.shape, a.dtype) for a in arrs),
        in_specs=[any_spec] * n, out_specs=(any_spec,) * n,
        scratch_shapes=[pltpu.SemaphoreType.DMA((n,)), pltpu.SemaphoreType.DMA((n,))],
    )(*arrs)


def all_reduce_small(pack, name):
    R = pack.shape[0]

    def body(p_ref, o_ref, slots, send_sems, recv_sems):
        x, y, c = _coords()
        me = 4 * x + 2 * y + c
        slots[me] = p_ref[...]
        cps = []
        for k in range(1, 8):
            kx, ky, kc = (k >> 2) & 1, (k >> 1) & 1, k & 1
            peer = ((1 - x) if kx else x, (1 - y) if ky else y, (1 - c) if kc else c)
            cp = pltpu.make_async_remote_copy(src_ref=p_ref, dst_ref=slots.at[me], send_sem=send_sems.at[k],
                                              recv_sem=recv_sems.at[k], device_id=peer, device_id_type=MESH)
            cp.start()
            cps.append((cp, peer))
        for k, (cp, peer) in enumerate(cps, start=1):
            frm = 4 * peer[0] + 2 * peer[1] + peer[2]
            pltpu.make_async_remote_copy(src_ref=p_ref, dst_ref=slots.at[frm], send_sem=send_sems.at[k],
                                         recv_sem=recv_sems.at[k], device_id=peer, device_id_type=MESH).wait_recv()
        for cp, _ in cps:
            cp.wait_send()
        acc = slots[0]
        for s in range(1, 8):
            acc = acc + slots[s]
        o_ref[...] = acc

    vm = pl.BlockSpec(memory_space=pltpu.VMEM)
    return pl.pallas_call(
        body, name=name, out_shape=jax.ShapeDtypeStruct((R, LANES), F32), in_specs=[vm], out_specs=vm,
        scratch_shapes=[pltpu.VMEM((8, R, LANES), F32), pltpu.SemaphoreType.DMA((8,)),
                        pltpu.SemaphoreType.DMA((8,))],
        compiler_params=pltpu.CompilerParams(vmem_limit_bytes=VMEM_LIMIT),
    )(pack)


def _gate_up(w):
    r, c = w.shape
    return w.reshape(r, 2, c // 2).transpose(1, 0, 2)


def _gate_up_inv(w):
    return w.transpose(1, 0, 2).reshape(w.shape[1], -1)


def _pack(arrs):
    parts = []
    for a in arrs:
        f = a.reshape(-1).astype(F32)
        parts.append(jnp.pad(f, (0, -f.shape[0] % 1024)).reshape(-1, LANES))
    return jnp.concatenate(parts, axis=0)


def _unpack(p, shapes):
    out, r = [], 0
    for s in shapes:
        n = math.prod(s)
        nr = (n + 1023) // 1024 * 8
        out.append(p[r:r + nr].reshape(-1)[:n].reshape(s))
        r += nr
    return out


def _vec(g):
    return g.reshape(1, -1)


class _LocalWeights:
    def __init__(self, cfg, wf):
        self.cfg, self.wf, self.grads = cfg, wf, {}

    def w_in(self, l):
        return self.wf['w_in'][l]

    def fwd_exchanges(self, l):
        return None, None

    def rest(self, l, got_sb, got_fox):
        w_up = self.wf['w_up'][l]
        return dict(w_out=self.wf['w_out'][l], w_down=self.wf['w_down'][l],
                    w_up=w_up.reshape(w_up.shape[0], NCHIP, -1).transpose(1, 0, 2))

    def bwd_exchanges(self, l, g):
        return None, None

    def in_dx_exchange(self, l, g):
        return None

    def bwd_done(self, l, g, got_sb, got_fox, got_in):
        self.grads[l] = g


def _w_in_parts(cfg, w_in):
    w_f = jnp.pad(w_in[:, cfg.NQKV:], ((0, 0), (0, LANES - cfg.NH)))
    return dict(w_qkv=w_in[:, :cfg.NQKV], w_f=w_f, w_in_ext=jnp.concatenate([w_in[:, :cfg.NQKV], w_f], axis=1))


def _layer_fwd(cfg, l, h, u, wl, io):
    tag = f"l{l}"
    qkv = matmul(u, wl['w_qkv'], 'nn', MXU, f"{tag}_qkv", tn_cap=1024)
    f_logit = matmul(u, wl['w_f'], 'nn', F32, f"{tag}_fproj")
    cpre = fox_gate_fwd(cfg, f_logit, wl['b_pad'], f"{tag}_gate_fwd")
    c_heads = cpre[:, :cfg.NH].T
    c_col = c_heads[:, :, None]
    c_row = c_heads.reshape(cfg.NH, cfg.LP // HD, HD)
    x_sb, x_fox = io.fwd_exchanges(l)
    (o_sb, on_sb, rtot), got_sb = sb_fwd(cfg, qkv, wl['g_sb'], f"{tag}_sb_fwd", x_sb)
    (o_fx, on_fx, lse), got_fox = fox_fwd(cfg, qkv, wl['g_fox'], c_col, c_row, f"{tag}_fox_fwd", x_fox)
    wl.update(io.rest(l, got_sb, got_fox))
    mixin = jnp.concatenate([on_sb, on_fx], axis=1)
    mix = matmul(mixin, wl['w_out'], 'nn', F32, f"{tag}_out")
    h1, u2 = resid_norm(cfg, mix, h, wl['g_mix_post'], wl['g_ffn_pre'], f"{tag}_mixres")
    a = matmul(u2, wl['w_up'], 'nn', F32, f"{tag}_up", tn_cap=1408, out_split=2)
    act = conv_act_fwd(cfg, a, wl['conv_w'], wl['conv_b'], f"{tag}_conv_fwd")
    ff = matmul(act, wl['w_down'], 'nn', F32, f"{tag}_down", tm_cap=704)
    h2, u_next = resid_norm(cfg, ff, h1, wl['g_ffn_post'], wl['g_next'], f"{tag}_ffnres")
    saved = dict(h=h, u=u, qkv=qkv, f_logit=f_logit, c_col=c_col, c_row=c_row, o_sb=o_sb, o_fx=o_fx, rtot=rtot,
                 lse=lse, mixin=mixin, mix=mix, h1=h1, u2=u2, a=a, act=act, ff=ff)
    return h2, u_next, saved


def _layer_bwd(cfg, l, dh2, wl, sv, io):
    tag = f"l{l}"
    g = {}
    d_ff, g['g_ffn_post'] = norm_bwd(cfg, sv['ff'], wl['g_ffn_post'], dh2, None, MXU, f"{tag}_ffnpost_bwd")
    d_act = matmul(d_ff, wl['w_down'], 'nt', F32, f"{tag}_down_dx")
    g['w_down'] = matmul(sv['act'], d_ff, 'tn', MXU, f"{tag}_down_dw", tm_cap=704, tk_cap=4224)
    d_a, d_conv = conv_act_bwd(cfg, sv['a'], d_act, wl['conv_w'], wl['conv_b'], f"{tag}_conv_bwd")
    g['conv'] = d_conv
    g['w_up'] = matmul(sv['u2'], d_a, 'tn', MXU, f"{tag}_up_dw", tm_cap=512, tn_cap=1408, tk_cap=4224,
                       out_split=NCHIP)
    du2 = matmul(d_a, wl['w_up'], 'nt', F32, f"{tag}_up_dx", tm_cap=704, tk_cap=5632)
    dh1, g['g_ffn_pre'] = norm_bwd(cfg, sv['h1'], wl['g_ffn_pre'], du2, dh2, F32, f"{tag}_ffnpre_bwd")
    d_mix, g['g_mix_post'] = norm_bwd(cfg, sv['mix'], wl['g_mix_post'], dh1, None, MXU, f"{tag}_mixpost_bwd")
    d_mixin = matmul(d_mix, wl['w_out'], 'nt', F32, f"{tag}_out_dx")
    g['w_out'] = matmul(sv['mixin'], d_mix, 'tn', MXU, f"{tag}_out_dw", tm_cap=1024, tk_cap=4224)
    WG = cfg.WG
    x_sb, x_fox = io.bwd_exchanges(l, g)
    (dq_s, dk_s, dv_s, g['g_sb']), got_sb = sb_bwd(cfg, sv['qkv'], wl['g_sb'], sv['o_sb'], d_mixin[:, :WG],
                                                   sv['rtot'], f"{tag}_sb_bwd", x_sb)
    (dq_f, dk_f, dv_f, g['g_fox'], dc_row), got_fox = fox_bwd(cfg, sv['qkv'], wl['g_fox'], sv['c_col'], sv['c_row'],
                                                              sv['o_fx'], d_mixin[:, WG:], sv['lse'],
                                                              f"{tag}_fox_bwd", x_fox)
    dc = jnp.pad(dc_row.reshape(cfg.NH, cfg.LP).T, ((0, 0), (0, LANES - cfg.NH)))
    d_f, g['b_f'] = fox_gate_bwd(cfg, dc, sv['f_logit'], wl['b_pad'], f"{tag}_gate_bwd")
    d_proj = jnp.concatenate([dq_s, dk_s, dv_s, dq_f, dk_f, dv_f, d_f], axis=1)
    g['w_in_ext'] = matmul(sv['u'], d_proj, 'tn', MXU, f"{tag}_in_dw", tm_cap=1024, tn_cap=896, tk_cap=4224)
    x_in = io.in_dx_exchange(l, g)
    du = matmul(d_proj, wl['w_in_ext'], 'nt', F32, f"{tag}_in_dx", tm_cap=704, xfer=x_in)
    du, got_in = du if x_in is not None else (du, None)
    dh, g['g_mix_pre'] = norm_bwd(cfg, sv['h'], wl['g_mix_pre'], du, dh1, F32, f"{tag}_mixpre_bwd")
    io.bwd_done(l, g, got_sb, got_fox, got_in)
    return dh


def _local_step(cfg, h0, target_p, wf, io=None):
    io = _LocalWeights(cfg, wf) if io is None else io
    layers = []
    for l in range(cfg.DEPTH):
        layers.append(dict(
            b_pad=jnp.pad(wf['b_f'][l], (0, LANES - cfg.NH)).reshape(1, LANES),
            g_sb=wf['g_sb'][l][:, None, :], g_fox=wf['g_fox'][l][:, None, :],
            conv_w=_gate_up(wf['conv_w'][l]), conv_b=wf['conv_b'][l].reshape(2, 1, cfg.F),
            g_mix_pre=_vec(wf['g_mix_pre'][l]), g_mix_post=_vec(wf['g_mix_post'][l]),
            g_ffn_pre=_vec(wf['g_ffn_pre'][l]), g_ffn_post=_vec(wf['g_ffn_post'][l]),
            g_next=_vec(wf['g_mix_pre'][(l + 1) % cfg.DEPTH])))
    h = h0
    u = pre_norm(cfg, h0, layers[0]['g_mix_pre'], "l0_prenorm")
    saved = []
    for l in range(cfg.DEPTH):
        layers[l].update(_w_in_parts(cfg, io.w_in(l)))
        h, u, sv = _layer_fwd(cfg, l, h, u, layers[l], io)
        saved.append(sv)
    dh, loss_blk = loss_head(cfg, h, target_p, "loss_head")
    for l in reversed(range(cfg.DEPTH)):
        dh = _layer_bwd(cfg, l, dh, layers[l], saved[l], io)
    return loss_blk, dh, [io.grads[l] for l in range(cfg.DEPTH)]


def _cols(g):
    return g.transpose(1, 0, 2).reshape(g.shape[1], -1)


def _rows(g):
    return g.reshape(-1, g.shape[2])


def _send_cols(g):
    r, c = g.shape
    return g.reshape(r, NCHIP, c // NCHIP).transpose(1, 0, 2).astype(MXU)


def _send_rows(g):
    r, c = g.shape
    return g.reshape(NCHIP, r // NCHIP, c).astype(MXU)


class _StreamedWeights:
    def __init__(self, cfg, w):
        self.cfg = cfg
        self.shard = {n: [w[n][l].astype(MXU) for l in range(cfg.DEPTH)] for n in BIG}
        self.grads, self.recv = {}, {n: [None] * cfg.DEPTH for n in BIG}
        n_cw = cfg.DEPTH * 3
        conv_w = jnp.pad(w['conv_w'].reshape(n_cw, -1), ((0, -n_cw % 16), (0, 0)))
        got = chip_exchange([self.shard['w_in'][0], w['meta'], conv_w], False, "gather_first")
        self.win = {0: _cols(got[0])}
        self.meta = _cols(got[1])
        self.conv_w = _cols(got[2])[:n_cw].reshape(cfg.DEPTH, 3, cfg.F2)
        self.pending = None

    def w_in(self, l):
        return self.win[l]

    def fwd_exchanges(self, l):
        s = self.shard
        fox = [s['w_down'][l]] + ([s['w_in'][l + 1]] if l + 1 < self.cfg.DEPTH else [])
        return ([s['w_out'][l], s['w_up'][l]], False), (fox, False)

    def rest(self, l, got_sb, got_fox):
        if l + 1 < self.cfg.DEPTH:
            self.win[l + 1] = _cols(got_fox[1])
        return dict(w_out=_rows(got_sb[0]), w_up=got_sb[1], w_down=_rows(got_fox[0]))

    def bwd_exchanges(self, l, g):
        sb = [g['w_up']] + ([] if self.pending is None else [self.pending])
        self.pending = None
        return (sb, True), ([_send_rows(g['w_down']), _send_rows(g['w_out'])], True)

    def in_dx_exchange(self, l, g):
        send = _send_cols(g['w_in_ext'][:, :self.cfg.N_IN])
        if l == 0:
            return [send], True
        self.pending = send
        return None

    def bwd_done(self, l, g, got_sb, got_fox, got_in):
        self.grads[l] = g
        self.recv['w_up'][l] = got_sb[0]
        if len(got_sb) > 1:
            self.recv['w_in'][l + 1] = got_sb[1]
        self.recv['w_down'][l], self.recv['w_out'][l] = got_fox
        if got_in is not None:
            self.recv['w_in'][l] = got_in[0]


def _step(cfg, x, w, target, m, v):
    xi, yi, ci = _coords()
    chip = 2 * xi + yi
    D, LP, L, NM = cfg.D, cfg.LP, cfg.L, cfg.NMETA
    DEP = cfg.DEPTH
    io = _StreamedWeights(cfg, w)
    wf = {n: w[n] for n in SMALL}
    meta_full, wf['conv_w'] = io.meta, io.conv_w

    zpad = jnp.zeros((LP - L, D), F32)
    h0 = jnp.concatenate([meta_full, x[0], zpad], axis=0)
    target_p = jnp.concatenate([jnp.zeros((NM, D), F32), target[0], zpad], axis=0)
    loss_blk, dh0, grads = _local_step(cfg, h0, target_p, wf, io)

    def stack(key, shape):
        return jnp.stack([grads[l][key].reshape(shape) for l in range(DEP)])

    conv_g = jnp.stack([_gate_up_inv(grads[l]['conv']) for l in range(DEP)])
    small_g = dict(
        loss=loss_blk[0:1, 0:1], meta=dh0[:NM], g_mix_pre=stack('g_mix_pre', (D,)),
        b_f=stack('b_f', (LANES,))[:, :cfg.NH], g_sb=stack('g_sb', (cfg.NH, HD)), g_fox=stack('g_fox', (cfg.NH, HD)),
        g_mix_post=stack('g_mix_post', (D,)), g_ffn_pre=stack('g_ffn_pre', (D,)),
        conv_w=conv_g[:, 0:3], conv_b=conv_g[:, 3], g_ffn_post=stack('g_ffn_post', (D,)))
    keys = list(small_g)
    red = dict(zip(keys, _unpack(all_reduce_small(_pack([small_g[k] for k in keys]), "reduce_small"),
                                 [small_g[k].shape for k in keys])))
    loss = red['loss'].reshape(())
    red['meta'] = lax.dynamic_slice(red['meta'], (0, chip * (D // NCHIP)), (NM, D // NCHIP))
    red['conv_w'] = lax.dynamic_slice(red['conv_w'], (0, 0, chip * (cfg.F2 // NCHIP)), (DEP, 3, cfg.F2 // NCHIP))

    recv = io.recv
    order = ['w_up', 'w_in', 'w_down', 'w_out']
    part, other = {}, {}
    for k, n in enumerate(order):
        if k == 0:
            part[n] = sum_slots(recv[n], f"sum_{n}")
        else:
            part[n], other[order[k - 1]] = sum_slots(recv[n], f"sum_{n}", swap=part[order[k - 1]])
    other[order[-1]], = sibling_exchange([part[order[-1]]], "sibling_grads")

    outs = {}
    for n in BIG:
        shp = w[n].shape
        s2 = (shp[0] * shp[1], shp[2])
        res = adamw(w[n].reshape(s2), m[n].reshape(s2), v[n].reshape(s2), [part[n], other[n]], f"adamw_{n}")
        outs[n] = [r.reshape(shp) for r in res]

    for n in SMALL:
        shp = w[n].shape
        s2 = (shp[0], math.prod(shp[1:]))
        res = adamw(w[n].reshape(s2), m[n].reshape(s2), v[n].reshape(s2), [red[n].reshape(s2)], f"adamw_{n}")
        outs[n] = [r.reshape(shp) for r in res]

    grad_x = dh0[NM:L][None]
    return (loss, grad_x, *[outs[n][0] for n in WEIGHTS], *[outs[n][1] for n in WEIGHTS],
            *[outs[n][2] for n in WEIGHTS], *[outs[n][3] for n in WEIGHTS])


def kernel(x, meta, g_mix_pre, w_in, b_f, g_sb, g_fox, w_out, g_mix_post, g_ffn_pre, w_up, conv_w, conv_b, w_down, g_ffn_post, loss_target, m_meta, m_g_mix_pre, m_w_in, m_b_f, m_g_sb, m_g_fox, m_w_out, m_g_mix_post, m_g_ffn_pre, m_w_up, m_conv_w, m_conv_b, m_w_down, m_g_ffn_post, v_meta, v_g_mix_pre, v_w_in, v_b_f, v_g_sb, v_g_fox, v_w_out, v_g_mix_post, v_g_ffn_pre, v_w_up, v_conv_w, v_conv_b, v_w_down, v_g_ffn_post):
    w = dict(zip(WEIGHTS, (meta, g_mix_pre, w_in, b_f, g_sb, g_fox, w_out, g_mix_post, g_ffn_pre, w_up, conv_w,
                           conv_b, w_down, g_ffn_post)))
    m = dict(zip(WEIGHTS, (m_meta, m_g_mix_pre, m_w_in, m_b_f, m_g_sb, m_g_fox, m_w_out, m_g_mix_post, m_g_ffn_pre,
                           m_w_up, m_conv_w, m_conv_b, m_w_down, m_g_ffn_post)))
    v = dict(zip(WEIGHTS, (v_meta, v_g_mix_pre, v_w_in, v_b_f, v_g_sb, v_g_fox, v_w_out, v_g_mix_post, v_g_ffn_pre,
                           v_w_up, v_conv_w, v_conv_b, v_w_down, v_g_ffn_post)))
    return _step(PROD, x, w, loss_target, m, v)
```

```python
import functools
import math
from typing import NamedTuple

import jax
import jax.numpy as jnp
from jax import lax
from jax.experimental import pallas as pl
from jax.experimental.pallas import tpu as pltpu

F32 = jnp.float32
MXU = jnp.bfloat16
HD = 128
LANES = 128
EPS = 1e-6
NEG = -1e30
LK_PIECES = 2
ADAM_LR, ADAM_B1, ADAM_B2, ADAM_EPS, ADAM_WD, ADAM_STEP = 0.001, 0.9, 0.999, 1e-08, 0.01, 10
VMEM_LIMIT = 56 * 1024 * 1024
MESH = pl.DeviceIdType.MESH
NCHIP = 4

WEIGHTS = ['meta', 'g_mix_pre', 'w_in', 'b_f', 'g_sb', 'g_fox', 'w_out', 'g_mix_post', 'g_ffn_pre',
           'w_up', 'conv_w', 'conv_b', 'w_down', 'g_ffn_post']
BIG = ['w_in', 'w_out', 'w_up', 'w_down']
SMALL = [n for n in WEIGHTS if n not in BIG]

NT = (((1,), (1,)), ((), ()))
TN = (((0,), (0,)), ((), ()))
NN = (((1,), (0,)), ((), ()))


class Cfg(NamedTuple):
    D: int
    SEQ: int
    NMETA: int
    NH: int
    F: int
    LP: int
    tq: int
    tr: int
    FB: int
    DEPTH: int = 2
    tc: int = 0

    @property
    def conv_rows(self): return self.tc or self.tr
    @property
    def L(self): return self.SEQ + self.NMETA
    @property
    def WG(self): return self.NH * HD
    @property
    def WMIX(self): return 2 * self.WG
    @property
    def NQKV(self): return 6 * self.WG
    @property
    def N_IN(self): return self.NQKV + self.NH
    @property
    def NEXT(self): return self.NQKV + LANES
    @property
    def F2(self): return 2 * self.F


PROD = Cfg(D=2048, SEQ=4096, NMETA=16, NH=8, F=5632, LP=4224, tq=384, tr=384, FB=512, tc=704)


def _tile(n, cap, mult=LANES):
    best = None
    for t in range(mult, min(n, cap) + 1, mult):
        if n % t == 0:
            best = t
    return best if best is not None else n


def _cp(sem):
    return pltpu.CompilerParams(dimension_semantics=sem, vmem_limit_bytes=VMEM_LIMIT)


def _split_dot(x, tri, pieces):
    acc, r = None, x
    for p in range(pieces):
        xp = r.astype(MXU)
        d = jnp.dot(xp, tri, preferred_element_type=F32)
        acc = d if acc is None else acc + d
        if p + 1 < pieces:
            r = r - xp.astype(F32)
    return acc


def _split_dot_left(tri, x, pieces):
    acc, r = None, x
    for p in range(pieces):
        xp = r.astype(MXU)
        d = jnp.dot(tri, xp, preferred_element_type=F32)
        acc = d if acc is None else acc + d
        if p + 1 < pieces:
            r = r - xp.astype(F32)
    return acc


def _softplus(z):
    return jnp.maximum(z, 0.0) + jnp.log(1.0 + jnp.exp(-jnp.abs(z)))


def _shape2(x):
    return tuple(x.shape) if x.ndim == 2 else (x.shape[1], x.shape[0] * x.shape[2])


def _split_width(x):
    return x.shape[-1]


def _spec2(x, rb, cb, idx):
    if len(x.shape) == 2:
        return pl.BlockSpec((rb, cb), idx)
    per = x.shape[2] // cb

    def im(i, j, k):
        ri, ci = idx(i, j, k)
        return ci // per, ri, ci % per

    return pl.BlockSpec((None, rb, cb), im)


def matmul(a, b, mode, out_dtype, name, tm_cap=1408, tn_cap=1024, tk_cap=8192, out_split=None, xfer=None):
    (K, M) = _shape2(a) if mode == 'tn' else _shape2(a)[::-1]
    N = _shape2(b)[0] if mode == 'nt' else _shape2(b)[1]
    n_unit = math.gcd(N // (out_split or 1), _split_width(b) if mode != 'nt' else N)
    k_unit = math.gcd(_split_width(a) if mode != 'tn' else K, _split_width(b) if mode == 'nt' else K)
    tm = _tile(M, tm_cap, LANES if mode == 'tn' else 8)
    tn, tk = _tile(n_unit, tn_cap), _tile(k_unit, tk_cap)
    nk = K // tk
    dn = {'nn': NN, 'nt': NT, 'tn': TN}[mode]

    def body(a_ref, b_ref, o_ref, *scratch):
        d = lax.dot_general(a_ref[...], b_ref[...], dn, preferred_element_type=F32)
        if nk == 1:
            o_ref[...] = d.astype(out_dtype)
        else:
            acc_ref, = scratch
            k = pl.program_id(2)

            @pl.when(k == 0)
            def _():
                acc_ref[...] = d

            @pl.when(k > 0)
            def _():
                acc_ref[...] += d

            @pl.when(k == nk - 1)
            def _():
                o_ref[...] = acc_ref[...].astype(out_dtype)

    a_spec = (_spec2(a, tk, tm, lambda i, j, k: (k, i)) if mode == 'tn'
              else _spec2(a, tm, tk, lambda i, j, k: (i, k)))
    b_spec = (_spec2(b, tn, tk, lambda i, j, k: (j, k)) if mode == 'nt'
              else _spec2(b, tk, tn, lambda i, j, k: (k, j)))
    out = (jax.ShapeDtypeStruct((M, N), out_dtype) if out_split is None
           else jax.ShapeDtypeStruct((out_split, M, N // out_split), out_dtype))
    (res,), got = call_with_exchange(
        body, name, (M // tm, N // tn, nk), [a_spec, b_spec], (_spec2(out, tm, tn, lambda i, j, k: (i, j)),),
        (out,), [] if nk == 1 else [pltpu.VMEM((tm, tn), F32)], (a, b), xfer)
    return res if xfer is None else (res, got)


def _rstd(x):
    return lax.rsqrt(jnp.mean(x * x, axis=-1, keepdims=True) + EPS)


def pre_norm(cfg, h, g, name):
    LP, D, tr = cfg.LP, cfg.D, cfg.tr

    def body(h_ref, g_ref, u_ref):
        x = h_ref[...]
        u_ref[...] = ((x * _rstd(x)) * g_ref[...]).astype(MXU)

    row = pl.BlockSpec((tr, D), lambda i: (i, 0))
    vec = pl.BlockSpec((1, D), lambda i: (0, 0))
    return pl.pallas_call(body, name=name, out_shape=jax.ShapeDtypeStruct((LP, D), MXU), grid=(LP // tr,),
                          in_specs=[row, vec], out_specs=row, compiler_params=_cp(("parallel",)))(h, g)


def resid_norm(cfg, y, h, g_post, g_next, name):
    LP, D, tr = cfg.LP, cfg.D, cfg.tr

    def body(y_ref, h_ref, gp_ref, gn_ref, hn_ref, u_ref):
        yv = y_ref[...]
        hn = h_ref[...] + (yv * _rstd(yv)) * gp_ref[...]
        hn_ref[...] = hn
        u_ref[...] = ((hn * _rstd(hn)) * gn_ref[...]).astype(MXU)

    row = pl.BlockSpec((tr, D), lambda i: (i, 0))
    vec = pl.BlockSpec((1, D), lambda i: (0, 0))
    return pl.pallas_call(
        body, name=name,
        out_shape=(jax.ShapeDtypeStruct((LP, D), F32), jax.ShapeDtypeStruct((LP, D), MXU)),
        grid=(LP // tr,), in_specs=[row, row, vec, vec], out_specs=(row, row),
        compiler_params=_cp(("parallel",)))(y, h, g_post, g_next)


def loss_head(cfg, h, target, name):
    LP, D, tr = cfg.LP, cfg.D, cfg.tr
    lo, hi = cfg.NMETA, cfg.L

    def body(h_ref, t_ref, dh_ref, loss_ref):
        i = pl.program_id(0)
        rows = lax.broadcasted_iota(jnp.int32, (tr, D), 0) + i * tr
        diff = jnp.where((rows >= lo) & (rows < hi), h_ref[...] - t_ref[...], 0.0)
        dh_ref[...] = diff * (1.0 / D)
        part = 0.5 * jnp.sum(jnp.sum(diff * diff, axis=1, keepdims=True), axis=0, keepdims=True) * (1.0 / D)

        @pl.when(i == 0)
        def _():
            loss_ref[...] = jnp.zeros_like(loss_ref)

        loss_ref[...] += jnp.broadcast_to(part, loss_ref.shape)

    row = pl.BlockSpec((tr, D), lambda i: (i, 0))
    return pl.pallas_call(
        body, name=name,
        out_shape=(jax.ShapeDtypeStruct((LP, D), F32), jax.ShapeDtypeStruct((8, LANES), F32)),
        grid=(LP // tr,), in_specs=[row, row],
        out_specs=(row, pl.BlockSpec((8, LANES), lambda i: (0, 0))),
        compiler_params=_cp(("arbitrary",)))(h, target)


def norm_bwd(cfg, x, g, dy, dres, out_dtype, name):
    LP, D, tr = cfg.LP, cfg.D, cfg.tr
    has_res = dres is not None

    def body(*refs):
        if has_res:
            x_ref, g_ref, dy_ref, dres_ref, dx_ref, dg_ref = refs
        else:
            x_ref, g_ref, dy_ref, dx_ref, dg_ref = refs
        xv, dyv = x_ref[...], dy_ref[...]
        r = _rstd(xv)
        xhat = xv * r
        gdy = dyv * g_ref[...]
        dx = r * (gdy - xhat * jnp.mean(gdy * xhat, axis=-1, keepdims=True))
        if has_res:
            dx = dx + dres_ref[...]
        dx_ref[...] = dx.astype(out_dtype)

        @pl.when(pl.program_id(0) == 0)
        def _():
            dg_ref[...] = jnp.zeros_like(dg_ref)

        dg_ref[...] += jnp.sum(dyv * xhat, axis=0, keepdims=True)

    row = pl.BlockSpec((tr, D), lambda i: (i, 0))
    vec = pl.BlockSpec((1, D), lambda i: (0, 0))
    ins = [x, g, dy] + ([dres] if has_res else [])
    return pl.pallas_call(
        body, name=name,
        out_shape=(jax.ShapeDtypeStruct((LP, D), out_dtype), jax.ShapeDtypeStruct((1, D), F32)),
        grid=(LP // tr,), in_specs=[row, vec, row] + ([row] if has_res else []), out_specs=(row, vec),
        compiler_params=_cp(("arbitrary",)))(*ins)


HALO = 8


def _conv3(ext, w, b):
    s1, s2 = pltpu.roll(ext, 1, 0), pltpu.roll(ext, 2, 0)
    return w[0:1, :] * s2 + w[1:2, :] * s1 + w[2:3, :] * ext + b, s1, s2


def conv_act_fwd(cfg, a, cw, cb, name):
    LP, F, FB, tm = cfg.LP, cfg.F, cfg.FB, cfg.conv_rows
    nb = tm // HALO

    def body(a_ref, prev_ref, w_ref, b_ref, act_ref):
        i = pl.program_id(1)
        c = []
        for hh in range(2):
            prev = jnp.where(i > 0, prev_ref[hh], 0.0)
            ext = jnp.concatenate([prev, a_ref[hh]], axis=0)
            c.append(_conv3(ext, w_ref[hh], b_ref[hh])[0][HALO:, :])
        act_ref[...] = (c[0] * jax.nn.sigmoid(c[0]) * c[1]).astype(MXU)

    return pl.pallas_call(
        body, name=name, out_shape=jax.ShapeDtypeStruct((LP, F), MXU), grid=(F // FB, LP // tm),
        in_specs=[pl.BlockSpec((2, tm, FB), lambda j, i: (0, i, j)),
                  pl.BlockSpec((2, HALO, FB), lambda j, i: (0, jnp.maximum(i * nb - 1, 0), j)),
                  pl.BlockSpec((2, 3, FB), lambda j, i: (0, 0, j)),
                  pl.BlockSpec((2, 1, FB), lambda j, i: (0, 0, j))],
        out_specs=pl.BlockSpec((tm, FB), lambda j, i: (i, j)),
        compiler_params=_cp(("parallel", "parallel")))(a, a, cw, cb)


def conv_act_bwd(cfg, a, d_act, cw, cb, name):
    LP, F, FB, tm = cfg.LP, cfg.F, cfg.FB, cfg.conv_rows
    nb, last = tm // HALO, LP // tm - 1
    nrow = LP // HALO
    n = tm + 2 * HALO

    def body(a_ref, aprev_ref, anext_ref, d_ref, dnext_ref, w_ref, b_ref, da_ref, dcv_ref):
        i = pl.program_id(1)

        @pl.when(i == 0)
        def _():
            dcv_ref[...] = jnp.zeros_like(dcv_ref)

        c, exts = [], []
        for hh in range(2):
            prev = jnp.where(i > 0, aprev_ref[hh], 0.0)
            ext = jnp.concatenate([prev, a_ref[hh], anext_ref[hh]], axis=0)
            chh, s1, s2 = _conv3(ext, w_ref[hh], b_ref[hh])
            c.append(chh)
            exts.append((ext, s1, s2))
        dnext = jnp.where(i < last, dnext_ref[...], 0.0)
        dact = jnp.concatenate([jnp.zeros((HALO, FB), F32), d_ref[...], dnext], axis=0)
        sg = jax.nn.sigmoid(c[0])
        d_c = [dact * c[1] * (sg * (1.0 + c[0] * (1.0 - sg))), dact * (c[0] * sg)]
        for hh in range(2):
            dc, w = d_c[hh], w_ref[hh]
            da = w[2:3, :] * dc + w[1:2, :] * pltpu.roll(dc, n - 1, 0) + w[0:1, :] * pltpu.roll(dc, n - 2, 0)
            da_ref[hh] = da[HALO:HALO + tm, :].astype(MXU)
            dcb = dc[HALO:HALO + tm, :]
            ext, s1, s2 = exts[hh]
            dcv_ref[hh, 0:1, :] += jnp.sum(dcb * s2[HALO:HALO + tm, :], axis=0, keepdims=True)
            dcv_ref[hh, 1:2, :] += jnp.sum(dcb * s1[HALO:HALO + tm, :], axis=0, keepdims=True)
            dcv_ref[hh, 2:3, :] += jnp.sum(dcb * ext[HALO:HALO + tm, :], axis=0, keepdims=True)
            dcv_ref[hh, 3:4, :] += jnp.sum(dcb, axis=0, keepdims=True)

    return pl.pallas_call(
        body, name=name,
        out_shape=(jax.ShapeDtypeStruct((2, LP, F), MXU), jax.ShapeDtypeStruct((2, 8, F), F32)),
        grid=(F // FB, LP // tm),
        in_specs=[pl.BlockSpec((2, tm, FB), lambda j, i: (0, i, j)),
                  pl.BlockSpec((2, HALO, FB), lambda j, i: (0, jnp.maximum(i * nb - 1, 0), j)),
                  pl.BlockSpec((2, HALO, FB), lambda j, i: (0, jnp.minimum((i + 1) * nb, nrow - 1), j)),
                  pl.BlockSpec((tm, FB), lambda j, i: (i, j)),
                  pl.BlockSpec((HALO, FB), lambda j, i: (jnp.minimum((i + 1) * nb, nrow - 1), j)),
                  pl.BlockSpec((2, 3, FB), lambda j, i: (0, 0, j)),
                  pl.BlockSpec((2, 1, FB), lambda j, i: (0, 0, j))],
        out_specs=(pl.BlockSpec((2, tm, FB), lambda j, i: (0, i, j)),
                   pl.BlockSpec((2, 8, FB), lambda j, i: (0, 0, j))),
        compiler_params=_cp(("parallel", "arbitrary")))(a, a, a, d_act, d_act, cw, cb)


def fox_gate_fwd(cfg, f_logit, b_pad, name):
    LP, tb = cfg.LP, cfg.tq

    def body(f_ref, b_ref, c_ref, carry_ref):
        @pl.when(pl.program_id(0) == 0)
        def _():
            carry_ref[...] = jnp.zeros_like(carry_ref)

        xv = f_ref[...] + b_ref[...]
        lf = -_softplus(-xv)
        r = lax.broadcasted_iota(jnp.int32, (tb, tb), 0)
        s = lax.broadcasted_iota(jnp.int32, (tb, tb), 1)
        c = _split_dot_left((s <= r).astype(MXU), lf, 3) + carry_ref[0:1, :]
        c_ref[...] = c
        carry_ref[0:1, :] = c[tb - 1:tb, :]

    blk = pl.BlockSpec((tb, LANES), lambda i: (i, 0))
    return pl.pallas_call(
        body, name=name, out_shape=jax.ShapeDtypeStruct((LP, LANES), F32), grid=(LP // tb,),
        in_specs=[blk, pl.BlockSpec((1, LANES), lambda i: (0, 0))], out_specs=blk,
        scratch_shapes=[pltpu.VMEM((8, LANES), F32)], compiler_params=_cp(("arbitrary",)))(f_logit, b_pad)


def fox_gate_bwd(cfg, dc, f_logit, b_pad, name):
    LP, tb = cfg.LP, cfg.tq
    nblk = LP // tb

    def body(dc_ref, f_ref, b_ref, df_ref, db_ref, carry_ref):
        @pl.when(pl.program_id(0) == 0)
        def _():
            carry_ref[...] = jnp.zeros_like(carry_ref)
            db_ref[...] = jnp.zeros_like(db_ref)

        r = lax.broadcasted_iota(jnp.int32, (tb, tb), 0)
        s = lax.broadcasted_iota(jnp.int32, (tb, tb), 1)
        dlf = _split_dot_left((s >= r).astype(MXU), dc_ref[...], 3) + carry_ref[0:1, :]
        carry_ref[0:1, :] = dlf[0:1, :]
        df = dlf * jax.nn.sigmoid(-(f_ref[...] + b_ref[...]))
        df_ref[...] = df.astype(MXU)
        db_ref[...] += jnp.sum(df, axis=0, keepdims=True)

    blk = pl.BlockSpec((tb, LANES), lambda i: (nblk - 1 - i, 0))
    vec = pl.BlockSpec((1, LANES), lambda i: (0, 0))
    return pl.pallas_call(
        body, name=name,
        out_shape=(jax.ShapeDtypeStruct((LP, LANES), MXU), jax.ShapeDtypeStruct((1, LANES), F32)),
        grid=(nblk,), in_specs=[blk, blk, vec], out_specs=(blk, vec),
        scratch_shapes=[pltpu.VMEM((8, LANES), F32)], compiler_params=_cp(("arbitrary",)))(dc, f_logit, b_pad)


def _head_norm_fwd(o, g):
    return (o * lax.rsqrt(jnp.mean(o * o, axis=-1, keepdims=True) + EPS)) * g


def _head_norm_bwd(o, g, d_on):
    r = lax.rsqrt(jnp.mean(o * o, axis=-1, keepdims=True) + EPS)
    ohat = o * r
    gdy = d_on * g
    d_o = r * (gdy - ohat * jnp.mean(gdy * ohat, axis=-1, keepdims=True))
    return d_o, jnp.sum(d_on * ohat, axis=0, keepdims=True)


def _diag_masks(tq, strict):
    rows = lax.broadcasted_iota(jnp.int32, (tq, HD), 0)
    cols = lax.broadcasted_iota(jnp.int32, (tq, HD), 1)
    return [(cols + kk * HD < rows) if strict else (cols + kk * HD <= rows) for kk in range(tq // HD)]


def _walk_groups(i, R, group, carry, masks, descending):
    big = 2 * R
    if descending:
        carry = group(i * R, R, carry, masks)
        carry = lax.fori_loop(0, i % 2, lambda t, c: group((i - 1) * R, R, c, None), carry)
        return lax.fori_loop(0, i // 2, lambda t, c: group((i // 2 - 1 - t) * big, big, c, None), carry)
    carry = lax.fori_loop(0, i // 2, lambda t, c: group(t * big, big, c, None), carry)
    carry = lax.fori_loop(0, i % 2, lambda t, c: group((i // 2) * big, R, c, None), carry)
    return group(i * R, R, carry, masks)


def _tri(pred):
    a = lax.broadcasted_iota(jnp.int32, (HD, HD), 0)
    b = lax.broadcasted_iota(jnp.int32, (HD, HD), 1)
    return pred(a, b).astype(MXU)


def _attn_specs(cfg, group):
    base = 3 * cfg.NH * group
    return base, base + cfg.NH, base + 2 * cfg.NH


def sb_fwd(cfg, qkv, g3, name, xfer=None):
    LP, NH, tq = cfg.LP, cfg.NH, cfg.tq
    nq, R = LP // tq, tq // HD
    scale = HD ** -0.5
    cq, ck, cv = _attn_specs(cfg, 0)

    def body(q_ref, k_ref, v_ref, g_ref, opre_ref, on_ref, rtot_ref):
        i = pl.program_id(1)
        q = q_ref[...]
        masks = _diag_masks(tq, True)
        m_after = _tri(lambda a, b: a > b)

        def group(j0, n, carry, mk):
            acc, rc = carry
            masked = mk is not None
            order = range(n - 1, -1, -1)
            offs = [pl.multiple_of((j0 + kk) * HD, HD) for kk in range(n)]
            zs = [lax.dot_general(q, k_ref[pl.ds(offs[kk], HD), :], NT, preferred_element_type=F32) * scale
                  for kk in range(n)]
            ls, lks, tris = [None] * n, [None] * n, [None] * n
            for kk in order:
                sp = _softplus(zs[kk])
                lks[kk] = jnp.where(mk[kk], -sp, 0.0) if masked else -sp
                tris[kk] = _split_dot(lks[kk], m_after, LK_PIECES)
                ls[kk] = zs[kk] - sp
            for kk in order:
                a = jnp.exp(ls[kk] + (tris[kk] + rc))
                if masked:
                    a = jnp.where(mk[kk], a, 0.0)
                acc = acc + jnp.dot(a.astype(MXU), v_ref[pl.ds(offs[kk], HD), :], preferred_element_type=F32)
                rc = rc + jnp.sum(lks[kk], axis=1, keepdims=True)
            return acc, rc

        acc, rc = _walk_groups(i, R, group, (jnp.zeros((tq, HD), F32), jnp.zeros((tq, 1), F32)), masks, True)
        opre_ref[...] = acc
        on_ref[...] = _head_norm_fwd(acc, g_ref[0]).astype(MXU)
        rtot_ref[0] = rc

    return call_with_exchange(
        body, name, (NH, nq),
        [pl.BlockSpec((tq, HD), lambda h, i: (i, cq + h)),
         pl.BlockSpec((LP, HD), lambda h, i: (0, ck + h)),
         pl.BlockSpec((LP, HD), lambda h, i: (0, cv + h)),
         pl.BlockSpec((1, 1, HD), lambda h, i: (h, 0, 0))],
        (pl.BlockSpec((tq, HD), lambda h, i: (i, h)),
         pl.BlockSpec((tq, HD), lambda h, i: (i, h)),
         pl.BlockSpec((1, tq, 1), lambda h, i: (h, i, 0))),
        (jax.ShapeDtypeStruct((LP, cfg.WG), F32), jax.ShapeDtypeStruct((LP, cfg.WG), MXU),
         jax.ShapeDtypeStruct((NH, LP, 1), F32)),
        [], (qkv, qkv, qkv, g3), xfer)


def sb_bwd(cfg, qkv, g3, o_pre, d_on, rtot, name, xfer=None):
    LP, NH, tq = cfg.LP, cfg.NH, cfg.tq
    nq, R = LP // tq, tq // HD
    scale = HD ** -0.5
    cq, ck, cv = _attn_specs(cfg, 0)

    def body(q_ref, k_ref, v_ref, g_ref, o_ref, don_ref, rtot_ref, dq_ref, dk_ref, dv_ref, dg_ref,
             dk_acc, dv_acc):
        i = pl.program_id(1)

        @pl.when(i == 0)
        def _():
            dk_acc[...] = jnp.zeros_like(dk_acc)
            dv_acc[...] = jnp.zeros_like(dv_acc)
            dg_ref[...] = jnp.zeros_like(dg_ref)

        q = q_ref[...]
        d_o, dg = _head_norm_bwd(o_ref[...], g_ref[0], don_ref[...])
        dg_ref[0] += dg
        do_b = d_o.astype(MXU)
        rt = rtot_ref[0]
        masks = _diag_masks(tq, True)
        m_le = _tri(lambda a, b: a <= b)
        m_lt = _tri(lambda a, b: a < b)

        def group(j0, n, carry, mk):
            dq, lc, pc = carry
            masked = mk is not None
            offs = [pl.multiple_of((j0 + kk) * HD, HD) for kk in range(n)]
            zs = [lax.dot_general(q, k_ref[pl.ds(offs[kk], HD), :], NT, preferred_element_type=F32) * scale
                  for kk in range(n)]
            das = [lax.dot_general(do_b, v_ref[pl.ds(offs[kk], HD), :], NT, preferred_element_type=F32)
                   for kk in range(n)]
            ls, lks, tri1 = [], [], []
            for kk in range(n):
                sp = _softplus(zs[kk])
                lk = jnp.where(mk[kk], -sp, 0.0) if masked else -sp
                tri1.append(_split_dot(lk, m_le, LK_PIECES))
                lks.append(lk)
                ls.append(zs[kk] - sp)
            ggs, a_bs, tri2 = [], [], []
            for kk in range(n):
                a = jnp.exp(ls[kk] + ((rt - lc) - tri1[kk]))
                if masked:
                    a = jnp.where(mk[kk], a, 0.0)
                gg = a * das[kk]
                tri2.append(_split_dot(gg, m_lt, 2))
                ggs.append(gg)
                a_bs.append(a.astype(MXU))
                lc = lc + jnp.sum(lks[kk], axis=1, keepdims=True)
            for kk in range(n):
                sig = jnp.exp(ls[kk])
                dz = (ggs[kk] * (1.0 - sig) - sig * (pc + tri2[kk])) * scale
                if masked:
                    dz = jnp.where(mk[kk], dz, 0.0)
                dz_b = dz.astype(MXU)
                dq = dq + jnp.dot(dz_b, k_ref[pl.ds(offs[kk], HD), :], preferred_element_type=F32)
                dk_acc[pl.ds(offs[kk], HD), :] += lax.dot_general(dz_b, q, TN, preferred_element_type=F32)
                dv_acc[pl.ds(offs[kk], HD), :] += lax.dot_general(a_bs[kk], do_b, TN, preferred_element_type=F32)
                pc = pc + jnp.sum(ggs[kk], axis=1, keepdims=True)
            return dq, lc, pc

        zc = jnp.zeros((tq, 1), F32)
        dq, _, _ = _walk_groups(i, R, group, (jnp.zeros((tq, HD), F32), zc, zc), masks, False)
        dq_ref[...] = dq.astype(MXU)

        @pl.when(i == nq - 1)
        def _():
            dk_ref[...] = dk_acc[...].astype(MXU)
            dv_ref[...] = dv_acc[...].astype(MXU)

    blk = pl.BlockSpec((tq, HD), lambda h, i: (i, h))
    full = pl.BlockSpec((LP, HD), lambda h, i: (0, h))
    gspec = pl.BlockSpec((1, 1, HD), lambda h, i: (h, 0, 0))
    return call_with_exchange(
        body, name, (NH, nq),
        [pl.BlockSpec((tq, HD), lambda h, i: (i, cq + h)),
         pl.BlockSpec((LP, HD), lambda h, i: (0, ck + h)),
         pl.BlockSpec((LP, HD), lambda h, i: (0, cv + h)),
         gspec, blk, blk, pl.BlockSpec((1, tq, 1), lambda h, i: (h, i, 0))],
        (blk, full, full, gspec),
        (jax.ShapeDtypeStruct((LP, cfg.WG), MXU),) * 3 + (jax.ShapeDtypeStruct((NH, 1, HD), F32),),
        [pltpu.VMEM((LP, HD), F32), pltpu.VMEM((LP, HD), F32)],
        (qkv, qkv, qkv, g3, o_pre, d_on, rtot), xfer)


def fox_fwd(cfg, qkv, g3, c_col, c_row, name, xfer=None):
    LP, NH, tq = cfg.LP, cfg.NH, cfg.tq
    nq, R = LP // tq, tq // HD
    scale = HD ** -0.5
    cq, ck, cv = _attn_specs(cfg, 1)

    def body(q_ref, k_ref, v_ref, g_ref, ccol_ref, crow_ref, opre_ref, on_ref, lse_ref):
        i = pl.program_id(1)
        q = q_ref[...]
        cq_b = jnp.broadcast_to(ccol_ref[0], (tq, HD))
        masks = _diag_masks(tq, False)

        def group(j0, n, carry, mk):
            acc, m, l = carry
            ss, offs = [], []
            for kk in range(n):
                j = j0 + kk
                off = pl.multiple_of(j * HD, HD)
                s = (lax.dot_general(q, k_ref[pl.ds(off, HD), :], NT, preferred_element_type=F32) * scale
                     + (cq_b - crow_ref[0, pl.ds(j, 1), :]))
                ss.append(s if mk is None else jnp.where(mk[kk], s, NEG))
                offs.append(off)
            mx = jnp.max(ss[0], axis=1, keepdims=True)
            for s in ss[1:]:
                mx = jnp.maximum(mx, jnp.max(s, axis=1, keepdims=True))
            m_new = jnp.maximum(m, mx)
            alpha = jnp.exp(m - m_new)
            acc, l = alpha * acc, alpha * l
            m_b = jnp.broadcast_to(m_new, (tq, HD))
            for s, off in zip(ss, offs):
                p = jnp.exp(s - m_b)
                l = l + jnp.sum(p, axis=1, keepdims=True)
                acc = acc + _split_dot(p, v_ref[pl.ds(off, HD), :], 2)
            return acc, m_new, l

        carry = (jnp.zeros((tq, HD), F32), jnp.full((tq, 1), NEG, F32), jnp.zeros((tq, 1), F32))
        acc, m, l = _walk_groups(i, R, group, carry, masks, False)
        o = acc / l
        opre_ref[...] = o
        on_ref[...] = _head_norm_fwd(o, g_ref[0]).astype(MXU)
        lse_ref[0] = m + jnp.log(l)

    return call_with_exchange(
        body, name, (NH, nq),
        [pl.BlockSpec((tq, HD), lambda h, i: (i, cq + h)),
         pl.BlockSpec((LP, HD), lambda h, i: (0, ck + h)),
         pl.BlockSpec((LP, HD), lambda h, i: (0, cv + h)),
         pl.BlockSpec((1, 1, HD), lambda h, i: (h, 0, 0)),
         pl.BlockSpec((1, tq, 1), lambda h, i: (h, i, 0)),
         pl.BlockSpec((1, LP // HD, HD), lambda h, i: (h, 0, 0))],
        (pl.BlockSpec((tq, HD), lambda h, i: (i, h)),
         pl.BlockSpec((tq, HD), lambda h, i: (i, h)),
         pl.BlockSpec((1, tq, 1), lambda h, i: (h, i, 0))),
        (jax.ShapeDtypeStruct((LP, cfg.WG), F32), jax.ShapeDtypeStruct((LP, cfg.WG), MXU),
         jax.ShapeDtypeStruct((NH, LP, 1), F32)),
        [], (qkv, qkv, qkv, g3, c_col, c_row), xfer)


def fox_bwd(cfg, qkv, g3, c_col, c_row, o_pre, d_on, lse, name, xfer=None):
    LP, NH, tq = cfg.LP, cfg.NH, cfg.tq
    nq, R = LP // tq, tq // HD
    scale = HD ** -0.5
    cq, ck, cv = _attn_specs(cfg, 1)

    def body(q_ref, k_ref, v_ref, g_ref, ccol_ref, crow_ref, o_ref, don_ref, lse_ref,
             dq_ref, dk_ref, dv_ref, dg_ref, dc_ref, dk_acc, dv_acc):
        i = pl.program_id(1)

        @pl.when(i == 0)
        def _():
            dk_acc[...] = jnp.zeros_like(dk_acc)
            dv_acc[...] = jnp.zeros_like(dv_acc)
            dg_ref[...] = jnp.zeros_like(dg_ref)
            dc_ref[...] = jnp.zeros_like(dc_ref)

        q = q_ref[...]
        ov = o_ref[...]
        d_o, dg = _head_norm_bwd(ov, g_ref[0], don_ref[...])
        dg_ref[0] += dg
        do_b = d_o.astype(MXU)
        delta = jnp.sum(do_b.astype(F32) * ov, axis=1, keepdims=True)
        cqv, lsev = ccol_ref[0], lse_ref[0]
        cl = cqv - lsev
        masks = _diag_masks(tq, False)

        def group(j0, n, dq, mk):
            offs = [pl.multiple_of((j0 + kk) * HD, HD) for kk in range(n)]
            ss = [lax.dot_general(q, k_ref[pl.ds(offs[kk], HD), :], NT, preferred_element_type=F32) * scale
                  + (cl - crow_ref[0, pl.ds(j0 + kk, 1), :]) for kk in range(n)]
            dps = [lax.dot_general(do_b, v_ref[pl.ds(offs[kk], HD), :], NT, preferred_element_type=F32)
                   for kk in range(n)]
            for kk in range(n):
                p = jnp.exp(ss[kk])
                if mk is not None:
                    p = jnp.where(mk[kk], p, 0.0)
                ds = p * (dps[kk] - delta)
                dc_ref[0, pl.ds(j0 + kk, 1), :] -= jnp.sum(ds, axis=0, keepdims=True)
                ds_b = (ds * scale).astype(MXU)
                dk_acc[pl.ds(offs[kk], HD), :] += lax.dot_general(ds_b, q, TN, preferred_element_type=F32)
                dv_acc[pl.ds(offs[kk], HD), :] += lax.dot_general(p.astype(MXU), do_b, TN,
                                                                  preferred_element_type=F32)
                dq = dq + jnp.dot(ds_b, k_ref[pl.ds(offs[kk], HD), :], preferred_element_type=F32)
            return dq

        dq = _walk_groups(i, R, group, jnp.zeros((tq, HD), F32), masks, False)
        dq_ref[...] = dq.astype(MXU)

        @pl.when(i == nq - 1)
        def _():
            dk_ref[...] = dk_acc[...].astype(MXU)
            dv_ref[...] = dv_acc[...].astype(MXU)

    blk = pl.BlockSpec((tq, HD), lambda h, i: (i, h))
    full = pl.BlockSpec((LP, HD), lambda h, i: (0, h))
    gspec = pl.BlockSpec((1, 1, HD), lambda h, i: (h, 0, 0))
    col = pl.BlockSpec((1, tq, 1), lambda h, i: (h, i, 0))
    rowv = pl.BlockSpec((1, LP // HD, HD), lambda h, i: (h, 0, 0))
    return call_with_exchange(
        body, name, (NH, nq),
        [pl.BlockSpec((tq, HD), lambda h, i: (i, cq + h)),
         pl.BlockSpec((LP, HD), lambda h, i: (0, ck + h)),
         pl.BlockSpec((LP, HD), lambda h, i: (0, cv + h)),
         gspec, col, rowv, blk, blk, col],
        (blk, full, full, gspec, rowv),
        (jax.ShapeDtypeStruct((LP, cfg.WG), MXU),) * 3
        + (jax.ShapeDtypeStruct((NH, 1, HD), F32), jax.ShapeDtypeStruct((NH, LP // HD, HD), F32)),
        [pltpu.VMEM((LP, HD), F32), pltpu.VMEM((LP, HD), F32)],
        (qkv, qkv, qkv, g3, c_col, c_row, o_pre, d_on, lse), xfer)


def _row_tile(rows, cols, n_arrays):
    budget = 24 * 1024 * 1024 // (2 * n_arrays * cols * 4)
    cap = max(8, min(rows, budget // 8 * 8))
    div = _tile(rows, cap, 8)
    return div if div <= cap and div * 4 >= cap else cap


def sum_slots(recvs, name, swap=None):
    S, rows, cols = recvs[0].shape
    nl = len(recvs)
    tr = _tile(rows, _row_tile(rows, cols, nl * S + 1), 8)
    nb = rows // tr

    def body(*refs):
        o_ref = refs[nl + (swap is not None)]
        layer, blk = pl.program_id(0), pl.program_id(1)
        if swap is not None:
            x, y, c = _coords()
            cp = pltpu.make_async_remote_copy(src_ref=refs[nl], dst_ref=refs[nl + 2], send_sem=refs[nl + 3],
                                              recv_sem=refs[nl + 4], device_id=(x, y, 1 - c), device_id_type=MESH)

            @pl.when((layer == 0) & (blk == 0))
            def _():
                cp.start()

        for ll in range(nl):
            @pl.when(layer == ll)
            def _(r_ref=refs[ll]):
                acc = r_ref[0].astype(F32)
                for s in range(1, S):
                    acc = acc + r_ref[s].astype(F32)
                o_ref[...] = acc

        if swap is not None:
            @pl.when((layer == nl - 1) & (blk == nb - 1))
            def _():
                cp.wait()

    def spec(ll):
        return pl.BlockSpec((S, tr, cols), lambda l, i: (0, jnp.where(l == ll, i, jnp.where(l < ll, 0, nb - 1)), 0))

    any_spec = pl.BlockSpec(memory_space=pl.ANY)
    out = jax.ShapeDtypeStruct((nl * rows, cols), F32)
    out_spec = pl.BlockSpec((tr, cols), lambda l, i: (l * nb + i, 0))
    extra = swap is not None
    return pl.pallas_call(
        body, name=name, grid=(nl, nb),
        out_shape=(out, jax.ShapeDtypeStruct(swap.shape, swap.dtype)) if extra else out,
        in_specs=[spec(ll) for ll in range(nl)] + ([any_spec] if extra else []),
        out_specs=(out_spec, any_spec) if extra else out_spec,
        scratch_shapes=[pltpu.SemaphoreType.DMA, pltpu.SemaphoreType.DMA] if extra else [],
        compiler_params=_cp(("arbitrary", "arbitrary")))(*recvs, *([swap] if extra else []))


def adamw(w, m, v, g_parts, name):
    rows, cols = w.shape
    npart = len(g_parts)
    tr = _row_tile(rows, cols, 7 + npart)
    c1 = 1.0 - ADAM_B1 ** ADAM_STEP
    c2 = 1.0 - ADAM_B2 ** ADAM_STEP

    def body(*refs):
        w_ref, m_ref, v_ref = refs[:3]
        g_refs = refs[3:3 + npart]
        g_out, d_out, m_out, v_out = refs[3 + npart:]
        g = g_refs[0][...]
        for r in g_refs[1:]:
            g = g + r[...]
        g_out[...] = g
        mn = ADAM_B1 * m_ref[...] + (1.0 - ADAM_B1) * g
        vn = ADAM_B2 * v_ref[...] + (1.0 - ADAM_B2) * (g * g)
        m_out[...] = mn
        v_out[...] = vn
        d_out[...] = -ADAM_LR * ((mn / c1) / (jnp.sqrt(vn / c2) + ADAM_EPS) + ADAM_WD * w_ref[...])

    blk = pl.BlockSpec((tr, cols), lambda i: (i, 0))
    return pl.pallas_call(
        body, name=name, out_shape=(jax.ShapeDtypeStruct((rows, cols), F32),) * 4,
        grid=(pl.cdiv(rows, tr),), in_specs=[blk] * (3 + npart), out_specs=(blk,) * 4,
        compiler_params=_cp(("parallel",)))(w, m, v, *g_parts)


def _coords():
    return lax.axis_index("x"), lax.axis_index("y"), lax.axis_index("c")


def chip_exchange(arrs, scatter, name):
    n = len(arrs)

    def body(*refs):
        ins, outs, sems = refs[:n], refs[n:2 * n], refs[2 * n:]
        _xfer_start(ins, outs, sems, scatter)
        _xfer_forward(ins, outs, sems, scatter)
        _xfer_finish(ins, outs, sems, scatter)

    out_shape, scratch = _xfer_shapes(arrs, scatter)
    any_spec = pl.BlockSpec(memory_space=pl.ANY)
    return pl.pallas_call(body, name=name, out_shape=out_shape, in_specs=[any_spec] * n, out_specs=(any_spec,) * n,
                          scratch_shapes=scratch)(*arrs)


def _xfer_shapes(arrs, scatter):
    n = len(arrs)
    shapes = [a.shape[1:] if scatter else a.shape for a in arrs]
    out_shape = tuple(jax.ShapeDtypeStruct((NCHIP,) + tuple(s), a.dtype) for s, a in zip(shapes, arrs))
    scratch = [pltpu.SemaphoreType.DMA((n, 6)), pltpu.SemaphoreType.DMA((n, 6)), pltpu.SemaphoreType.DMA((n,))]
    return out_shape, scratch


def _xfer_copies(ins, outs, sems, scatter):
    send_sems, recv_sems, local_sems = sems
    n = len(ins)
    x, y, c = _coords()
    me = 2 * x + y
    peers = [(1 - x, y), (x, 1 - y), (1 - x, 1 - y)]

    def rdma(src, dst, a, j, dev):
        return pltpu.make_async_remote_copy(src_ref=src, dst_ref=dst, send_sem=send_sems.at[a, j],
                                            recv_sem=recv_sems.at[a, j], device_id=dev, device_id_type=MESH)

    def half(ref, hc):
        hr = ref.shape[0] // 2
        return ref.at[pl.ds(pl.multiple_of(hc * hr, 8), hr)]

    local, sends, recvs, fwds, fwd_recvs = [], [], [], [], []
    for a in range(n):
        local.append(pltpu.make_async_copy(ins[a].at[me] if scatter else ins[a], outs[a].at[me], local_sems.at[a]))
        for k, (px, py) in enumerate(peers):
            there = 2 * px + py
            if scatter:
                sends.append(rdma(ins[a].at[there], outs[a].at[me], a, k, (px, py, c)))
                recvs.append(rdma(ins[a].at[me], outs[a].at[there], a, k, (px, py, c)))
            else:
                mine, landed = half(outs[a].at[me], c), half(outs[a].at[there], c)
                sends.append(rdma(half(ins[a], c), mine, a, k, (px, py, c)))
                recvs.append(rdma(half(ins[a], c), landed, a, k, (px, py, c)))
                fwds.append(rdma(landed, landed, a, 3 + k, (x, y, 1 - c)))
                other = half(outs[a].at[there], 1 - c)
                fwd_recvs.append(rdma(other, other, a, 3 + k, (x, y, 1 - c)))
    return local, sends, recvs, fwds, fwd_recvs


def _xfer_start(ins, outs, sems, scatter):
    local, sends, _, _, _ = _xfer_copies(ins, outs, sems, scatter)
    for cp in local + sends:
        cp.start()


def _xfer_forward(ins, outs, sems, scatter):
    _, _, recvs, fwds, _ = _xfer_copies(ins, outs, sems, scatter)
    for r, f in zip(recvs, fwds):
        r.wait_recv()
        f.start()


def _xfer_finish(ins, outs, sems, scatter):
    local, sends, recvs, fwds, fwd_recvs = _xfer_copies(ins, outs, sems, scatter)
    for cp in (recvs if scatter else fwd_recvs):
        cp.wait_recv()
    for cp in sends + fwds:
        cp.wait_send()
    for cp in local:
        cp.wait()


def call_with_exchange(core, name, grid, in_specs, out_specs, out_shape, scratch, args, xfer):
    n_in, n_out, n_scr = len(in_specs), len(out_specs), len(scratch)
    if xfer is None:
        res = pl.pallas_call(core, name=name, out_shape=out_shape, grid=grid, in_specs=in_specs,
                             out_specs=out_specs, scratch_shapes=scratch,
                             compiler_params=_cp(("arbitrary",) * len(grid)))(*args)
        return res, ()
    arrs, scatter = xfer
    nx = len(arrs)
    x_shape, x_scratch = _xfer_shapes(arrs, scatter)

    def body(*refs):
        a, xi = refs[:n_in], refs[n_in:n_in + nx]
        o, xo = refs[n_in + nx:n_in + nx + n_out], refs[n_in + nx + n_out:n_in + 2 * nx + n_out]
        rest = refs[n_in + 2 * nx + n_out:]
        scr, sems = rest[:n_scr], rest[n_scr:]
        first, late, last = None, None, None
        for d, g in enumerate(grid):
            pid = pl.program_id(d)
            f, m, l = pid == 0, pid == ((3 * g) // 4 if d == 0 else 0), pid == g - 1
            first = f if first is None else first & f
            late = m if late is None else late & m
            last = l if last is None else last & l

        @pl.when(first)
        def _():
            _xfer_start(xi, xo, sems, scatter)

        core(*a, *o, *scr)

        if not scatter:
            @pl.when(late)
            def _():
                _xfer_forward(xi, xo, sems, scatter)

        @pl.when(last)
        def _():
            _xfer_finish(xi, xo, sems, scatter)

    any_spec = pl.BlockSpec(memory_space=pl.ANY)
    res = pl.pallas_call(
        body, name=name, out_shape=tuple(out_shape) + tuple(x_shape), grid=grid,
        in_specs=list(in_specs) + [any_spec] * nx, out_specs=tuple(out_specs) + (any_spec,) * nx,
        scratch_shapes=list(scratch) + x_scratch,
        compiler_params=_cp(("arbitrary",) * len(grid)))(*args, *arrs)
    return res[:n_out], res[n_out:]


def sibling_exchange(arrs, name):
    n = len(arrs)

    def body(*refs):
        ins, outs = refs[:n], refs[n:2 * n]
        send_sems, recv_sems = refs[2 * n:]
        x, y, c = _coords()
        cps = [pltpu.make_async_remote_copy(src_ref=ins[a], dst_ref=outs[a], send_sem=send_sems.at[a],
                                            recv_sem=recv_sems.at[a], device_id=(x, y, 1 - c),
                                            device_id_type=MESH) for a in range(n)]
        for cp in cps:
            cp.start()
        for cp in cps:
            cp.wait()

    any_spec = pl.BlockSpec(memory_space=pl.ANY)
    return pl.pallas_call(
        body, name=name, out_shape=tuple(jax.ShapeDtypeStruct(a.shape, a.dtype) for a in arrs),
        in_specs=[any_spec] * n, out_specs=(any_spec,) * n,
        scratch_shapes=[pltpu.SemaphoreType.DMA((n,)), pltpu.SemaphoreType.DMA((n,))],
    )(*arrs)


def all_reduce_small(pack, name):
    R = pack.shape[0]

    def body(p_ref, o_ref, slots, send_sems, recv_sems):
        x, y, c = _coords()
        me = 4 * x + 2 * y + c
        slots[me] = p_ref[...]
        cps = []
        for k in range(1, 8):
            kx, ky, kc = (k >> 2) & 1, (k >> 1) & 1, k & 1
            peer = ((1 - x) if kx else x, (1 - y) if ky else y, (1 - c) if kc else c)
            cp = pltpu.make_async_remote_copy(src_ref=p_ref, dst_ref=slots.at[me], send_sem=send_sems.at[k],
                                              recv_sem=recv_sems.at[k], device_id=peer, device_id_type=MESH)
            cp.start()
            cps.append((cp, peer))
        for k, (cp, peer) in enumerate(cps, start=1):
            frm = 4 * peer[0] + 2 * peer[1] + peer[2]
            pltpu.make_async_remote_copy(src_ref=p_ref, dst_ref=slots.at[frm], send_sem=send_sems.at[k],
                                         recv_sem=recv_sems.at[k], device_id=peer, device_id_type=MESH).wait_recv()
        for cp, _ in cps:
            cp.wait_send()
        acc = slots[0]
        for s in range(1, 8):
            acc = acc + slots[s]
        o_ref[...] = acc

    vm = pl.BlockSpec(memory_space=pltpu.VMEM)
    return pl.pallas_call(
        body, name=name, out_shape=jax.ShapeDtypeStruct((R, LANES), F32), in_specs=[vm], out_specs=vm,
        scratch_shapes=[pltpu.VMEM((8, R, LANES), F32), pltpu.SemaphoreType.DMA((8,)),
                        pltpu.SemaphoreType.DMA((8,))],
        compiler_params=pltpu.CompilerParams(vmem_limit_bytes=VMEM_LIMIT),
    )(pack)


def _gate_up(w):
    r, c = w.shape
    return w.reshape(r, 2, c // 2).transpose(1, 0, 2)


def _gate_up_inv(w):
    return w.transpose(1, 0, 2).reshape(w.shape[1], -1)


def _pack(arrs):
    parts = []
    for a in arrs:
        f = a.reshape(-1).astype(F32)
        parts.append(jnp.pad(f, (0, -f.shape[0] % 1024)).reshape(-1, LANES))
    return jnp.concatenate(parts, axis=0)


def _unpack(p, shapes):
    out, r = [], 0
    for s in shapes:
        n = math.prod(s)
        nr = (n + 1023) // 1024 * 8
        out.append(p[r:r + nr].reshape(-1)[:n].reshape(s))
        r += nr
    return out


def _vec(g):
    return g.reshape(1, -1)


class _LocalWeights:
    def __init__(self, cfg, wf):
        self.cfg, self.wf, self.grads = cfg, wf, {}

    def w_in(self, l):
        return self.wf['w_in'][l]

    def fwd_exchanges(self, l):
        return None, None

    def rest(self, l, got_sb, got_fox):
        w_up = self.wf['w_up'][l]
        return dict(w_out=self.wf['w_out'][l], w_down=self.wf['w_down'][l],
                    w_up=w_up.reshape(w_up.shape[0], NCHIP, -1).transpose(1, 0, 2))

    def bwd_exchanges(self, l, g):
        return None, None

    def in_dx_exchange(self, l, g):
        return None

    def bwd_done(self, l, g, got_sb, got_fox, got_in):
        self.grads[l] = g


def _w_in_parts(cfg, w_in):
    w_f = jnp.pad(w_in[:, cfg.NQKV:], ((0, 0), (0, LANES - cfg.NH)))
    return dict(w_qkv=w_in[:, :cfg.NQKV], w_f=w_f, w_in_ext=jnp.concatenate([w_in[:, :cfg.NQKV], w_f], axis=1))


def _layer_fwd(cfg, l, h, u, wl, io):
    tag = f"l{l}"
    qkv = matmul(u, wl['w_qkv'], 'nn', MXU, f"{tag}_qkv", tn_cap=1024)
    f_logit = matmul(u, wl['w_f'], 'nn', F32, f"{tag}_fproj")
    cpre = fox_gate_fwd(cfg, f_logit, wl['b_pad'], f"{tag}_gate_fwd")
    c_heads = cpre[:, :cfg.NH].T
    c_col = c_heads[:, :, None]
    c_row = c_heads.reshape(cfg.NH, cfg.LP // HD, HD)
    x_sb, x_fox = io.fwd_exchanges(l)
    (o_sb, on_sb, rtot), got_sb = sb_fwd(cfg, qkv, wl['g_sb'], f"{tag}_sb_fwd", x_sb)
    (o_fx, on_fx, lse), got_fox = fox_fwd(cfg, qkv, wl['g_fox'], c_col, c_row, f"{tag}_fox_fwd", x_fox)
    wl.update(io.rest(l, got_sb, got_fox))
    mixin = jnp.concatenate([on_sb, on_fx], axis=1)
    mix = matmul(mixin, wl['w_out'], 'nn', F32, f"{tag}_out")
    h1, u2 = resid_norm(cfg, mix, h, wl['g_mix_post'], wl['g_ffn_pre'], f"{tag}_mixres")
    a = matmul(u2, wl['w_up'], 'nn', F32, f"{tag}_up", tn_cap=1408, out_split=2)
    act = conv_act_fwd(cfg, a, wl['conv_w'], wl['conv_b'], f"{tag}_conv_fwd")
    ff = matmul(act, wl['w_down'], 'nn', F32, f"{tag}_down", tm_cap=704)
    h2, u_next = resid_norm(cfg, ff, h1, wl['g_ffn_post'], wl['g_next'], f"{tag}_ffnres")
    saved = dict(h=h, u=u, qkv=qkv, f_logit=f_logit, c_col=c_col, c_row=c_row, o_sb=o_sb, o_fx=o_fx, rtot=rtot,
                 lse=lse, mixin=mixin, mix=mix, h1=h1, u2=u2, a=a, act=act, ff=ff)
    return h2, u_next, saved


def _layer_bwd(cfg, l, dh2, wl, sv, io):
    tag = f"l{l}"
    g = {}
    d_ff, g['g_ffn_post'] = norm_bwd(cfg, sv['ff'], wl['g_ffn_post'], dh2, None, MXU, f"{tag}_ffnpost_bwd")
    d_act = matmul(d_ff, wl['w_down'], 'nt', F32, f"{tag}_down_dx")
    g['w_down'] = matmul(sv['act'], d_ff, 'tn', MXU, f"{tag}_down_dw", tm_cap=704, tk_cap=4224)
    d_a, d_conv = conv_act_bwd(cfg, sv['a'], d_act, wl['conv_w'], wl['conv_b'], f"{tag}_conv_bwd")
    g['conv'] = d_conv
    g['w_up'] = matmul(sv['u2'], d_a, 'tn', MXU, f"{tag}_up_dw", tm_cap=512, tn_cap=1408, tk_cap=4224,
                       out_split=NCHIP)
    du2 = matmul(d_a, wl['w_up'], 'nt', F32, f"{tag}_up_dx", tm_cap=704, tk_cap=5632)
    dh1, g['g_ffn_pre'] = norm_bwd(cfg, sv['h1'], wl['g_ffn_pre'], du2, dh2, F32, f"{tag}_ffnpre_bwd")
    d_mix, g['g_mix_post'] = norm_bwd(cfg, sv['mix'], wl['g_mix_post'], dh1, None, MXU, f"{tag}_mixpost_bwd")
    d_mixin = matmul(d_mix, wl['w_out'], 'nt', F32, f"{tag}_out_dx")
    g['w_out'] = matmul(sv['mixin'], d_mix, 'tn', MXU, f"{tag}_out_dw", tm_cap=1024, tk_cap=4224)
    WG = cfg.WG
    x_sb, x_fox = io.bwd_exchanges(l, g)
    (dq_s, dk_s, dv_s, g['g_sb']), got_sb = sb_bwd(cfg, sv['qkv'], wl['g_sb'], sv['o_sb'], d_mixin[:, :WG],
                                                   sv['rtot'], f"{tag}_sb_bwd", x_sb)
    (dq_f, dk_f, dv_f, g['g_fox'], dc_row), got_fox = fox_bwd(cfg, sv['qkv'], wl['g_fox'], sv['c_col'], sv['c_row'],
                                                              sv['o_fx'], d_mixin[:, WG:], sv['lse'],
                                                              f"{tag}_fox_bwd", x_fox)
    dc = jnp.pad(dc_row.reshape(cfg.NH, cfg.LP).T, ((0, 0), (0, LANES - cfg.NH)))
    d_f, g['b_f'] = fox_gate_bwd(cfg, dc, sv['f_logit'], wl['b_pad'], f"{tag}_gate_bwd")
    d_proj = jnp.concatenate([dq_s, dk_s, dv_s, dq_f, dk_f, dv_f, d_f], axis=1)
    g['w_in_ext'] = matmul(sv['u'], d_proj, 'tn', MXU, f"{tag}_in_dw", tm_cap=1024, tn_cap=896, tk_cap=4224)
    x_in = io.in_dx_exchange(l, g)
    du = matmul(d_proj, wl['w_in_ext'], 'nt', F32, f"{tag}_in_dx", tm_cap=704, xfer=x_in)
    du, got_in = du if x_in is not None else (du, None)
    dh, g['g_mix_pre'] = norm_bwd(cfg, sv['h'], wl['g_mix_pre'], du, dh1, F32, f"{tag}_mixpre_bwd")
    io.bwd_done(l, g, got_sb, got_fox, got_in)
    return dh


def _local_step(cfg, h0, target_p, wf, io=None):
    io = _LocalWeights(cfg, wf) if io is None else io
    layers = []
    for l in range(cfg.DEPTH):
        layers.append(dict(
            b_pad=jnp.pad(wf['b_f'][l], (0, LANES - cfg.NH)).reshape(1, LANES),
            g_sb=wf['g_sb'][l][:, None, :], g_fox=wf['g_fox'][l][:, None, :],
            conv_w=_gate_up(wf['conv_w'][l]), conv_b=wf['conv_b'][l].reshape(2, 1, cfg.F),
            g_mix_pre=_vec(wf['g_mix_pre'][l]), g_mix_post=_vec(wf['g_mix_post'][l]),
            g_ffn_pre=_vec(wf['g_ffn_pre'][l]), g_ffn_post=_vec(wf['g_ffn_post'][l]),
            g_next=_vec(wf['g_mix_pre'][(l + 1) % cfg.DEPTH])))
    h = h0
    u = pre_norm(cfg, h0, layers[0]['g_mix_pre'], "l0_prenorm")
    saved = []
    for l in range(cfg.DEPTH):
        layers[l].update(_w_in_parts(cfg, io.w_in(l)))
        h, u, sv = _layer_fwd(cfg, l, h, u, layers[l], io)
        saved.append(sv)
    dh, loss_blk = loss_head(cfg, h, target_p, "loss_head")
    for l in reversed(range(cfg.DEPTH)):
        dh = _layer_bwd(cfg, l, dh, layers[l], saved[l], io)
    return loss_blk, dh, [io.grads[l] for l in range(cfg.DEPTH)]


def _cols(g):
    return g.transpose(1, 0, 2).reshape(g.shape[1], -1)


def _rows(g):
    return g.reshape(-1, g.shape[2])


def _send_cols(g):
    r, c = g.shape
    return g.reshape(r, NCHIP, c // NCHIP).transpose(1, 0, 2).astype(MXU)


def _send_rows(g):
    r, c = g.shape
    return g.reshape(NCHIP, r // NCHIP, c).astype(MXU)


class _StreamedWeights:
    def __init__(self, cfg, w):
        self.cfg = cfg
        self.shard = {n: [w[n][l].astype(MXU) for l in range(cfg.DEPTH)] for n in BIG}
        self.grads, self.recv = {}, {n: [None] * cfg.DEPTH for n in BIG}
        n_cw = cfg.DEPTH * 3
        conv_w = jnp.pad(w['conv_w'].reshape(n_cw, -1), ((0, -n_cw % 16), (0, 0)))
        got = chip_exchange([self.shard['w_in'][0], w['meta'], conv_w], False, "gather_first")
        self.win = {0: _cols(got[0])}
        self.meta = _cols(got[1])
        self.conv_w = _cols(got[2])[:n_cw].reshape(cfg.DEPTH, 3, cfg.F2)
        self.pending = None

    def w_in(self, l):
        return self.win[l]

    def fwd_exchanges(self, l):
        s = self.shard
        fox = [s['w_down'][l]] + ([s['w_in'][l + 1]] if l + 1 < self.cfg.DEPTH else [])
        return ([s['w_out'][l], s['w_up'][l]], False), (fox, False)

    def rest(self, l, got_sb, got_fox):
        if l + 1 < self.cfg.DEPTH:
            self.win[l + 1] = _cols(got_fox[1])
        return dict(w_out=_rows(got_sb[0]), w_up=got_sb[1], w_down=_rows(got_fox[0]))

    def bwd_exchanges(self, l, g):
        sb = [g['w_up']] + ([] if self.pending is None else [self.pending])
        self.pending = None
        return (sb, True), ([_send_rows(g['w_down']), _send_rows(g['w_out'])], True)

    def in_dx_exchange(self, l, g):
        send = _send_cols(g['w_in_ext'][:, :self.cfg.N_IN])
        if l == 0:
            return [send], True
        self.pending = send
        return None

    def bwd_done(self, l, g, got_sb, got_fox, got_in):
        self.grads[l] = g
        self.recv['w_up'][l] = got_sb[0]
        if len(got_sb) > 1:
            self.recv['w_in'][l + 1] = got_sb[1]
        self.recv['w_down'][l], self.recv['w_out'][l] = got_fox
        if got_in is not None:
            self.recv['w_in'][l] = got_in[0]


def _step(cfg, x, w, target, m, v):
    xi, yi, ci = _coords()
    chip = 2 * xi + yi
    D, LP, L, NM = cfg.D, cfg.LP, cfg.L, cfg.NMETA
    DEP = cfg.DEPTH
    io = _StreamedWeights(cfg, w)
    wf = {n: w[n] for n in SMALL}
    meta_full, wf['conv_w'] = io.meta, io.conv_w

    zpad = jnp.zeros((LP - L, D), F32)
    h0 = jnp.concatenate([meta_full, x[0], zpad], axis=0)
    target_p = jnp.concatenate([jnp.zeros((NM, D), F32), target[0], zpad], axis=0)
    loss_blk, dh0, grads = _local_step(cfg, h0, target_p, wf, io)

    def stack(key, shape):
        return jnp.stack([grads[l][key].reshape(shape) for l in range(DEP)])

    conv_g = jnp.stack([_gate_up_inv(grads[l]['conv']) for l in range(DEP)])
    small_g = dict(
        loss=loss_blk[0:1, 0:1], meta=dh0[:NM], g_mix_pre=stack('g_mix_pre', (D,)),
        b_f=stack('b_f', (LANES,))[:, :cfg.NH], g_sb=stack('g_sb', (cfg.NH, HD)), g_fox=stack('g_fox', (cfg.NH, HD)),
        g_mix_post=stack('g_mix_post', (D,)), g_ffn_pre=stack('g_ffn_pre', (D,)),
        conv_w=conv_g[:, 0:3], conv_b=conv_g[:, 3], g_ffn_post=stack('g_ffn_post', (D,)))
    keys = list(small_g)
    red = dict(zip(keys, _unpack(all_reduce_small(_pack([small_g[k] for k in keys]), "reduce_small"),
                                 [small_g[k].shape for k in keys])))
    loss = red['loss'].reshape(())
    red['meta'] = lax.dynamic_slice(red['meta'], (0, chip * (D // NCHIP)), (NM, D // NCHIP))
    red['conv_w'] = lax.dynamic_slice(red['conv_w'], (0, 0, chip * (cfg.F2 // NCHIP)), (DEP, 3, cfg.F2 // NCHIP))

    recv = io.recv
    order = ['w_up', 'w_in', 'w_down', 'w_out']
    part, other = {}, {}
    for k, n in enumerate(order):
        if k == 0:
            part[n] = sum_slots(recv[n], f"sum_{n}")
        else:
            part[n], other[order[k - 1]] = sum_slots(recv[n], f"sum_{n}", swap=part[order[k - 1]])
    other[order[-1]], = sibling_exchange([part[order[-1]]], "sibling_grads")

    outs = {}
    for n in BIG:
        shp = w[n].shape
        s2 = (shp[0] * shp[1], shp[2])
        res = adamw(w[n].reshape(s2), m[n].reshape(s2), v[n].reshape(s2), [part[n], other[n]], f"adamw_{n}")
        outs[n] = [r.reshape(shp) for r in res]

    for n in SMALL:
        shp = w[n].shape
        s2 = (shp[0], math.prod(shp[1:]))
        res = adamw(w[n].reshape(s2), m[n].reshape(s2), v[n].reshape(s2), [red[n].reshape(s2)], f"adamw_{n}")
        outs[n] = [r.reshape(shp) for r in res]

    grad_x = dh0[NM:L][None]
    return (loss, grad_x, *[outs[n][0] for n in WEIGHTS], *[outs[n][1] for n in WEIGHTS],
            *[outs[n][2] for n in WEIGHTS], *[outs[n][3] for n in WEIGHTS])


def kernel(x, meta, g_mix_pre, w_in, b_f, g_sb, g_fox, w_out, g_mix_post, g_ffn_pre, w_up, conv_w, conv_b, w_down, g_ffn_post, loss_target, m_meta, m_g_mix_pre, m_w_in, m_b_f, m_g_sb, m_g_fox, m_w_out, m_g_mix_post, m_g_ffn_pre, m_w_up, m_conv_w, m_conv_b, m_w_down, m_g_ffn_post, v_meta, v_g_mix_pre, v_w_in, v_b_f, v_g_sb, v_g_fox, v_w_out, v_g_mix_post, v_g_ffn_pre, v_w_up, v_conv_w, v_conv_b, v_w_down, v_g_ffn_post):
    w = dict(zip(WEIGHTS, (meta, g_mix_pre, w_in, b_f, g_sb, g_fox, w_out, g_mix_post, g_ffn_pre, w_up, conv_w,
                           conv_b, w_down, g_ffn_post)))
    m = dict(zip(WEIGHTS, (m_meta, m_g_mix_pre, m_w_in, m_b_f, m_g_sb, m_g_fox, m_w_out, m_g_mix_post, m_g_ffn_pre,
                           m_w_up, m_conv_w, m_conv_b, m_w_down, m_g_ffn_post)))
    v = dict(zip(WEIGHTS, (v_meta, v_g_mix_pre, v_w_in, v_b_f, v_g_sb, v_g_fox, v_w_out, v_g_mix_post, v_g_ffn_pre,
                           v_w_up, v_conv_w, v_conv_b, v_w_down, v_g_ffn_post)))
    return _step(PROD, x, w, loss_target, m, v)
```

```python
import functools
import math
from typing import NamedTuple

import jax
import jax.numpy as jnp
from jax import lax
from jax.experimental import pallas as pl
from jax.experimental.pallas import tpu as pltpu

F32 = jnp.float32
MXU = jnp.bfloat16
HD = 128
LANES = 128
EPS = 1e-6
NEG = -1e30
LK_PIECES = 2
ADAM_LR, ADAM_B1, ADAM_B2, ADAM_EPS, ADAM_WD, ADAM_STEP = 0.001, 0.9, 0.999, 1e-08, 0.01, 10
VMEM_LIMIT = 56 * 1024 * 1024
MESH = pl.DeviceIdType.MESH
NCHIP = 4

WEIGHTS = ['meta', 'g_mix_pre', 'w_in', 'b_f', 'g_sb', 'g_fox', 'w_out', 'g_mix_post', 'g_ffn_pre',
           'w_up', 'conv_w', 'conv_b', 'w_down', 'g_ffn_post']
BIG = ['w_in', 'w_out', 'w_up', 'w_down']
SMALL = [n for n in WEIGHTS if n not in BIG]

NT = (((1,), (1,)), ((), ()))
TN = (((0,), (0,)), ((), ()))
NN = (((1,), (0,)), ((), ()))


class Cfg(NamedTuple):
    D: int
    SEQ: int
    NMETA: int
    NH: int
    F: int
    LP: int
    tq: int
    tr: int
    FB: int
    DEPTH: int = 2
    tc: int = 0

    @property
    def conv_rows(self): return self.tc or self.tr
    @property
    def L(self): return self.SEQ + self.NMETA
    @property
    def WG(self): return self.NH * HD
    @property
    def WMIX(self): return 2 * self.WG
    @property
    def NQKV(self): return 6 * self.WG
    @property
    def N_IN(self): return self.NQKV + self.NH
    @property
    def NEXT(self): return self.NQKV + LANES
    @property
    def F2(self): return 2 * self.F


PROD = Cfg(D=2048, SEQ=4096, NMETA=16, NH=8, F=5632, LP=4224, tq=384, tr=384, FB=512, tc=704)


def _tile(n, cap, mult=LANES):
    best = None
    for t in range(mult, min(n, cap) + 1, mult):
        if n % t == 0:
            best = t
    return best if best is not None else n


def _cp(sem):
    return pltpu.CompilerParams(dimension_semantics=sem, vmem_limit_bytes=VMEM_LIMIT)


def _split_dot(x, tri, pieces):
    acc, r = None, x
    for p in range(pieces):
        xp = r.astype(MXU)
        d = jnp.dot(xp, tri, preferred_element_type=F32)
        acc = d if acc is None else acc + d
        if p + 1 < pieces:
            r = r - xp.astype(F32)
    return acc


def _split_dot_left(tri, x, pieces):
    acc, r = None, x
    for p in range(pieces):
        xp = r.astype(MXU)
        d = jnp.dot(tri, xp, preferred_element_type=F32)
        acc = d if acc is None else acc + d
        if p + 1 < pieces:
            r = r - xp.astype(F32)
    return acc


def _softplus(z):
    return jnp.maximum(z, 0.0) + jnp.log(1.0 + jnp.exp(-jnp.abs(z)))


def _shape2(x):
    return tuple(x.shape) if x.ndim == 2 else (x.shape[1], x.shape[0] * x.shape[2])


def _split_width(x):
    return x.shape[-1]


def _spec2(x, rb, cb, idx):
    if len(x.shape) == 2:
        return pl.BlockSpec((rb, cb), idx)
    per = x.shape[2] // cb

    def im(i, j, k):
        ri, ci = idx(i, j, k)
        return ci // per, ri, ci % per

    return pl.BlockSpec((None, rb, cb), im)


def matmul(a, b, mode, out_dtype, name, tm_cap=1408, tn_cap=1024, tk_cap=8192, out_split=None, xfer=None):
    (K, M) = _shape2(a) if mode == 'tn' else _shape2(a)[::-1]
    N = _shape2(b)[0] if mode == 'nt' else _shape2(b)[1]
    n_unit = math.gcd(N // (out_split or 1), _split_width(b) if mode != 'nt' else N)
    k_unit = math.gcd(_split_width(a) if mode != 'tn' else K, _split_width(b) if mode == 'nt' else K)
    tm = _tile(M, tm_cap, LANES if mode == 'tn' else 8)
    tn, tk = _tile(n_unit, tn_cap), _tile(k_unit, tk_cap)
    nk = K // tk
    dn = {'nn': NN, 'nt': NT, 'tn': TN}[mode]

    def body(a_ref, b_ref, o_ref, *scratch):
        d = lax.dot_general(a_ref[...], b_ref[...], dn, preferred_element_type=F32)
        if nk == 1:
            o_ref[...] = d.astype(out_dtype)
        else:
            acc_ref, = scratch
            k = pl.program_id(2)

            @pl.when(k == 0)
            def _():
                acc_ref[...] = d

            @pl.when(k > 0)
            def _():
                acc_ref[...] += d

            @pl.when(k == nk - 1)
            def _():
                o_ref[...] = acc_ref[...].astype(out_dtype)

    a_spec = (_spec2(a, tk, tm, lambda i, j, k: (k, i)) if mode == 'tn'
              else _spec2(a, tm, tk, lambda i, j, k: (i, k)))
    b_spec = (_spec2(b, tn, tk, lambda i, j, k: (j, k)) if mode == 'nt'
              else _spec2(b, tk, tn, lambda i, j, k: (k, j)))
    out = (jax.ShapeDtypeStruct((M, N), out_dtype) if out_split is None
           else jax.ShapeDtypeStruct((out_split, M, N // out_split), out_dtype))
    (res,), got = call_with_exchange(
        body, name, (M // tm, N // tn, nk), [a_spec, b_spec], (_spec2(out, tm, tn, lambda i, j, k: (i, j)),),
        (out,), [] if nk == 1 else [pltpu.VMEM((tm, tn), F32)], (a, b), xfer)
    return res if xfer is None else (res, got)


def _rstd(x):
    return lax.rsqrt(jnp.mean(x * x, axis=-1, keepdims=True) + EPS)


def pre_norm(cfg, h, g, name):
    LP, D, tr = cfg.LP, cfg.D, cfg.tr

    def body(h_ref, g_ref, u_ref):
        x = h_ref[...]
        u_ref[...] = ((x * _rstd(x)) * g_ref[...]).astype(MXU)

    row = pl.BlockSpec((tr, D), lambda i: (i, 0))
    vec = pl.BlockSpec((1, D), lambda i: (0, 0))
    return pl.pallas_call(body, name=name, out_shape=jax.ShapeDtypeStruct((LP, D), MXU), grid=(LP // tr,),
                          in_specs=[row, vec], out_specs=row, compiler_params=_cp(("parallel",)))(h, g)


def resid_norm(cfg, y, h, g_post, g_next, name):
    LP, D, tr = cfg.LP, cfg.D, cfg.tr

    def body(y_ref, h_ref, gp_ref, gn_ref, hn_ref, u_ref):
        yv = y_ref[...]
        hn = h_ref[...] + (yv * _rstd(yv)) * gp_ref[...]
        hn_ref[...] = hn
        u_ref[...] = ((hn * _rstd(hn)) * gn_ref[...]).astype(MXU)

    row = pl.BlockSpec((tr, D), lambda i: (i, 0))
    vec = pl.BlockSpec((1, D), lambda i: (0, 0))
    return pl.pallas_call(
        body, name=name,
        out_shape=(jax.ShapeDtypeStruct((LP, D), F32), jax.ShapeDtypeStruct((LP, D), MXU)),
        grid=(LP // tr,), in_specs=[row, row, vec, vec], out_specs=(row, row),
        compiler_params=_cp(("parallel",)))(y, h, g_post, g_next)


def loss_head(cfg, h, target, name):
    LP, D, tr = cfg.LP, cfg.D, cfg.tr
    lo, hi = cfg.NMETA, cfg.L

    def body(h_ref, t_ref, dh_ref, loss_ref):
        i = pl.program_id(0)
        rows = lax.broadcasted_iota(jnp.int32, (tr, D), 0) + i * tr
        diff = jnp.where((rows >= lo) & (rows < hi), h_ref[...] - t_ref[...], 0.0)
        dh_ref[...] = diff * (1.0 / D)
        part = 0.5 * jnp.sum(jnp.sum(diff * diff, axis=1, keepdims=True), axis=0, keepdims=True) * (1.0 / D)

        @pl.when(i == 0)
        def _():
            loss_ref[...] = jnp.zeros_like(loss_ref)

        loss_ref[...] += jnp.broadcast_to(part, loss_ref.shape)

    row = pl.BlockSpec((tr, D), lambda i: (i, 0))
    return pl.pallas_call(
        body, name=name,
        out_shape=(jax.ShapeDtypeStruct((LP, D), F32), jax.ShapeDtypeStruct((8, LANES), F32)),
        grid=(LP // tr,), in_specs=[row, row],
        out_specs=(row, pl.BlockSpec((8, LANES), lambda i: (0, 0))),
        compiler_params=_cp(("arbitrary",)))(h, target)


def norm_bwd(cfg, x, g, dy, dres, out_dtype, name):
    LP, D, tr = cfg.LP, cfg.D, cfg.tr
    has_res = dres is not None

    def body(*refs):
        if has_res:
            x_ref, g_ref, dy_ref, dres_ref, dx_ref, dg_ref = refs
        else:
            x_ref, g_ref, dy_ref, dx_ref, dg_ref = refs
        xv, dyv = x_ref[...], dy_ref[...]
        r = _rstd(xv)
        xhat = xv * r
        gdy = dyv * g_ref[...]
        dx = r * (gdy - xhat * jnp.mean(gdy * xhat, axis=-1, keepdims=True))
        if has_res:
            dx = dx + dres_ref[...]
        dx_ref[...] = dx.astype(out_dtype)

        @pl.when(pl.program_id(0) == 0)
        def _():
            dg_ref[...] = jnp.zeros_like(dg_ref)

        dg_ref[...] += jnp.sum(dyv * xhat, axis=0, keepdims=True)

    row = pl.BlockSpec((tr, D), lambda i: (i, 0))
    vec = pl.BlockSpec((1, D), lambda i: (0, 0))
    ins = [x, g, dy] + ([dres] if has_res else [])
    return pl.pallas_call(
        body, name=name,
        out_shape=(jax.ShapeDtypeStruct((LP, D), out_dtype), jax.ShapeDtypeStruct((1, D), F32)),
        grid=(LP // tr,), in_specs=[row, vec, row] + ([row] if has_res else []), out_specs=(row, vec),
        compiler_params=_cp(("arbitrary",)))(*ins)


HALO = 8


def _conv3(ext, w, b):
    s1, s2 = pltpu.roll(ext, 1, 0), pltpu.roll(ext, 2, 0)
    return w[0:1, :] * s2 + w[1:2, :] * s1 + w[2:3, :] * ext + b, s1, s2


def conv_act_fwd(cfg, a, cw, cb, name):
    LP, F, FB, tm = cfg.LP, cfg.F, cfg.FB, cfg.conv_rows
    nb = tm // HALO

    def body(a_ref, prev_ref, w_ref, b_ref, act_ref):
        i = pl.program_id(1)
        c = []
        for hh in range(2):
            prev = jnp.where(i > 0, prev_ref[hh], 0.0)
            ext = jnp.concatenate([prev, a_ref[hh]], axis=0)
            c.append(_conv3(ext, w_ref[hh], b_ref[hh])[0][HALO:, :])
        act_ref[...] = (c[0] * jax.nn.sigmoid(c[0]) * c[1]).astype(MXU)

    return pl.pallas_call(
        body, name=name, out_shape=jax.ShapeDtypeStruct((LP, F), MXU), grid=(F // FB, LP // tm),
        in_specs=[pl.BlockSpec((2, tm, FB), lambda j, i: (0, i, j)),
                  pl.BlockSpec((2, HALO, FB), lambda j, i: (0, jnp.maximum(i * nb - 1, 0), j)),
                  pl.BlockSpec((2, 3, FB), lambda j, i: (0, 0, j)),
                  pl.BlockSpec((2, 1, FB), lambda j, i: (0, 0, j))],
        out_specs=pl.BlockSpec((tm, FB), lambda j, i: (i, j)),
        compiler_params=_cp(("parallel", "parallel")))(a, a, cw, cb)


def conv_act_bwd(cfg, a, d_act, cw, cb, name):
    LP, F, FB, tm = cfg.LP, cfg.F, cfg.FB, cfg.conv_rows
    nb, last = tm // HALO, LP // tm - 1
    nrow = LP // HALO
    n = tm + 2 * HALO

    def body(a_ref, aprev_ref, anext_ref, d_ref, dnext_ref, w_ref, b_ref, da_ref, dcv_ref):
        i = pl.program_id(1)

        @pl.when(i == 0)
        def _():
            dcv_ref[...] = jnp.zeros_like(dcv_ref)

        c, exts = [], []
        for hh in range(2):
            prev = jnp.where(i > 0, aprev_ref[hh], 0.0)
            ext = jnp.concatenate([prev, a_ref[hh], anext_ref[hh]], axis=0)
            chh, s1, s2 = _conv3(ext, w_ref[hh], b_ref[hh])
            c.append(chh)
            exts.append((ext, s1, s2))
        dnext = jnp.where(i < last, dnext_ref[...], 0.0)
        dact = jnp.concatenate([jnp.zeros((HALO, FB), F32), d_ref[...], dnext], axis=0)
        sg = jax.nn.sigmoid(c[0])
        d_c = [dact * c[1] * (sg * (1.0 + c[0] * (1.0 - sg))), dact * (c[0] * sg)]
        for hh in range(2):
            dc, w = d_c[hh], w_ref[hh]
            da = w[2:3, :] * dc + w[1:2, :] * pltpu.roll(dc, n - 1, 0) + w[0:1, :] * pltpu.roll(dc, n - 2, 0)
            da_ref[hh] = da[HALO:HALO + tm, :].astype(MXU)
            dcb = dc[HALO:HALO + tm, :]
            ext, s1, s2 = exts[hh]
            dcv_ref[hh, 0:1, :] += jnp.sum(dcb * s2[HALO:HALO + tm, :], axis=0, keepdims=True)
            dcv_ref[hh, 1:2, :] += jnp.sum(dcb * s1[HALO:HALO + tm, :], axis=0, keepdims=True)
            dcv_ref[hh, 2:3, :] += jnp.sum(dcb * ext[HALO:HALO + tm, :], axis=0, keepdims=True)
            dcv_ref[hh, 3:4, :] += jnp.sum(dcb, axis=0, keepdims=True)

    return pl.pallas_call(
        body, name=name,
        out_shape=(jax.ShapeDtypeStruct((2, LP, F), MXU), jax.ShapeDtypeStruct((2, 8, F), F32)),
        grid=(F // FB, LP // tm),
        in_specs=[pl.BlockSpec((2, tm, FB), lambda j, i: (0, i, j)),
                  pl.BlockSpec((2, HALO, FB), lambda j, i: (0, jnp.maximum(i * nb - 1, 0), j)),
                  pl.BlockSpec((2, HALO, FB), lambda j, i: (0, jnp.minimum((i + 1) * nb, nrow - 1), j)),
                  pl.BlockSpec((tm, FB), lambda j, i: (i, j)),
                  pl.BlockSpec((HALO, FB), lambda j, i: (jnp.minimum((i + 1) * nb, nrow - 1), j)),
                  pl.BlockSpec((2, 3, FB), lambda j, i: (0, 0, j)),
                  pl.BlockSpec((2, 1, FB), lambda j, i: (0, 0, j))],
        out_specs=(pl.BlockSpec((2, tm, FB), lambda j, i: (0, i, j)),
                   pl.BlockSpec((2, 8, FB), lambda j, i: (0, 0, j))),
        compiler_params=_cp(("parallel", "arbitrary")))(a, a, a, d_act, d_act, cw, cb)


def fox_gate_fwd(cfg, f_logit, b_pad, name):
    LP, tb = cfg.LP, cfg.tq

    def body(f_ref, b_ref, c_ref, carry_ref):
        @pl.when(pl.program_id(0) == 0)
        def _():
            carry_ref[...] = jnp.zeros_like(carry_ref)

        xv = f_ref[...] + b_ref[...]
        lf = -_softplus(-xv)
        r = lax.broadcasted_iota(jnp.int32, (tb, tb), 0)
        s = lax.broadcasted_iota(jnp.int32, (tb, tb), 1)
        c = _split_dot_left((s <= r).astype(MXU), lf, 3) + carry_ref[0:1, :]
        c_ref[...] = c
        carry_ref[0:1, :] = c[tb - 1:tb, :]

    blk = pl.BlockSpec((tb, LANES), lambda i: (i, 0))
    return pl.pallas_call(
        body, name=name, out_shape=jax.ShapeDtypeStruct((LP, LANES), F32), grid=(LP // tb,),
        in_specs=[blk, pl.BlockSpec((1, LANES), lambda i: (0, 0))], out_specs=blk,
        scratch_shapes=[pltpu.VMEM((8, LANES), F32)], compiler_params=_cp(("arbitrary",)))(f_logit, b_pad)


def fox_gate_bwd(cfg, dc, f_logit, b_pad, name):
    LP, tb = cfg.LP, cfg.tq
    nblk = LP // tb

    def body(dc_ref, f_ref, b_ref, df_ref, db_ref, carry_ref):
        @pl.when(pl.program_id(0) == 0)
        def _():
            carry_ref[...] = jnp.zeros_like(carry_ref)
            db_ref[...] = jnp.zeros_like(db_ref)

        r = lax.broadcasted_iota(jnp.int32, (tb, tb), 0)
        s = lax.broadcasted_iota(jnp.int32, (tb, tb), 1)
        dlf = _split_dot_left((s >= r).astype(MXU), dc_ref[...], 3) + carry_ref[0:1, :]
        carry_ref[0:1, :] = dlf[0:1, :]
        df = dlf * jax.nn.sigmoid(-(f_ref[...] + b_ref[...]))
        df_ref[...] = df.astype(MXU)
        db_ref[...] += jnp.sum(df, axis=0, keepdims=True)

    blk = pl.BlockSpec((tb, LANES), lambda i: (nblk - 1 - i, 0))
    vec = pl.BlockSpec((1, LANES), lambda i: (0, 0))
    return pl.pallas_call(
        body, name=name,
        out_shape=(jax.ShapeDtypeStruct((LP, LANES), MXU), jax.ShapeDtypeStruct((1, LANES), F32)),
        grid=(nblk,), in_specs=[blk, blk, vec], out_specs=(blk, vec),
        scratch_shapes=[pltpu.VMEM((8, LANES), F32)], compiler_params=_cp(("arbitrary",)))(dc, f_logit, b_pad)


def _head_norm_fwd(o, g):
    return (o * lax.rsqrt(jnp.mean(o * o, axis=-1, keepdims=True) + EPS)) * g


def _head_norm_bwd(o, g, d_on):
    r = lax.rsqrt(jnp.mean(o * o, axis=-1, keepdims=True) + EPS)
    ohat = o * r
    gdy = d_on * g
    d_o = r * (gdy - ohat * jnp.mean(gdy * ohat, axis=-1, keepdims=True))
    return d_o, jnp.sum(d_on * ohat, axis=0, keepdims=True)


def _diag_masks(tq, strict):
    rows = lax.broadcasted_iota(jnp.int32, (tq, HD), 0)
    cols = lax.broadcasted_iota(jnp.int32, (tq, HD), 1)
    return [(cols + kk * HD < rows) if strict else (cols + kk * HD <= rows) for kk in range(tq // HD)]


def _walk_groups(i, R, group, carry, masks, descending):
    big = 2 * R
    if descending:
        carry = group(i * R, R, carry, masks)
        carry = lax.fori_loop(0, i % 2, lambda t, c: group((i - 1) * R, R, c, None), carry)
        return lax.fori_loop(0, i // 2, lambda t, c: group((i // 2 - 1 - t) * big, big, c, None), carry)
    carry = lax.fori_loop(0, i // 2, lambda t, c: group(t * big, big, c, None), carry)
    carry = lax.fori_loop(0, i % 2, lambda t, c: group((i // 2) * big, R, c, None), carry)
    return group(i * R, R, carry, masks)


def _tri(pred):
    a = lax.broadcasted_iota(jnp.int32, (HD, HD), 0)
    b = lax.broadcasted_iota(jnp.int32, (HD, HD), 1)
    return pred(a, b).astype(MXU)


def _attn_specs(cfg, group):
    base = 3 * cfg.NH * group
    return base, base + cfg.NH, base + 2 * cfg.NH


def sb_fwd(cfg, qkv, g3, name, xfer=None):
    LP, NH, tq = cfg.LP, cfg.NH, cfg.tq
    nq, R = LP // tq, tq // HD
    scale = HD ** -0.5
    cq, ck, cv = _attn_specs(cfg, 0)

    def body(q_ref, k_ref, v_ref, g_ref, opre_ref, on_ref, rtot_ref):
        i = pl.program_id(1)
        q = q_ref[...]
        masks = _diag_masks(tq, True)
        m_after = _tri(lambda a, b: a > b)

        def group(j0, n, carry, mk):
            acc, rc = carry
            masked = mk is not None
            order = range(n - 1, -1, -1)
            offs = [pl.multiple_of((j0 + kk) * HD, HD) for kk in range(n)]
            zs = [lax.dot_general(q, k_ref[pl.ds(offs[kk], HD), :], NT, preferred_element_type=F32) * scale
                  for kk in range(n)]
            ls, lks, tris = [None] * n, [None] * n, [None] * n
            for kk in order:
                sp = _softplus(zs[kk])
                lks[kk] = jnp.where(mk[kk], -sp, 0.0) if masked else -sp
                tris[kk] = _split_dot(lks[kk], m_after, LK_PIECES)
                ls[kk] = zs[kk] - sp
            for kk in order:
                a = jnp.exp(ls[kk] + (tris[kk] + rc))
                if masked:
                    a = jnp.where(mk[kk], a, 0.0)
                acc = acc + jnp.dot(a.astype(MXU), v_ref[pl.ds(offs[kk], HD), :], preferred_element_type=F32)
                rc = rc + jnp.sum(lks[kk], axis=1, keepdims=True)
            return acc, rc

        acc, rc = _walk_groups(i, R, group, (jnp.zeros((tq, HD), F32), jnp.zeros((tq, 1), F32)), masks, True)
        opre_ref[...] = acc
        on_ref[...] = _head_norm_fwd(acc, g_ref[0]).astype(MXU)
        rtot_ref[0] = rc

    return call_with_exchange(
        body, name, (NH, nq),
        [pl.BlockSpec((tq, HD), lambda h, i: (i, cq + h)),
         pl.BlockSpec((LP, HD), lambda h, i: (0, ck + h)),
         pl.BlockSpec((LP, HD), lambda h, i: (0, cv + h)),
         pl.BlockSpec((1, 1, HD), lambda h, i: (h, 0, 0))],
        (pl.BlockSpec((tq, HD), lambda h, i: (i, h)),
         pl.BlockSpec((tq, HD), lambda h, i: (i, h)),
         pl.BlockSpec((1, tq, 1), lambda h, i: (h, i, 0))),
        (jax.ShapeDtypeStruct((LP, cfg.WG), F32), jax.ShapeDtypeStruct((LP, cfg.WG), MXU),
         jax.ShapeDtypeStruct((NH, LP, 1), F32)),
        [], (qkv, qkv, qkv, g3), xfer)


def sb_bwd(cfg, qkv, g3, o_pre, d_on, rtot, name, xfer=None):
    LP, NH, tq = cfg.LP, cfg.NH, cfg.tq
    nq, R = LP // tq, tq // HD
    scale = HD ** -0.5
    cq, ck, cv = _attn_specs(cfg, 0)

    def body(q_ref, k_ref, v_ref, g_ref, o_ref, don_ref, rtot_ref, dq_ref, dk_ref, dv_ref, dg_ref,
             dk_acc, dv_acc):
        i = pl.program_id(1)

        @pl.when(i == 0)
        def _():
            dk_acc[...] = jnp.zeros_like(dk_acc)
            dv_acc[...] = jnp.zeros_like(dv_acc)
            dg_ref[...] = jnp.zeros_like(dg_ref)

        q = q_ref[...]
        d_o, dg = _head_norm_bwd(o_ref[...], g_ref[0], don_ref[...])
        dg_ref[0] += dg
        do_b = d_o.astype(MXU)
        rt = rtot_ref[0]
        masks = _diag_masks(tq, True)
        m_le = _tri(lambda a, b: a <= b)
        m_lt = _tri(lambda a, b: a < b)

        def group(j0, n, carry, mk):
            dq, lc, pc = carry
            masked = mk is not None
            offs = [pl.multiple_of((j0 + kk) * HD, HD) for kk in range(n)]
            zs = [lax.dot_general(q, k_ref[pl.ds(offs[kk], HD), :], NT, preferred_element_type=F32) * scale
                  for kk in range(n)]
            das = [lax.dot_general(do_b, v_ref[pl.ds(offs[kk], HD), :], NT, preferred_element_type=F32)
                   for kk in range(n)]
            ls, lks, tri1 = [], [], []
            for kk in range(n):
                sp = _softplus(zs[kk])
                lk = jnp.where(mk[kk], -sp, 0.0) if masked else -sp
                tri1.append(_split_dot(lk, m_le, LK_PIECES))
                lks.append(lk)
                ls.append(zs[kk] - sp)
            ggs, a_bs, tri2 = [], [], []
            for kk in range(n):
                a = jnp.exp(ls[kk] + ((rt - lc) - tri1[kk]))
                if masked:
                    a = jnp.where(mk[kk], a, 0.0)
                gg = a * das[kk]
                tri2.append(_split_dot(gg, m_lt, 2))
                ggs.append(gg)
                a_bs.append(a.astype(MXU))
                lc = lc + jnp.sum(lks[kk], axis=1, keepdims=True)
            for kk in range(n):
                sig = jnp.exp(ls[kk])
                dz = (ggs[kk] * (1.0 - sig) - sig * (pc + tri2[kk])) * scale
                if masked:
                    dz = jnp.where(mk[kk], dz, 0.0)
                dz_b = dz.astype(MXU)
                dq = dq + jnp.dot(dz_b, k_ref[pl.ds(offs[kk], HD), :], preferred_element_type=F32)
                dk_acc[pl.ds(offs[kk], HD), :] += lax.dot_general(dz_b, q, TN, preferred_element_type=F32)
                dv_acc[pl.ds(offs[kk], HD), :] += lax.dot_general(a_bs[kk], do_b, TN, preferred_element_type=F32)
                pc = pc + jnp.sum(ggs[kk], axis=1, keepdims=True)
            return dq, lc, pc

        zc = jnp.zeros((tq, 1), F32)
        dq, _, _ = _walk_groups(i, R, group, (jnp.zeros((tq, HD), F32), zc, zc), masks, False)
        dq_ref[...] = dq.astype(MXU)

        @pl.when(i == nq - 1)
        def _():
            dk_ref[...] = dk_acc[...].astype(MXU)
            dv_ref[...] = dv_acc[...].astype(MXU)

    blk = pl.BlockSpec((tq, HD), lambda h, i: (i, h))
    full = pl.BlockSpec((LP, HD), lambda h, i: (0, h))
    gspec = pl.BlockSpec((1, 1, HD), lambda h, i: (h, 0, 0))
    return call_with_exchange(
        body, name, (NH, nq),
        [pl.BlockSpec((tq, HD), lambda h, i: (i, cq + h)),
         pl.BlockSpec((LP, HD), lambda h, i: (0, ck + h)),
         pl.BlockSpec((LP, HD), lambda h, i: (0, cv + h)),
         gspec, blk, blk, pl.BlockSpec((1, tq, 1), lambda h, i: (h, i, 0))],
        (blk, full, full, gspec),
        (jax.ShapeDtypeStruct((LP, cfg.WG), MXU),) * 3 + (jax.ShapeDtypeStruct((NH, 1, HD), F32),),
        [pltpu.VMEM((LP, HD), F32), pltpu.VMEM((LP, HD), F32)],
        (qkv, qkv, qkv, g3, o_pre, d_on, rtot), xfer)


def fox_fwd(cfg, qkv, g3, c_col, c_row, name, xfer=None):
    LP, NH, tq = cfg.LP, cfg.NH, cfg.tq
    nq, R = LP // tq, tq // HD
    scale = HD ** -0.5
    cq, ck, cv = _attn_specs(cfg, 1)

    def body(q_ref, k_ref, v_ref, g_ref, ccol_ref, crow_ref, opre_ref, on_ref, lse_ref):
        i = pl.program_id(1)
        q = q_ref[...]
        cq_b = jnp.broadcast_to(ccol_ref[0], (tq, HD))
        masks = _diag_masks(tq, False)

        def group(j0, n, carry, mk):
            acc, m, l = carry
            ss, offs = [], []
            for kk in range(n):
                j = j0 + kk
                off = pl.multiple_of(j * HD, HD)
                s = (lax.dot_general(q, k_ref[pl.ds(off, HD), :], NT, preferred_element_type=F32) * scale
                     + (cq_b - crow_ref[0, pl.ds(j, 1), :]))
                ss.append(s if mk is None else jnp.where(mk[kk], s, NEG))
                offs.append(off)
            mx = jnp.max(ss[0], axis=1, keepdims=True)
            for s in ss[1:]:
                mx = jnp.maximum(mx, jnp.max(s, axis=1, keepdims=True))
            m_new = jnp.maximum(m, mx)
            alpha = jnp.exp(m - m_new)
            acc, l = alpha * acc, alpha * l
            m_b = jnp.broadcast_to(m_new, (tq, HD))
            for s, off in zip(ss, offs):
                p = jnp.exp(s - m_b)
                l = l + jnp.sum(p, axis=1, keepdims=True)
                acc = acc + _split_dot(p, v_ref[pl.ds(off, HD), :], 2)
            return acc, m_new, l

        carry = (jnp.zeros((tq, HD), F32), jnp.full((tq, 1), NEG, F32), jnp.zeros((tq, 1), F32))
        acc, m, l = _walk_groups(i, R, group, carry, masks, False)
        o = acc / l
        opre_ref[...] = o
        on_ref[...] = _head_norm_fwd(o, g_ref[0]).astype(MXU)
        lse_ref[0] = m + jnp.log(l)

    return call_with_exchange(
        body, name, (NH, nq),
        [pl.BlockSpec((tq, HD), lambda h, i: (i, cq + h)),
         pl.BlockSpec((LP, HD), lambda h, i: (0, ck + h)),
         pl.BlockSpec((LP, HD), lambda h, i: (0, cv + h)),
         pl.BlockSpec((1, 1, HD), lambda h, i: (h, 0, 0)),
         pl.BlockSpec((1, tq, 1), lambda h, i: (h, i, 0)),
         pl.BlockSpec((1, LP // HD, HD), lambda h, i: (h, 0, 0))],
        (pl.BlockSpec((tq, HD), lambda h, i: (i, h)),
         pl.BlockSpec((tq, HD), lambda h, i: (i, h)),
         pl.BlockSpec((1, tq, 1), lambda h, i: (h, i, 0))),
        (jax.ShapeDtypeStruct((LP, cfg.WG), F32), jax.ShapeDtypeStruct((LP, cfg.WG), MXU),
         jax.ShapeDtypeStruct((NH, LP, 1), F32)),
        [], (qkv, qkv, qkv, g3, c_col, c_row), xfer)


def fox_bwd(cfg, qkv, g3, c_col, c_row, o_pre, d_on, lse, name, xfer=None):
    LP, NH, tq = cfg.LP, cfg.NH, cfg.tq
    nq, R = LP // tq, tq // HD
    scale = HD ** -0.5
    cq, ck, cv = _attn_specs(cfg, 1)

    def body(q_ref, k_ref, v_ref, g_ref, ccol_ref, crow_ref, o_ref, don_ref, lse_ref,
             dq_ref, dk_ref, dv_ref, dg_ref, dc_ref, dk_acc, dv_acc):
        i = pl.program_id(1)

        @pl.when(i == 0)
        def _():
            dk_acc[...] = jnp.zeros_like(dk_acc)
            dv_acc[...] = jnp.zeros_like(dv_acc)
            dg_ref[...] = jnp.zeros_like(dg_ref)
            dc_ref[...] = jnp.zeros_like(dc_ref)

        q = q_ref[...]
        ov = o_ref[...]
        d_o, dg = _head_norm_bwd(ov, g_ref[0], don_ref[...])
        dg_ref[0] += dg
        do_b = d_o.astype(MXU)
        delta = jnp.sum(do_b.astype(F32) * ov, axis=1, keepdims=True)
        cqv, lsev = ccol_ref[0], lse_ref[0]
        cl = cqv - lsev
        masks = _diag_masks(tq, False)

        def group(j0, n, dq, mk):
            offs = [pl.multiple_of((j0 + kk) * HD, HD) for kk in range(n)]
            ss = [lax.dot_general(q, k_ref[pl.ds(offs[kk], HD), :], NT, preferred_element_type=F32) * scale
                  + (cl - crow_ref[0, pl.ds(j0 + kk, 1), :]) for kk in range(n)]
            dps = [lax.dot_general(do_b, v_ref[pl.ds(offs[kk], HD), :], NT, preferred_element_type=F32)
                   for kk in range(n)]
            for kk in range(n):
                p = jnp.exp(ss[kk])
                if mk is not None:
                    p = jnp.where(mk[kk], p, 0.0)
                ds = p * (dps[kk] - delta)
                dc_ref[0, pl.ds(j0 + kk, 1), :] -= jnp.sum(ds, axis=0, keepdims=True)
                ds_b = (ds * scale).astype(MXU)
                dk_acc[pl.ds(offs[kk], HD), :] += lax.dot_general(ds_b, q, TN, preferred_element_type=F32)
                dv_acc[pl.ds(offs[kk], HD), :] += lax.dot_general(p.astype(MXU), do_b, TN,
                                                                  preferred_element_type=F32)
                dq = dq + jnp.dot(ds_b, k_ref[pl.ds(offs[kk], HD), :], preferred_element_type=F32)
            return dq

        dq = _walk_groups(i, R, group, jnp.zeros((tq, HD), F32), masks, False)
        dq_ref[...] = dq.astype(MXU)

        @pl.when(i == nq - 1)
        def _():
            dk_ref[...] = dk_acc[...].astype(MXU)
            dv_ref[...] = dv_acc[...].astype(MXU)

    blk = pl.BlockSpec((tq, HD), lambda h, i: (i, h))
    full = pl.BlockSpec((LP, HD), lambda h, i: (0, h))
    gspec = pl.BlockSpec((1, 1, HD), lambda h, i: (h, 0, 0))
    col = pl.BlockSpec((1, tq, 1), lambda h, i: (h, i, 0))
    rowv = pl.BlockSpec((1, LP // HD, HD), lambda h, i: (h, 0, 0))
    return call_with_exchange(
        body, name, (NH, nq),
        [pl.BlockSpec((tq, HD), lambda h, i: (i, cq + h)),
         pl.BlockSpec((LP, HD), lambda h, i: (0, ck + h)),
         pl.BlockSpec((LP, HD), lambda h, i: (0, cv + h)),
         gspec, col, rowv, blk, blk, col],
        (blk, full, full, gspec, rowv),
        (jax.ShapeDtypeStruct((LP, cfg.WG), MXU),) * 3
        + (jax.ShapeDtypeStruct((NH, 1, HD), F32), jax.ShapeDtypeStruct((NH, LP // HD, HD), F32)),
        [pltpu.VMEM((LP, HD), F32), pltpu.VMEM((LP, HD), F32)],
        (qkv, qkv, qkv, g3, c_col, c_row, o_pre, d_on, lse), xfer)


def _row_tile(rows, cols, n_arrays):
    budget = 24 * 1024 * 1024 // (2 * n_arrays * cols * 4)
    cap = max(8, min(rows, budget // 8 * 8))
    div = _tile(rows, cap, 8)
    return div if div <= cap and div * 4 >= cap else cap


def sum_slots(recvs, name, swap=None):
    S, rows, cols = recvs[0].shape
    nl = len(recvs)
    tr = _tile(rows, _row_tile(rows, cols, nl * S + 1), 8)
    nb = rows // tr

    def body(*refs):
        o_ref = refs[nl + (swap is not None)]
        layer, blk = pl.program_id(0), pl.program_id(1)
        if swap is not None:
            x, y, c = _coords()
            cp = pltpu.make_async_remote_copy(src_ref=refs[nl], dst_ref=refs[nl + 2], send_sem=refs[nl + 3],
                                              recv_sem=refs[nl + 4], device_id=(x, y, 1 - c), device_id_type=MESH)

            @pl.when((layer == 0) & (blk == 0))
            def _():
                cp.start()

        for ll in range(nl):
            @pl.when(layer == ll)
            def _(r_ref=refs[ll]):
                acc = r_ref[0].astype(F32)
                for s in range(1, S):
                    acc = acc + r_ref[s].astype(F32)
                o_ref[...] = acc

        if swap is not None:
            @pl.when((layer == nl - 1) & (blk == nb - 1))
            def _():
                cp.wait()

    def spec(ll):
        return pl.BlockSpec((S, tr, cols), lambda l, i: (0, jnp.where(l == ll, i, jnp.where(l < ll, 0, nb - 1)), 0))

    any_spec = pl.BlockSpec(memory_space=pl.ANY)
    out = jax.ShapeDtypeStruct((nl * rows, cols), F32)
    out_spec = pl.BlockSpec((tr, cols), lambda l, i: (l * nb + i, 0))
    extra = swap is not None
    return pl.pallas_call(
        body, name=name, grid=(nl, nb),
        out_shape=(out, jax.ShapeDtypeStruct(swap.shape, swap.dtype)) if extra else out,
        in_specs=[spec(ll) for ll in range(nl)] + ([any_spec] if extra else []),
        out_specs=(out_spec, any_spec) if extra else out_spec,
        scratch_shapes=[pltpu.SemaphoreType.DMA, pltpu.SemaphoreType.DMA] if extra else [],
        compiler_params=_cp(("arbitrary", "arbitrary")))(*recvs, *([swap] if extra else []))


def adamw(w, m, v, g_parts, name):
    rows, cols = w.shape
    npart = len(g_parts)
    tr = _row_tile(rows, cols, 7 + npart)
    c1 = 1.0 - ADAM_B1 ** ADAM_STEP
    c2 = 1.0 - ADAM_B2 ** ADAM_STEP

    def body(*refs):
        w_ref, m_ref, v_ref = refs[:3]
        g_refs = refs[3:3 + npart]
        g_out, d_out, m_out, v_out = refs[3 + npart:]
        g = g_refs[0][...]
        for r in g_refs[1:]:
            g = g + r[...]
        g_out[...] = g
        mn = ADAM_B1 * m_ref[...] + (1.0 - ADAM_B1) * g
        vn = ADAM_B2 * v_ref[...] + (1.0 - ADAM_B2) * (g * g)
        m_out[...] = mn
        v_out[...] = vn
        d_out[...] = -ADAM_LR * ((mn / c1) / (jnp.sqrt(vn / c2) + ADAM_EPS) + ADAM_WD * w_ref[...])

    blk = pl.BlockSpec((tr, cols), lambda i: (i, 0))
    return pl.pallas_call(
        body, name=name, out_shape=(jax.ShapeDtypeStruct((rows, cols), F32),) * 4,
        grid=(pl.cdiv(rows, tr),), in_specs=[blk] * (3 + npart), out_specs=(blk,) * 4,
        compiler_params=_cp(("parallel",)))(w, m, v, *g_parts)


def _coords():
    return lax.axis_index("x"), lax.axis_index("y"), lax.axis_index("c")


def chip_exchange(arrs, scatter, name):
    n = len(arrs)

    def body(*refs):
        ins, outs, sems = refs[:n], refs[n:2 * n], refs[2 * n:]
        _xfer_start(ins, outs, sems, scatter)
        _xfer_forward(ins, outs, sems, scatter)
        _xfer_finish(ins, outs, sems, scatter)

    out_shape, scratch = _xfer_shapes(arrs, scatter)
    any_spec = pl.BlockSpec(memory_space=pl.ANY)
    return pl.pallas_call(body, name=name, out_shape=out_shape, in_specs=[any_spec] * n, out_specs=(any_spec,) * n,
                          scratch_shapes=scratch)(*arrs)


def _xfer_shapes(arrs, scatter):
    n = len(arrs)
    shapes = [a.shape[1:] if scatter else a.shape for a in arrs]
    out_shape = tuple(jax.ShapeDtypeStruct((NCHIP,) + tuple(s), a.dtype) for s, a in zip(shapes, arrs))
    scratch = [pltpu.SemaphoreType.DMA((n, 6)), pltpu.SemaphoreType.DMA((n, 6)), pltpu.SemaphoreType.DMA((n,))]
    return out_shape, scratch


def _xfer_copies(ins, outs, sems, scatter):
    send_sems, recv_sems, local_sems = sems
    n = len(ins)
    x, y, c = _coords()
    me = 2 * x + y
    peers = [(1 - x, y), (x, 1 - y), (1 - x, 1 - y)]

    def rdma(src, dst, a, j, dev):
        return pltpu.make_async_remote_copy(src_ref=src, dst_ref=dst, send_sem=send_sems.at[a, j],
                                            recv_sem=recv_sems.at[a, j], device_id=dev, device_id_type=MESH)

    def half(ref, hc):
        hr = ref.shape[0] // 2
        return ref.at[pl.ds(pl.multiple_of(hc * hr, 8), hr)]

    local, sends, recvs, fwds, fwd_recvs = [], [], [], [], []
    for a in range(n):
        local.append(pltpu.make_async_copy(ins[a].at[me] if scatter else ins[a], outs[a].at[me], local_sems.at[a]))
        for k, (px, py) in enumerate(peers):
            there = 2 * px + py
            if scatter:
                sends.append(rdma(ins[a].at[there], outs[a].at[me], a, k, (px, py, c)))
                recvs.append(rdma(ins[a].at[me], outs[a].at[there], a, k, (px, py, c)))
            else:
                mine, landed = half(outs[a].at[me], c), half(outs[a].at[there], c)
                sends.append(rdma(half(ins[a], c), mine, a, k, (px, py, c)))
                recvs.append(rdma(half(ins[a], c), landed, a, k, (px, py, c)))
                fwds.append(rdma(landed, landed, a, 3 + k, (x, y, 1 - c)))
                other = half(outs[a].at[there], 1 - c)
                fwd_recvs.append(rdma(other, other, a, 3 + k, (x, y, 1 - c)))
    return local, sends, recvs, fwds, fwd_recvs


def _xfer_start(ins, outs, sems, scatter):
    local, sends, _, _, _ = _xfer_copies(ins, outs, sems, scatter)
    for cp in local + sends:
        cp.start()


def _xfer_forward(ins, outs, sems, scatter):
    _, _, recvs, fwds, _ = _xfer_copies(ins, outs, sems, scatter)
    for r, f in zip(recvs, fwds):
        r.wait_recv()
        f.start()


def _xfer_finish(ins, outs, sems, scatter):
    local, sends, recvs, fwds, fwd_recvs = _xfer_copies(ins, outs, sems, scatter)
    for cp in (recvs if scatter else fwd_recvs):
        cp.wait_recv()
    for cp in sends + fwds:
        cp.wait_send()
    for cp in local:
        cp.wait()


def call_with_exchange(core, name, grid, in_specs, out_specs, out_shape, scratch, args, xfer):
    n_in, n_out, n_scr = len(in_specs), len(out_specs), len(scratch)
    if xfer is None:
        res = pl.pallas_call(core, name=name, out_shape=out_shape, grid=grid, in_specs=in_specs,
                             out_specs=out_specs, scratch_shapes=scratch,
                             compiler_params=_cp(("arbitrary",) * len(grid)))(*args)
        return res, ()
    arrs, scatter = xfer
    nx = len(arrs)
    x_shape, x_scratch = _xfer_shapes(arrs, scatter)

    def body(*refs):
        a, xi = refs[:n_in], refs[n_in:n_in + nx]
        o, xo = refs[n_in + nx:n_in + nx + n_out], refs[n_in + nx + n_out:n_in + 2 * nx + n_out]
        rest = refs[n_in + 2 * nx + n_out:]
        scr, sems = rest[:n_scr], rest[n_scr:]
        first, late, last = None, None, None
        for d, g in enumerate(grid):
            pid = pl.program_id(d)
            f, m, l = pid == 0, pid == ((3 * g) // 4 if d == 0 else 0), pid == g - 1
            first = f if first is None else first & f
            late = m if late is None else late & m
            last = l if last is None else last & l

        @pl.when(first)
        def _():
            _xfer_start(xi, xo, sems, scatter)

        core(*a, *o, *scr)

        if not scatter:
            @pl.when(late)
            def _():
                _xfer_forward(xi, xo, sems, scatter)

        @pl.when(last)
        def _():
            _xfer_finish(xi, xo, sems, scatter)

    any_spec = pl.BlockSpec(memory_space=pl.ANY)
    res = pl.pallas_call(
        body, name=name, out_shape=tuple(out_shape) + tuple(x_shape), grid=grid,
        in_specs=list(in_specs) + [any_spec] * nx, out_specs=tuple(out_specs) + (any_spec,) * nx,
        scratch_shapes=list(scratch) + x_scratch,
        compiler_params=_cp(("arbitrary",) * len(grid)))(*args, *arrs)
    return res[:n_out], res[n_out:]


def sibling_exchange(arrs, name):
    n = len(arrs)

    def body(*refs):
        ins, outs = refs[:n], refs[n:2 * n]
        send_sems, recv_sems = refs[2 * n:]
        x, y, c = _coords()
        cps = [pltpu.make_async_remote_copy(src_ref=ins[a], dst_ref=outs[a], send_sem=send_sems.at[a],
                                            recv_sem=recv_sems.at[a], device_id=(x, y, 1 - c),
                                            device_id_type=MESH) for a in range(n)]
        for cp in cps:
            cp.start()
        for cp in cps:
            cp.wait()

    any_spec = pl.BlockSpec(memory_space=pl.ANY)
    return pl.pallas_call(
        body, name=name, out_shape=tuple(jax.ShapeDtypeStruct(a.shape, a.dtype) for a in arrs),
        in_specs=[any_spec] * n, out_specs=(any_spec,) * n,
        scratch_shapes=[pltpu.SemaphoreType.DMA((n,)), pltpu.SemaphoreType.DMA((n,))],
    )(*arrs)


def all_reduce_small(pack, name):
    R = pack.shape[0]

    def body(p_ref, o_ref, slots, send_sems, recv_sems):
        x, y, c = _coords()
        me = 4 * x + 2 * y + c
        slots[me] = p_ref[...]
        cps = []
        for k in range(1, 8):
            kx, ky, kc = (k >> 2) & 1, (k >> 1) & 1, k & 1
            peer = ((1 - x) if kx else x, (1 - y) if ky else y, (1 - c) if kc else c)
            cp = pltpu.make_async_remote_copy(src_ref=p_ref, dst_ref=slots.at[me], send_sem=send_sems.at[k],
                                              recv_sem=recv_sems.at[k], device_id=peer, device_id_type=MESH)
            cp.start()
            cps.append((cp, peer))
        for k, (cp, peer) in enumerate(cps, start=1):
            frm = 4 * peer[0] + 2 * peer[1] + peer[2]
            pltpu.make_async_remote_copy(src_ref=p_ref, dst_ref=slots.at[frm], send_sem=send_sems.at[k],
                                         recv_sem=recv_sems.at[k], device_id=peer, device_id_type=MESH).wait_recv()
        for cp, _ in cps:
            cp.wait_send()
        acc = slots[0]
        for s in range(1, 8):
            acc = acc + slots[s]
        o_ref[...] = acc

    vm = pl.BlockSpec(memory_space=pltpu.VMEM)
    return pl.pallas_call(
        body, name=name, out_shape=jax.ShapeDtypeStruct((R, LANES), F32), in_specs=[vm], out_specs=vm,
        scratch_shapes=[pltpu.VMEM((8, R, LANES), F32), pltpu.SemaphoreType.DMA((8,)),
                        pltpu.SemaphoreType.DMA((8,))],
        compiler_params=pltpu.CompilerParams(vmem_limit_bytes=VMEM_LIMIT),
    )(pack)


def _gate_up(w):
    r, c = w.shape
    return w.reshape(r, 2, c // 2).transpose(1, 0, 2)


def _gate_up_inv(w):
    return w.transpose(1, 0, 2).reshape(w.shape[1], -1)


def _pack(arrs):
    parts = []
    for a in arrs:
        f = a.reshape(-1).astype(F32)
        parts.append(jnp.pad(f, (0, -f.shape[0] % 1024)).reshape(-1, LANES))
    return jnp.concatenate(parts, axis=0)


def _unpack(p, shapes):
    out, r = [], 0
    for s in shapes:
        n = math.prod(s)
        nr = (n + 1023) // 1024 * 8
        out.append(p[r:r + nr].reshape(-1)[:n].reshape(s))
        r += nr
    return out


def _vec(g):
    return g.reshape(1, -1)


class _LocalWeights:
    def __init__(self, cfg, wf):
        self.cfg, self.wf, self.grads = cfg, wf, {}

    def w_in(self, l):
        return self.wf['w_in'][l]

    def fwd_exchanges(self, l):
        return None, None

    def rest(self, l, got_sb, got_fox):
        w_up = self.wf['w_up'][l]
        return dict(w_out=self.wf['w_out'][l], w_down=self.wf['w_down'][l],
                    w_up=w_up.reshape(w_up.shape[0], NCHIP, -1).transpose(1, 0, 2))

    def bwd_exchanges(self, l, g):
        return None, None

    def in_dw_exchange(self, l, dw_a):
        return None

    def in_dx_exchange(self, l, g):
        return None

    def bwd_done(self, l, g, got_sb, got_fox, got_a, got_in):
        self.grads[l] = g


def _w_in_parts(cfg, w_in):
    w_f = jnp.pad(w_in[:, cfg.NQKV:], ((0, 0), (0, LANES - cfg.NH)))
    return dict(w_qkv=w_in[:, :cfg.NQKV], w_f=w_f, w_in_ext=jnp.concatenate([w_in[:, :cfg.NQKV], w_f], axis=1))


def _layer_fwd(cfg, l, h, u, wl, io):
    tag = f"l{l}"
    qkv = matmul(u, wl['w_qkv'], 'nn', MXU, f"{tag}_qkv", tn_cap=1024)
    f_logit = matmul(u, wl['w_f'], 'nn', F32, f"{tag}_fproj")
    cpre = fox_gate_fwd(cfg, f_logit, wl['b_pad'], f"{tag}_gate_fwd")
    c_heads = cpre[:, :cfg.NH].T
    c_col = c_heads[:, :, None]
    c_row = c_heads.reshape(cfg.NH, cfg.LP // HD, HD)
    x_sb, x_fox = io.fwd_exchanges(l)
    (o_sb, on_sb, rtot), got_sb = sb_fwd(cfg, qkv, wl['g_sb'], f"{tag}_sb_fwd", x_sb)
    (o_fx, on_fx, lse), got_fox = fox_fwd(cfg, qkv, wl['g_fox'], c_col, c_row, f"{tag}_fox_fwd", x_fox)
    wl.update(io.rest(l, got_sb, got_fox))
    mixin = jnp.concatenate([on_sb, on_fx], axis=1)
    mix = matmul(mixin, wl['w_out'], 'nn', F32, f"{tag}_out")
    h1, u2 = resid_norm(cfg, mix, h, wl['g_mix_post'], wl['g_ffn_pre'], f"{tag}_mixres")
    a = matmul(u2, wl['w_up'], 'nn', F32, f"{tag}_up", tn_cap=1408, out_split=2)
    act = conv_act_fwd(cfg, a, wl['conv_w'], wl['conv_b'], f"{tag}_conv_fwd")
    ff = matmul(act, wl['w_down'], 'nn', F32, f"{tag}_down", tm_cap=704)
    h2, u_next = resid_norm(cfg, ff, h1, wl['g_ffn_post'], wl['g_next'], f"{tag}_ffnres")
    saved = dict(h=h, u=u, qkv=qkv, f_logit=f_logit, c_col=c_col, c_row=c_row, o_sb=o_sb, o_fx=o_fx, rtot=rtot,
                 lse=lse, mixin=mixin, mix=mix, h1=h1, u2=u2, a=a, act=act, ff=ff)
    return h2, u_next, saved


def _layer_bwd(cfg, l, dh2, wl, sv, io):
    tag = f"l{l}"
    g = {}
    d_ff, g['g_ffn_post'] = norm_bwd(cfg, sv['ff'], wl['g_ffn_post'], dh2, None, MXU, f"{tag}_ffnpost_bwd")
    d_act = matmul(d_ff, wl['w_down'], 'nt', F32, f"{tag}_down_dx")
    g['w_down'] = matmul(sv['act'], d_ff, 'tn', MXU, f"{tag}_down_dw", tm_cap=704, tk_cap=4224)
    d_a, d_conv = conv_act_bwd(cfg, sv['a'], d_act, wl['conv_w'], wl['conv_b'], f"{tag}_conv_bwd")
    g['conv'] = d_conv
    g['w_up'] = matmul(sv['u2'], d_a, 'tn', MXU, f"{tag}_up_dw", tm_cap=512, tn_cap=1408, tk_cap=4224,
                       out_split=NCHIP)
    du2 = matmul(d_a, wl['w_up'], 'nt', F32, f"{tag}_up_dx", tm_cap=704, tk_cap=5632)
    dh1, g['g_ffn_pre'] = norm_bwd(cfg, sv['h1'], wl['g_ffn_pre'], du2, dh2, F32, f"{tag}_ffnpre_bwd")
    d_mix, g['g_mix_post'] = norm_bwd(cfg, sv['mix'], wl['g_mix_post'], dh1, None, MXU, f"{tag}_mixpost_bwd")
    d_mixin = matmul(d_mix, wl['w_out'], 'nt', F32, f"{tag}_out_dx")
    g['w_out'] = matmul(sv['mixin'], d_mix, 'tn', MXU, f"{tag}_out_dw", tm_cap=1024, tk_cap=4224)
    WG = cfg.WG
    x_sb, x_fox = io.bwd_exchanges(l, g)
    (dq_s, dk_s, dv_s, g['g_sb']), got_sb = sb_bwd(cfg, sv['qkv'], wl['g_sb'], sv['o_sb'], d_mixin[:, :WG],
                                                   sv['rtot'], f"{tag}_sb_bwd", x_sb)
    (dq_f, dk_f, dv_f, g['g_fox'], dc_row), got_fox = fox_bwd(cfg, sv['qkv'], wl['g_fox'], sv['c_col'], sv['c_row'],
                                                              sv['o_fx'], d_mixin[:, WG:], sv['lse'],
                                                              f"{tag}_fox_bwd", x_fox)
    dc = jnp.pad(dc_row.reshape(cfg.NH, cfg.LP).T, ((0, 0), (0, LANES - cfg.NH)))
    d_f, g['b_f'] = fox_gate_bwd(cfg, dc, sv['f_logit'], wl['b_pad'], f"{tag}_gate_bwd")
    d_proj = jnp.concatenate([dq_s, dk_s, dv_s, dq_f, dk_f, dv_f, d_f], axis=1)
    half = cfg.D // 2
    dw_a = matmul(sv['u'][:, :half], d_proj, 'tn', MXU, f"{tag}_in_dw_a", tm_cap=1024, tn_cap=896, tk_cap=4224)
    x_a = io.in_dw_exchange(l, dw_a)
    dw_b = matmul(sv['u'][:, half:], d_proj, 'tn', MXU, f"{tag}_in_dw_b", tm_cap=1024, tn_cap=896, tk_cap=4224,
                  xfer=x_a)
    dw_b, got_a = dw_b if x_a is not None else (dw_b, None)
    g['w_in_halves'] = (dw_a, dw_b)
    x_in = io.in_dx_exchange(l, g)
    du = matmul(d_proj, wl['w_in_ext'], 'nt', F32, f"{tag}_in_dx", tm_cap=704, xfer=x_in)
    du, got_in = du if x_in is not None else (du, None)
    dh, g['g_mix_pre'] = norm_bwd(cfg, sv['h'], wl['g_mix_pre'], du, dh1, F32, f"{tag}_mixpre_bwd")
    io.bwd_done(l, g, got_sb, got_fox, got_a, got_in)
    return dh


def _local_step(cfg, h0, target_p, wf, io=None):
    io = _LocalWeights(cfg, wf) if io is None else io
    layers = []
    for l in range(cfg.DEPTH):
        layers.append(dict(
            b_pad=jnp.pad(wf['b_f'][l], (0, LANES - cfg.NH)).reshape(1, LANES),
            g_sb=wf['g_sb'][l][:, None, :], g_fox=wf['g_fox'][l][:, None, :],
            conv_w=_gate_up(wf['conv_w'][l]), conv_b=wf['conv_b'][l].reshape(2, 1, cfg.F),
            g_mix_pre=_vec(wf['g_mix_pre'][l]), g_mix_post=_vec(wf['g_mix_post'][l]),
            g_ffn_pre=_vec(wf['g_ffn_pre'][l]), g_ffn_post=_vec(wf['g_ffn_post'][l]),
            g_next=_vec(wf['g_mix_pre'][(l + 1) % cfg.DEPTH])))
    h = h0
    u = pre_norm(cfg, h0, layers[0]['g_mix_pre'], "l0_prenorm")
    saved = []
    for l in range(cfg.DEPTH):
        layers[l].update(_w_in_parts(cfg, io.w_in(l)))
        h, u, sv = _layer_fwd(cfg, l, h, u, layers[l], io)
        saved.append(sv)
    dh, loss_blk = loss_head(cfg, h, target_p, "loss_head")
    for l in reversed(range(cfg.DEPTH)):
        dh = _layer_bwd(cfg, l, dh, layers[l], saved[l], io)
    return loss_blk, dh, [io.grads[l] for l in range(cfg.DEPTH)]


def _cols(g):
    return g.transpose(1, 0, 2).reshape(g.shape[1], -1)


def _rows(g):
    return g.reshape(-1, g.shape[2])


def _send_cols(g):
    r, c = g.shape
    return g.reshape(r, NCHIP, c // NCHIP).transpose(1, 0, 2).astype(MXU)


def _send_rows(g):
    r, c = g.shape
    return g.reshape(NCHIP, r // NCHIP, c).astype(MXU)


class _StreamedWeights:
    def __init__(self, cfg, w):
        self.cfg = cfg
        self.shard = {n: [w[n][l].astype(MXU) for l in range(cfg.DEPTH)] for n in BIG}
        self.grads, self.recv = {}, {n: [None] * cfg.DEPTH for n in BIG}
        n_cw = cfg.DEPTH * 3
        conv_w = jnp.pad(w['conv_w'].reshape(n_cw, -1), ((0, -n_cw % 16), (0, 0)))
        got = chip_exchange([self.shard['w_in'][0], w['meta'], conv_w], False, "gather_first")
        self.win = {0: _cols(got[0])}
        self.meta = _cols(got[1])
        self.conv_w = _cols(got[2])[:n_cw].reshape(cfg.DEPTH, 3, cfg.F2)
        self.pending = None

    def w_in(self, l):
        return self.win[l]

    def fwd_exchanges(self, l):
        s = self.shard
        fox = [s['w_down'][l]] + ([s['w_in'][l + 1]] if l + 1 < self.cfg.DEPTH else [])
        return ([s['w_out'][l], s['w_up'][l]], False), (fox, False)

    def rest(self, l, got_sb, got_fox):
        if l + 1 < self.cfg.DEPTH:
            self.win[l + 1] = _cols(got_fox[1])
        return dict(w_out=_rows(got_sb[0]), w_up=got_sb[1], w_down=_rows(got_fox[0]))

    def bwd_exchanges(self, l, g):
        sb = [g['w_up']] + (self.pending or [])
        self.pending = None
        return (sb, True), ([_send_rows(g['w_down']), _send_rows(g['w_out'])], True)

    def in_dx_exchange(self, l, g):
        sends = [_send_cols(h[:, :self.cfg.N_IN]) for h in g['w_in_halves']]
        if l == 0:
            return [sends[1]], True
        self.pending = sends
        return None

    def in_dw_exchange(self, l, dw_a):
        return ([_send_cols(dw_a[:, :self.cfg.N_IN])], True) if l == 0 else None

    def bwd_done(self, l, g, got_sb, got_fox, got_a, got_in):
        self.grads[l] = g
        self.recv['w_up'][l] = got_sb[0]
        if len(got_sb) > 1:
            self.recv['w_in'][l + 1] = list(got_sb[1:])
        self.recv['w_down'][l], self.recv['w_out'][l] = got_fox
        if got_in is not None:
            self.recv['w_in'][l] = [got_a[0], got_in[0]]


def _step(cfg, x, w, target, m, v):
    xi, yi, ci = _coords()
    chip = 2 * xi + yi
    D, LP, L, NM = cfg.D, cfg.LP, cfg.L, cfg.NMETA
    DEP = cfg.DEPTH
    io = _StreamedWeights(cfg, w)
    wf = {n: w[n] for n in SMALL}
    meta_full, wf['conv_w'] = io.meta, io.conv_w

    zpad = jnp.zeros((LP - L, D), F32)
    h0 = jnp.concatenate([meta_full, x[0], zpad], axis=0)
    target_p = jnp.concatenate([jnp.zeros((NM, D), F32), target[0], zpad], axis=0)
    loss_blk, dh0, grads = _local_step(cfg, h0, target_p, wf, io)

    def stack(key, shape):
        return jnp.stack([grads[l][key].reshape(shape) for l in range(DEP)])

    conv_g = jnp.stack([_gate_up_inv(grads[l]['conv']) for l in range(DEP)])
    small_g = dict(
        loss=loss_blk[0:1, 0:1], meta=dh0[:NM], g_mix_pre=stack('g_mix_pre', (D,)),
        b_f=stack('b_f', (LANES,))[:, :cfg.NH], g_sb=stack('g_sb', (cfg.NH, HD)), g_fox=stack('g_fox', (cfg.NH, HD)),
        g_mix_post=stack('g_mix_post', (D,)), g_ffn_pre=stack('g_ffn_pre', (D,)),
        conv_w=conv_g[:, 0:3], conv_b=conv_g[:, 3], g_ffn_post=stack('g_ffn_post', (D,)))
    keys = list(small_g)
    red = dict(zip(keys, _unpack(all_reduce_small(_pack([small_g[k] for k in keys]), "reduce_small"),
                                 [small_g[k].shape for k in keys])))
    loss = red['loss'].reshape(())
    red['meta'] = lax.dynamic_slice(red['meta'], (0, chip * (D // NCHIP)), (NM, D // NCHIP))
    red['conv_w'] = lax.dynamic_slice(red['conv_w'], (0, 0, chip * (cfg.F2 // NCHIP)), (DEP, 3, cfg.F2 // NCHIP))

    recv = dict(io.recv, w_in=[r for halves in io.recv['w_in'] for r in halves])
    order = ['w_up', 'w_in', 'w_down', 'w_out']
    part, other = {}, {}
    for k, n in enumerate(order):
        if k == 0:
            part[n] = sum_slots(recv[n], f"sum_{n}")
        else:
            part[n], other[order[k - 1]] = sum_slots(recv[n], f"sum_{n}", swap=part[order[k - 1]])
    other[order[-1]], = sibling_exchange([part[order[-1]]], "sibling_grads")

    outs = {}
    for n in BIG:
        shp = w[n].shape
        s2 = (shp[0] * shp[1], shp[2])
        res = adamw(w[n].reshape(s2), m[n].reshape(s2), v[n].reshape(s2), [part[n], other[n]], f"adamw_{n}")
        outs[n] = [r.reshape(shp) for r in res]

    for n in SMALL:
        shp = w[n].shape
        s2 = (shp[0], math.prod(shp[1:]))
        res = adamw(w[n].reshape(s2), m[n].reshape(s2), v[n].reshape(s2), [red[n].reshape(s2)], f"adamw_{n}")
        outs[n] = [r.reshape(shp) for r in res]

    grad_x = dh0[NM:L][None]
    return (loss, grad_x, *[outs[n][0] for n in WEIGHTS], *[outs[n][1] for n in WEIGHTS],
            *[outs[n][2] for n in WEIGHTS], *[outs[n][3] for n in WEIGHTS])


def kernel(x, meta, g_mix_pre, w_in, b_f, g_sb, g_fox, w_out, g_mix_post, g_ffn_pre, w_up, conv_w, conv_b, w_down, g_ffn_post, loss_target, m_meta, m_g_mix_pre, m_w_in, m_b_f, m_g_sb, m_g_fox, m_w_out, m_g_mix_post, m_g_ffn_pre, m_w_up, m_conv_w, m_conv_b, m_w_down, m_g_ffn_post, v_meta, v_g_mix_pre, v_w_in, v_b_f, v_g_sb, v_g_fox, v_w_out, v_g_mix_post, v_g_ffn_pre, v_w_up, v_conv_w, v_conv_b, v_w_down, v_g_ffn_post):
    w = dict(zip(WEIGHTS, (meta, g_mix_pre, w_in, b_f, g_sb, g_fox, w_out, g_mix_post, g_ffn_pre, w_up, conv_w,
                           conv_b, w_down, g_ffn_post)))
    m = dict(zip(WEIGHTS, (m_meta, m_g_mix_pre, m_w_in, m_b_f, m_g_sb, m_g_fox, m_w_out, m_g_mix_post, m_g_ffn_pre,
                           m_w_up, m_conv_w, m_conv_b, m_w_down, m_g_ffn_post)))
    v = dict(zip(WEIGHTS, (v_meta, v_g_mix_pre, v_w_in, v_b_f, v_g_sb, v_g_fox, v_w_out, v_g_mix_post, v_g_ffn_pre,
                           v_w_up, v_conv_w, v_conv_b, v_w_down, v_g_ffn_post)))
    return _step(PROD, x, w, loss_target, m, v)
```
